```python
import jax, jax.numpy as jnp
from jax import lax
import numpy as np

D_MODEL = 4096
BATCH = 4
SEQ = 2048
DEPTH = 4

HEAD_DIM = 128
N_MIXERS = 4
GROUP_HEADS = D_MODEL // (N_MIXERS * HEAD_DIM)
GROUP_WIDTH = GROUP_HEADS * HEAD_DIM
MIX_WIDTH = N_MIXERS * GROUP_WIDTH
ROPE_THETA = 10000.0
EPS = 1e-6
NEG_INF = -1e30
GRID_W = 64
Q_BLOCK = 128

A_PATTERNS = ((128, 1), (512, 4), (2048, 16))
A_QBLOCK = 64
B_KV_HEADS = GROUP_HEADS // 4
C_WIN_R = 8
C_WIN_C = 16
D_Q_RANK = D_MODEL // 4
D_KV_RANK = 512
D_NOPE = 128
D_ROPE = 64
D_V = HEAD_DIM
D_QK = D_NOPE + D_ROPE
D_FF = 11008
N_EXPERTS = 8
TOP_K = 2
E_FF = D_MODEL // 2

IN_SIZES = (GROUP_WIDTH, GROUP_WIDTH, GROUP_WIDTH,
            GROUP_WIDTH, B_KV_HEADS * HEAD_DIM, B_KV_HEADS * HEAD_DIM,
            GROUP_WIDTH, GROUP_WIDTH, GROUP_WIDTH,
            D_Q_RANK, D_KV_RANK, D_ROPE)
IN_COLS = sum(IN_SIZES)

kernel_name = 'hybrid_parallel_mixer_encoder'


def rms_norm(x, g):
    x32 = x.astype(jnp.float32)
    y = x32 * lax.rsqrt(jnp.mean(x32 * x32, axis=-1, keepdims=True) + EPS)
    return (y * g.astype(jnp.float32)).astype(x.dtype)


def rope(x, pos):
    half = x.shape[-1] // 2
    inv_freq = ROPE_THETA ** (-jnp.arange(half, dtype=jnp.float32) / half)
    ang = pos.astype(jnp.float32)[:, None] * inv_freq[None, :]
    cos, sin = jnp.cos(ang), jnp.sin(ang)
    x32 = x.astype(jnp.float32)
    x1, x2 = x32[..., :half], x32[..., half:]
    return jnp.concatenate([x1 * cos - x2 * sin, x2 * cos + x1 * sin], axis=-1).astype(x.dtype)


def axial_rope(x, t):
    half = x.shape[-1] // 2
    return jnp.concatenate([rope(x[..., :half], t // GRID_W), rope(x[..., half:], t % GRID_W)], axis=-1)


def split_heads(t, n_heads):
    b, s, _ = t.shape
    return t.reshape(b, s, n_heads, -1).transpose(0, 2, 1, 3)


def merge_heads(t):
    b, h, s, e = t.shape
    return t.transpose(0, 2, 1, 3).reshape(b, s, h * e)


def blocked_attention(q, k, v):
    b, hk, g, s, dq = q.shape
    nblk = s // Q_BLOCK
    scale = dq ** -0.5
    qb = q.reshape(b, hk, g, nblk, Q_BLOCK, dq).transpose(3, 0, 1, 2, 4, 5)

    def one_block(qi):
        sc = jnp.einsum('bkgqe,bkse->bkgqs', qi, k, preferred_element_type=jnp.float32) * scale
        p = jax.nn.softmax(sc, axis=-1).astype(v.dtype)
        return jnp.einsum('bkgqs,bksd->bkgqd', p, v)

    o = lax.map(one_block, qb)
    return o.transpose(1, 2, 3, 0, 4, 5).reshape(b, hk, g, s, v.shape[-1])


def dilated_window_partial(q, k, v, window, dilation):
    b, h, s, dh = q.shape
    d = dilation
    L = s // d
    half = window // (2 * d)
    nb = -(-L // A_QBLOCK)
    Lp = nb * A_QBLOCK
    kw = A_QBLOCK + 2 * half

    def by_residue(t):
        return t.reshape(b, h, L, d, dh).transpose(0, 1, 3, 2, 4)

    qr = jnp.pad(by_residue(q), ((0, 0), (0, 0), (0, 0), (0, Lp - L), (0, 0)))
    qr = qr.reshape(b, h, d, nb, A_QBLOCK, dh)
    pad_kv = ((0, 0), (0, 0), (0, 0), (half, Lp - L + half), (0, 0))
    kp = jnp.pad(by_residue(k), pad_kv)
    vp = jnp.pad(by_residue(v), pad_kv)
    kidx = jnp.arange(nb)[:, None] * A_QBLOCK + jnp.arange(kw)[None, :]
    kb = kp[:, :, :, kidx]
    vb = vp[:, :, :, kidx]
    sc = jnp.einsum('bhrnqe,bhrnke->bhrnqk', qr, kb, preferred_element_type=jnp.float32) * (dh ** -0.5)
    qpos = jnp.arange(nb)[:, None] * A_QBLOCK + jnp.arange(A_QBLOCK)[None, :]
    kpos = kidx - half
    rel = kpos[:, None, :] - qpos[:, :, None]
    valid = (jnp.abs(rel) <= half) & (kpos[:, None, :] >= 0) & (kpos[:, None, :] < L)
    sc = jnp.where(valid, sc, NEG_INF)
    m = jnp.max(sc, axis=-1)
    p = jnp.exp(sc - m[..., None])
    l = jnp.sum(p, axis=-1)
    acc = jnp.einsum('bhrnqk,bhrnke->bhrnqe', p.astype(v.dtype), vb, preferred_element_type=jnp.float32)

    def back(t):
        t = t.reshape((b, h, d, Lp) + t.shape[5:])[:, :, :, :L]
        t = jnp.moveaxis(t, 2, 3)
        return t.reshape((b, h, s) + t.shape[4:])

    return back(acc), back(m), back(l)


def dilated_mixture_attention(q, k, v):
    parts = [dilated_window_partial(q, k, v, w, d) for w, d in A_PATTERNS]
    m_max = jnp.max(jnp.stack([m for _, m, _ in parts]), axis=0)
    num = sum(acc * jnp.exp(m - m_max)[..., None] for acc, m, _ in parts)
    den = sum(l * jnp.exp(m - m_max) for _, m, l in parts)
    return (num / den[..., None]).astype(q.dtype)


def neighborhood_attention(q, k, v, rpb):
    b, h, s, dh = q.shape
    rows = s // GRID_W
    wr = min(C_WIN_R, rows)

    def grid(t):
        return t.reshape(b, h, rows, GRID_W, dh)

    r = jnp.arange(rows)
    row_start = jnp.clip(r - wr // 2, 0, rows - wr)
    row_idx = row_start[:, None] + jnp.arange(wr)[None, :]
    kr = grid(k)[:, :, row_idx]
    vr = grid(v)[:, :, row_idx]
    sc = jnp.einsum('bhrce,bhriwe->bhrciw', grid(q), kr, preferred_element_type=jnp.float32) * (dh ** -0.5)
    c = jnp.arange(GRID_W)
    col_start = jnp.clip(c - C_WIN_C // 2, 0, GRID_W - C_WIN_C)
    col_ok = (c[None, :] >= col_start[:, None]) & (c[None, :] < col_start[:, None] + C_WIN_C)
    drow = row_idx - r[:, None] + (C_WIN_R - 1)
    dcol = jnp.clip(c[None, :] - c[:, None] + (C_WIN_C - 1), 0, 2 * C_WIN_C - 2)
    bias = rpb[:, drow[:, None, :, None], dcol[None, :, None, :]]
    sc = jnp.where(col_ok[None, None, None, :, None, :], sc + bias.astype(jnp.float32)[None], NEG_INF)
    p = jax.nn.softmax(sc.reshape(b, h, rows, GRID_W, wr * GRID_W), axis=-1)
    p = p.reshape(b, h, rows, GRID_W, wr, GRID_W).astype(v.dtype)
    o = jnp.einsum('bhrciw,bhriwe->bhrce', p, vr)
    return o.reshape(b, h, s, dh)


def mixer_block(h, w_in, g_qk_abc, w_uq, w_ukv, g_cq, g_ckv, g_qk_mla, rpb, g_grp, w_out):
    b, s, _ = h.shape
    t = jnp.arange(s)
    proj = jnp.einsum('bsd,dc->bsc', h, w_in)
    offs = np.cumsum(IN_SIZES)[:-1].tolist()
    qa, ka, va, qb, kb, vb, qc, kc, vc, cq, ckv, kpe = jnp.split(proj, offs, axis=-1)

    qa = rope(rms_norm(split_heads(qa, GROUP_HEADS), g_qk_abc[0, 0]), t)
    ka = rope(rms_norm(split_heads(ka, GROUP_HEADS), g_qk_abc[0, 1]), t)
    oa = dilated_mixture_attention(qa, ka, split_heads(va, GROUP_HEADS))

    qb = axial_rope(rms_norm(split_heads(qb, GROUP_HEADS), g_qk_abc[1, 0]), t)
    kb = axial_rope(rms_norm(split_heads(kb, B_KV_HEADS), g_qk_abc[1, 1]), t)
    qb = qb.reshape(b, B_KV_HEADS, GROUP_HEADS // B_KV_HEADS, s, HEAD_DIM)
    ob = blocked_attention(qb, kb, split_heads(vb, B_KV_HEADS)).reshape(b, GROUP_HEADS, s, HEAD_DIM)

    qc = rms_norm(split_heads(qc, GROUP_HEADS), g_qk_abc[2, 0])
    kc = rms_norm(split_heads(kc, GROUP_HEADS), g_qk_abc[2, 1])
    oc = neighborhood_attention(qc, kc, split_heads(vc, GROUP_HEADS), rpb)

    cq = rms_norm(cq, g_cq)
    ckv = rms_norm(ckv, g_ckv)
    qd = split_heads(jnp.einsum('bsr,rc->bsc', cq, w_uq), GROUP_HEADS)
    kvd = split_heads(jnp.einsum('bsr,rc->bsc', ckv, w_ukv), GROUP_HEADS)
    k_pe = jnp.broadcast_to(kpe[:, None], (b, GROUP_HEADS, s, D_ROPE))
    kd = jnp.concatenate([kvd[..., :D_NOPE], k_pe], axis=-1)
    vd = kvd[..., D_NOPE:]
    qd = rms_norm(qd, g_qk_mla[0])
    kd = rms_norm(kd, g_qk_mla[1])
    qd = jnp.concatenate([qd[..., :D_NOPE], rope(qd[..., D_NOPE:], t)], axis=-1)
    kd = jnp.concatenate([kd[..., :D_NOPE], rope(kd[..., D_NOPE:], t)], axis=-1)
    od = blocked_attention(qd[:, :, None], kd, vd)[:, :, 0]

    groups = [merge_heads(o) for o in (oa, ob, oc, od)]
    normed = [rms_norm(o, g_grp[i * GROUP_WIDTH:(i + 1) * GROUP_WIDTH]) for i, o in enumerate(groups)]
    return jnp.einsum('bsc,cd->bsd', jnp.concatenate(normed, axis=-1), w_out)


def swiglu(h, wg, wu, wd):
    a = jax.nn.silu(jnp.einsum('bsd,df->bsf', h, wg)) * jnp.einsum('bsd,df->bsf', h, wu)
    return jnp.einsum('bsf,fd->bsd', a, wd)


def moe_swiglu(h, w_router, wg, wu, wd):
    logits = jnp.einsum('bsd,de->bse', h, w_router, preferred_element_type=jnp.float32)
    top_val, top_idx = lax.top_k(logits, TOP_K)
    gates = jax.nn.softmax(top_val, axis=-1)
    combine = jnp.sum(jax.nn.one_hot(top_idx, N_EXPERTS, dtype=jnp.float32) * gates[..., None], axis=-2)
    combine = combine.astype(h.dtype)
    out = 0
    for e in range(N_EXPERTS):
        out = out + combine[..., e:e + 1] * swiglu(h, wg[e], wu[e], wd[e])
    return out


def setup_inputs(seed: int = 0) -> dict:
    key = jax.random.key(seed)
    ks = jax.random.split(key, 20)
    f32 = jnp.float32
    n_dense = (DEPTH + 1) // 2
    n_moe = DEPTH // 2

    def normal(k, shape, fan_in):
        return jax.random.normal(k, shape, f32) * (fan_in ** -0.5)

    def gain(k, shape):
        return 1.0 + 0.05 * jax.random.normal(k, shape, f32)

    return {
        'x': jax.random.normal(ks[0], (BATCH, SEQ, D_MODEL), f32),
        'g_mix': gain(ks[1], (DEPTH, D_MODEL)),
        'w_in': normal(ks[2], (DEPTH, D_MODEL, IN_COLS), D_MODEL),
        'g_qk_abc': gain(ks[3], (DEPTH, 3, 2, HEAD_DIM)),
        'w_uq': normal(ks[4], (DEPTH, D_Q_RANK, GROUP_HEADS * D_QK), D_Q_RANK),
        'w_ukv': normal(ks[5], (DEPTH, D_KV_RANK, GROUP_HEADS * (D_NOPE + D_V)), D_KV_RANK),
        'g_cq': gain(ks[6], (DEPTH, D_Q_RANK)),
        'g_ckv': gain(ks[7], (DEPTH, D_KV_RANK)),
        'g_qk_mla': gain(ks[8], (DEPTH, 2, D_QK)),
        'rpb': 0.5 * jax.random.normal(ks[9], (DEPTH, GROUP_HEADS, 2 * C_WIN_R - 1, 2 * C_WIN_C - 1), f32),
        'g_grp': gain(ks[10], (DEPTH, MIX_WIDTH)),
        'w_out': normal(ks[11], (DEPTH, MIX_WIDTH, D_MODEL), MIX_WIDTH),
        'g_ffn': gain(ks[12], (DEPTH, D_MODEL)),
        'w_gate': normal(ks[13], (n_dense, D_MODEL, D_FF), D_MODEL),
        'w_up': normal(ks[14], (n_dense, D_MODEL, D_FF), D_MODEL),
        'w_down': normal(ks[15], (n_dense, D_FF, D_MODEL), D_FF),
        'w_router': normal(ks[16], (n_moe, D_MODEL, N_EXPERTS), D_MODEL),
        'we_gate': normal(ks[17], (n_moe, N_EXPERTS, D_MODEL, E_FF), D_MODEL),
        'we_up': normal(ks[18], (n_moe, N_EXPERTS, D_MODEL, E_FF), D_MODEL),
        'we_down': normal(ks[19], (n_moe, N_EXPERTS, E_FF, D_MODEL), E_FF),
    }


def reference(x, g_mix, w_in, g_qk_abc, w_uq, w_ukv, g_cq, g_ckv, g_qk_mla, rpb, g_grp, w_out,
              g_ffn, w_gate, w_up, w_down, w_router, we_gate, we_up, we_down):
    for l in range(DEPTH):
        h = rms_norm(x, g_mix[l])
        x = x + mixer_block(h, w_in[l], g_qk_abc[l], w_uq[l], w_ukv[l], g_cq[l], g_ckv[l],
                            g_qk_mla[l], rpb[l], g_grp[l], w_out[l])
        h = rms_norm(x, g_ffn[l])
        if l % 2 == 0:
            x = x + swiglu(h, w_gate[l // 2], w_up[l // 2], w_down[l // 2])
        else:
            x = x + moe_swiglu(h, w_router[l // 2], we_gate[l // 2], we_up[l // 2], we_down[l // 2])
    return x
```

```python
import functools

import jax
import jax.numpy as jnp
from jax import lax
from jax.experimental import pallas as pl
from jax.experimental.pallas import tpu as pltpu

F32 = jnp.float32
BF16 = jnp.bfloat16

LANE = 128
HEAD_DIM = 128
N_MIXERS = 4
ROPE_THETA = 10000.0
EPS = 1e-6
NEG_INF = -1e30
GRID_W = 64
A_PATTERNS = ((128, 1), (512, 4), (2048, 16))
C_WIN_R = 8
C_WIN_C = 16
C_QROWS = 4
C_KROWS = 12
D_KV_RANK = 512
D_NOPE = 128
D_ROPE = 64
D_QK = D_NOPE + D_ROPE
TOP_K = 2
VMEM_LIMIT_BYTES = 56 * 1024 * 1024


def _blk(dim, target, mult=LANE):
    best = None
    d = mult
    while d <= min(dim, target):
        if dim % d == 0:
            best = d
        d += mult
    return best if best is not None else dim


def _params(n_grid):
    return pltpu.CompilerParams(dimension_semantics=("arbitrary",) * n_grid,
                                vmem_limit_bytes=VMEM_LIMIT_BYTES)


def _rmsnorm_kernel(*refs, n_in):
    x_refs = refs[:n_in]
    g_ref = refs[n_in]
    o_ref = refs[n_in + 1]
    off = 0
    for x_ref in x_refs:
        x = x_ref[...]
        w = x.shape[-1]
        ms = jnp.mean(x * x, axis=-1, keepdims=True)
        y = (x * lax.rsqrt(ms + EPS)) * g_ref[:, off:off + w]
        o_ref[:, off:off + w] = y.astype(o_ref.dtype)
        off += w


def _rmsnorm(xs, gain, bm=256):
    n = xs[0][0].shape[0]
    bm = _blk(n, bm, 8)
    total = sum(w for _, _, w in xs)
    in_specs = [pl.BlockSpec((bm, w), functools.partial(lambda i, c: (i, c), c=c)) for _, c, w in xs]
    in_specs.append(pl.BlockSpec((1, total), lambda i: (0, 0)))
    return pl.pallas_call(
        functools.partial(_rmsnorm_kernel, n_in=len(xs)),
        grid=(n // bm,),
        in_specs=in_specs,
        out_specs=pl.BlockSpec((bm, total), lambda i: (i, 0)),
        out_shape=jax.ShapeDtypeStruct((n, total), BF16),
        compiler_params=_params(1),
    )(*[a for a, _, _ in xs], gain.reshape(1, total))


def _mm_kernel(a_ref, w_ref, *rest, nk, has_res):
    if has_res:
        r_ref, o_ref = rest
    else:
        (o_ref,) = rest
    part = jnp.dot(a_ref[...], w_ref[...], preferred_element_type=F32)
    if nk == 1:
        o_ref[...] = (r_ref[...] + part) if has_res else part
        return
    k = pl.program_id(2)

    @pl.when(k == 0)
    def _():
        o_ref[...] = (r_ref[...] + part) if has_res else part

    @pl.when(k > 0)
    def _():
        o_ref[...] += part


def _matmul(a, w, res=None, bm=1024, bn=512, bk=None):
    m, kdim = a.shape
    n = w.shape[1]
    bm = _blk(m, bm, 8)
    bn = _blk(n, bn)
    bk = kdim if bk is None else _blk(kdim, bk)
    nk = kdim // bk
    in_specs = [pl.BlockSpec((bm, bk), lambda i, j, k: (i, k)),
                pl.BlockSpec((bk, bn), lambda i, j, k: (k, j))]
    args = [a, w]
    if res is not None:
        in_specs.append(pl.BlockSpec((bm, bn), lambda i, j, k: (i, j)))
        args.append(res)
    return pl.pallas_call(
        functools.partial(_mm_kernel, nk=nk, has_res=res is not None),
        grid=(m // bm, n // bn, nk),
        in_specs=in_specs,
        out_specs=pl.BlockSpec((bm, bn), lambda i, j, k: (i, j)),
        out_shape=jax.ShapeDtypeStruct((m, n), F32),
        compiler_params=_params(3),
    )(*args)


def _gateup_kernel(h_ref, wg_ref, wu_ref, *rest, has_scale):
    if has_scale:
        c_ref, o_ref = rest
    else:
        (o_ref,) = rest
    h = h_ref[...]
    wg = wg_ref[0] if has_scale else wg_ref[...]
    wu = wu_ref[0] if has_scale else wu_ref[...]
    g = jnp.dot(h, wg, preferred_element_type=F32)
    u = jnp.dot(h, wu, preferred_element_type=F32)
    a = (g / (1.0 + jnp.exp(-g))) * u
    if has_scale:
        c = c_ref[0]
        a = a * jnp.concatenate([c] * (a.shape[1] // LANE), axis=1)
    o_ref[...] = a.astype(o_ref.dtype)


def _gateup_dense(h, wg, wu, bm=1024, bn=512):
    m, d = h.shape
    f = wg.shape[1]
    bm = _blk(m, bm, 8)
    bn = _blk(f, bn)
    return pl.pallas_call(
        functools.partial(_gateup_kernel, has_scale=False),
        grid=(m // bm, f // bn),
        in_specs=[pl.BlockSpec((bm, d), lambda i, j: (i, 0)),
                  pl.BlockSpec((d, bn), lambda i, j: (0, j)),
                  pl.BlockSpec((d, bn), lambda i, j: (0, j))],
        out_specs=pl.BlockSpec((bm, bn), lambda i, j: (i, j)),
        out_shape=jax.ShapeDtypeStruct((m, f), BF16),
        compiler_params=_params(2),
    )(h, wg, wu)


def _gateup_experts(h, wg, wu, comb, bm=1024, bn=512):
    m, d = h.shape
    e, _, ef = wg.shape
    bm = _blk(m, bm, 8)
    bn = _blk(ef, bn)
    per = ef // bn
    return pl.pallas_call(
        functools.partial(_gateup_kernel, has_scale=True),
        grid=(m // bm, e * per),
        in_specs=[pl.BlockSpec((bm, d), lambda i, j: (i, 0)),
                  pl.BlockSpec((1, d, bn), lambda i, j: (j // per, 0, j % per)),
                  pl.BlockSpec((1, d, bn), lambda i, j: (j // per, 0, j % per)),
                  pl.BlockSpec((1, bm, LANE), lambda i, j: (j // per, i, 0))],
        out_specs=pl.BlockSpec((bm, bn), lambda i, j: (i, j)),
        out_shape=jax.ShapeDtypeStruct((m, e * ef), BF16),
        compiler_params=_params(2),
    )(h, wg, wu, comb)


def _router_kernel(h_ref, w_ref, o_ref, *, n_exp):
    logits = jnp.dot(h_ref[...], w_ref[...], preferred_element_type=F32)
    lane = lax.broadcasted_iota(jnp.int32, logits.shape, 1).astype(F32)
    logits = jnp.where(lane < n_exp, logits, -jnp.inf)
    v1 = jnp.max(logits, axis=-1, keepdims=True)
    i1 = jnp.min(jnp.where(logits == v1, lane, float(LANE)), axis=-1, keepdims=True)
    rest = jnp.where(lane == i1, -jnp.inf, logits)
    v2 = jnp.max(rest, axis=-1, keepdims=True)
    i2 = jnp.min(jnp.where(rest == v2, lane, float(LANE)), axis=-1, keepdims=True)
    e2 = jnp.exp(v2 - v1)
    den = 1.0 + e2
    comb = jnp.where(lane == i1, 1.0 / den, 0.0) + jnp.where(lane == i2, e2 / den, 0.0)
    for e in range(n_exp):
        ce = jnp.sum(jnp.where(lane == e, comb, 0.0), axis=-1, keepdims=True)
        o_ref[e] = jnp.broadcast_to(ce, comb.shape)


def _router(h, w_router_padded, n_exp, bm=512):
    m, d = h.shape
    bm = _blk(m, bm, 8)
    return pl.pallas_call(
        functools.partial(_router_kernel, n_exp=n_exp),
        grid=(m // bm,),
        in_specs=[pl.BlockSpec((bm, d), lambda i: (i, 0)),
                  pl.BlockSpec((d, LANE), lambda i: (0, 0))],
        out_specs=pl.BlockSpec((n_exp, bm, LANE), lambda i: (0, i, 0)),
        out_shape=jax.ShapeDtypeStruct((n_exp, m, LANE), F32),
        compiler_params=_params(1),
    )(h, w_router_padded)


def _swap(y, kind):
    if kind == "half64":
        return pltpu.roll(y, 64, 1)
    lane = lax.broadcasted_iota(jnp.int32, y.shape, 1)
    lo = pltpu.roll(y, 96, 1)
    hi = pltpu.roll(y, 32, 1)
    return jnp.where((lane & 63) < 32, lo, hi)


def _norm_rope(chunks, gains, cos, sin, rope_flags, swap_kind, dim):
    ssq = None
    for c in chunks:
        t = jnp.sum(c * c, axis=-1, keepdims=True)
        ssq = t if ssq is None else ssq + t
    r = lax.rsqrt(ssq / dim + EPS)
    outs = []
    for c, g, flag in zip(chunks, gains, rope_flags):
        y = (c * r) * g
        if flag:
            y = y * cos + _swap(y, swap_kind) * sin
        outs.append(y.astype(BF16))
    return outs


def _attn_kernel(*refs, n_grp, n_chunk, rope_flags, swap_kind, dim, scale, bq, seq, kw, window, tmode):
    has_rope = any(rope_flags)
    it = iter(refs)
    q_ref = next(it)
    k_refs = [next(it) for _ in range(n_chunk)]
    v_ref = next(it)
    gq_ref = next(it)
    gk_ref = next(it)
    cos_ref = next(it) if has_rope else None
    sin_ref = next(it) if has_rope else None
    t_ref = next(it) if tmode else None
    o_ref = next(it)
    k_scr = next(it)
    v_scr = next(it)
    qb = pl.program_id(2)

    @pl.when(qb == 0)
    def _():
        step = min(seq, 512)
        gains = [gk_ref[:, c * LANE:(c + 1) * LANE] for c in range(n_chunk)]
        for r0 in range(0, seq, step):
            chunks = [kr[r0:r0 + step, :] for kr in k_refs]
            cos = cos_ref[r0:r0 + step, :] if has_rope else None
            sin = sin_ref[r0:r0 + step, :] if has_rope else None
            outs = _norm_rope(chunks, gains, cos, sin, rope_flags, swap_kind, dim)
            for c, o in enumerate(outs):
                k_scr[r0:r0 + step, c * LANE:(c + 1) * LANE] = o
            v_scr[r0:r0 + step, :] = v_ref[r0:r0 + step, :].astype(BF16)

    row0 = pl.multiple_of(qb * bq, bq)
    cos = cos_ref[pl.ds(row0, bq), :] if has_rope else None
    sin = sin_ref[pl.ds(row0, bq), :] if has_rope else None
    gains = [gq_ref[:, c * LANE:(c + 1) * LANE] for c in range(n_chunk)]
    qs = []
    for g in range(n_grp):
        base = g * n_chunk
        chunks = [q_ref[:, (base + c) * LANE:(base + c + 1) * LANE] for c in range(n_chunk)]
        outs = _norm_rope(chunks, gains, cos, sin, rope_flags, swap_kind, dim)
        qs.append(outs[0] if n_chunk == 1 else jnp.concatenate(outs, axis=1))
    q = qs[0] if n_grp == 1 else jnp.concatenate(qs, axis=0)

    if window == "full":
        kwin = k_scr[...]
        vwin = v_scr[...]
    else:
        rows = seq // GRID_W
        start_row = jnp.clip(qb * C_QROWS - C_WIN_R // 2, 0, rows - kw // GRID_W)
        start = pl.multiple_of(start_row * GRID_W, GRID_W)
        kwin = k_scr[pl.ds(start, kw), :]
        vwin = v_scr[pl.ds(start, kw), :]

    s = lax.dot_general(q, kwin, (((1,), (1,)), ((), ())), preferred_element_type=F32) * scale
    if tmode == "add":
        t = t_ref[0, 0]
        s = jnp.where(t > -1e29, s + t, NEG_INF)
    elif tmode == "mul":
        t = t_ref[...].astype(F32)
        s = jnp.where(t > 0.0, s, NEG_INF)
    m = jnp.max(s, axis=-1, keepdims=True)
    p = jnp.exp(s - m)
    if tmode == "mul":
        p = p * t
    l = jnp.sum(p, axis=-1, keepdims=True)
    o = jnp.dot(p.astype(BF16), vwin, preferred_element_type=F32) / l
    for g in range(n_grp):
        o_ref[:, g * LANE:(g + 1) * LANE] = o[g * bq:(g + 1) * bq, :]


def _attention(q_arr, q_col0, k_arrs, v_arr, v_col0, gq, gk, cos, sin, table, *, batch, seq, n_kv, n_grp,
               rope_flags, swap_kind, dim, bq, window, tmode):
    n_chunk = len(k_arrs)
    nqb = seq // bq
    kw = seq if window == "full" else min(C_KROWS, seq // GRID_W) * GRID_W
    qw = n_grp * n_chunk * LANE
    q_blk0 = q_col0 * LANE // qw
    assert q_blk0 * qw == q_col0 * LANE
    in_specs = [pl.BlockSpec((bq, qw), lambda b, g, i: (b * nqb + i, q_blk0 + g))]
    args = [q_arr]
    for arr, col0, per_head in k_arrs:
        in_specs.append(pl.BlockSpec((seq, LANE), functools.partial(
            lambda b, g, i, col0, per_head: (b, col0 + g * per_head), col0=col0, per_head=per_head)))
        args.append(arr)
    in_specs.append(pl.BlockSpec((seq, LANE), lambda b, g, i: (b, v_col0 + g)))
    args.append(v_arr)
    in_specs.append(pl.BlockSpec((1, n_chunk * LANE), lambda b, g, i: (0, 0)))
    args.append(gq.reshape(1, n_chunk * LANE))
    in_specs.append(pl.BlockSpec((1, n_chunk * LANE), lambda b, g, i: (0, 0)))
    args.append(gk.reshape(1, n_chunk * LANE))
    if any(rope_flags):
        in_specs += [pl.BlockSpec((seq, LANE), lambda b, g, i: (0, 0))] * 2
        args += [cos, sin]
    if tmode == "add":
        in_specs.append(pl.BlockSpec((1, 1, bq, kw), lambda b, g, i: (g, i, 0, 0)))
        args.append(table)
    elif tmode == "mul":
        in_specs.append(pl.BlockSpec((bq, kw), lambda b, g, i: (i, 0)))
        args.append(table)
    kern = functools.partial(
        _attn_kernel, n_grp=n_grp, n_chunk=n_chunk, rope_flags=rope_flags, swap_kind=swap_kind, dim=dim,
        scale=dim ** -0.5, bq=bq, seq=seq, kw=kw, window=window, tmode=tmode)
    return pl.pallas_call(
        kern,
        grid=(batch, n_kv, nqb),
        in_specs=in_specs,
        out_specs=pl.BlockSpec((bq, n_grp * LANE), lambda b, g, i: (b * nqb + i, g)),
        out_shape=jax.ShapeDtypeStruct((batch * seq, n_kv * n_grp * LANE), F32),
        scratch_shapes=[pltpu.VMEM((seq, n_chunk * LANE), BF16), pltpu.VMEM((seq, LANE), BF16)],
        compiler_params=_params(3),
    )(*args)


def _rope_cs(pos, half):
    inv_freq = ROPE_THETA ** (-jnp.arange(half, dtype=F32) / half)
    ang = pos.astype(F32)[:, None] * inv_freq[None, :]
    return jnp.cos(ang), jnp.sin(ang)


def _rope_tables(seq):
    t = jnp.arange(seq)
    c, s = _rope_cs(t, HEAD_DIM // 2)
    a_cos, a_sin = jnp.concatenate([c, c], -1), jnp.concatenate([-s, s], -1)
    cr, sr = _rope_cs(t // GRID_W, HEAD_DIM // 4)
    cc, sc = _rope_cs(t % GRID_W, HEAD_DIM // 4)
    b_cos = jnp.concatenate([cr, cr, cc, cc], -1)
    b_sin = jnp.concatenate([-sr, sr, -sc, sc], -1)
    cd, sd = _rope_cs(t, D_ROPE // 2)
    z = jnp.zeros((seq, LANE - D_ROPE), F32)
    d_cos = jnp.concatenate([cd, cd, z], -1)
    d_sin = jnp.concatenate([-sd, sd, z], -1)
    return (a_cos, a_sin), (b_cos, b_sin), (d_cos, d_sin)


def _dilation_multiplicity(seq):
    t = jnp.arange(seq)
    delta = t[None, :] - t[:, None]
    mult = jnp.zeros((seq, seq), jnp.int32)
    for window, d in A_PATTERNS:
        half = window // (2 * d)
        mult = mult + ((delta % d == 0) & (jnp.abs(delta) <= half * d)).astype(jnp.int32)
    return mult.astype(BF16)


def _neighbourhood_table(rpb_l, seq):
    heads = rpb_l.shape[0]
    rows = seq // GRID_W
    wr = min(C_WIN_R, rows)
    kr = min(C_KROWS, rows)
    nqb = rows // C_QROWS
    qb = jnp.arange(nqb)
    r = qb[:, None] * C_QROWS + jnp.arange(C_QROWS)[None, :]
    win0 = jnp.clip(qb * C_QROWS - C_WIN_R // 2, 0, rows - kr)
    krow = win0[:, None] + jnp.arange(kr)[None, :]
    row_start = jnp.clip(r - wr // 2, 0, rows - wr)
    row_ok = (krow[:, None, :] >= row_start[:, :, None]) & (krow[:, None, :] < row_start[:, :, None] + wr)
    drow = jnp.clip(krow[:, None, :] - r[:, :, None] + (C_WIN_R - 1), 0, 2 * C_WIN_R - 2)
    c = jnp.arange(GRID_W)
    col_start = jnp.clip(c - C_WIN_C // 2, 0, GRID_W - C_WIN_C)
    col_ok = (c[None, :] >= col_start[:, None]) & (c[None, :] < col_start[:, None] + C_WIN_C)
    dcol = jnp.clip(c[None, :] - c[:, None] + (C_WIN_C - 1), 0, 2 * C_WIN_C - 2)
    by_col = jnp.take(rpb_l, dcol.reshape(-1), axis=2).reshape(heads, 2 * C_WIN_R - 1, GRID_W, GRID_W)
    by_row = jnp.take(by_col, drow.reshape(-1), axis=1)
    bias = by_row.reshape(heads, nqb, C_QROWS, kr, GRID_W, GRID_W).transpose(0, 1, 2, 4, 3, 5)
    ok = row_ok[None, :, :, None, :, None] & col_ok[None, None, None, :, None, :]
    table = jnp.where(ok, bias, NEG_INF)
    return table.reshape(heads, nqb, C_QROWS * GRID_W, kr * GRID_W)


def _mixer(x2, h, lw, tabs, batch, seq):
    d_model = x2.shape[1]
    gw = d_model // N_MIXERS
    heads = gw // HEAD_DIM
    kv_heads = heads // 4
    nb = gw // LANE
    (a_cos, a_sin), (b_cos, b_sin), (d_cos, d_sin) = tabs["rope"]

    proj = _matmul(h, lw["w_main"])
    kpe = _matmul(h, lw["w_kpe"], bn=LANE)
    col = {"cq": 0, "qa": nb, "ka": 2 * nb, "va": 3 * nb, "qc": 4 * nb, "kc": 5 * nb, "vc": 6 * nb,
           "qb": 7 * nb}
    col["ckv"] = 8 * nb
    col["kb"] = col["ckv"] + D_KV_RANK // LANE
    col["vb"] = col["kb"] + kv_heads

    g_abc = lw["g_qk_abc"]
    common = dict(batch=batch, seq=seq)

    oa = _attention(proj, col["qa"], [(proj, col["ka"], 1)], proj, col["va"], g_abc[0, 0], g_abc[0, 1],
                    a_cos, a_sin, tabs["mult"], n_kv=heads, n_grp=1, rope_flags=(True,), swap_kind="half64",
                    dim=HEAD_DIM, bq=min(512, seq), window="full", tmode="mul", **common)
    ob = _attention(proj, col["qb"], [(proj, col["kb"], 1)], proj, col["vb"], g_abc[1, 0], g_abc[1, 1],
                    b_cos, b_sin, None, n_kv=kv_heads, n_grp=4, rope_flags=(True,), swap_kind="half32",
                    dim=HEAD_DIM, bq=min(128, seq), window="full", tmode=None, **common)
    oc = _attention(proj, col["qc"], [(proj, col["kc"], 1)], proj, col["vc"], g_abc[2, 0], g_abc[2, 1],
                    None, None, lw["nbr_table"], n_kv=heads, n_grp=1, rope_flags=(False,), swap_kind=None,
                    dim=HEAD_DIM, bq=C_QROWS * GRID_W, window="rows", tmode="add", **common)

    cq_n = _rmsnorm([(proj, 0, gw)], lw["g_cq"])
    ckv_n = _rmsnorm([(proj, col["ckv"] * LANE // D_KV_RANK, D_KV_RANK)], lw["g_ckv"])
    qd = _matmul(cq_n, lw["w_uq"])
    kvd = _matmul(ckv_n, lw["w_ukv"])
    od = _attention(qd, 0, [(kvd, 0, 1), (kpe, 0, 0)], kvd, heads, lw["gq_mla"], lw["gk_mla"],
                    d_cos, d_sin, None, n_kv=heads, n_grp=1, rope_flags=(False, True), swap_kind="half32",
                    dim=D_QK, bq=min(512, seq), window="full", tmode=None, **common)

    normed = _rmsnorm([(oa, 0, gw), (ob, 0, gw), (oc, 0, gw), (od, 0, gw)], lw["g_grp"])
    return _matmul(normed, lw["w_out"], res=x2)


def _layer_weights(l, seq, g_qk_abc, w_in, w_uq, w_ukv, g_cq, g_ckv, g_qk_mla, rpb, g_grp, w_out):
    d_model = w_in.shape[1]
    gw = d_model // N_MIXERS
    heads = gw // HEAD_DIM
    kvw = (heads // 4) * HEAD_DIM
    q_rank = w_uq.shape[1]
    sizes = (gw, gw, gw, gw, kvw, kvw, gw, gw, gw, q_rank, D_KV_RANK, D_ROPE)
    offs = [0]
    for s in sizes:
        offs.append(offs[-1] + s)
    w = w_in[l]
    qa, ka, va, qb, kb, vb, qc, kc, vc, cq, ckv, kpe = [w[:, offs[i]:offs[i + 1]] for i in range(len(sizes))]
    w_main = jnp.concatenate([cq, qa, ka, va, qc, kc, vc, qb, ckv, kb, vb], axis=1).astype(BF16)
    w_kpe = jnp.pad(kpe, ((0, 0), (0, LANE - D_ROPE))).astype(BF16)
    uq = w_uq[l].reshape(q_rank, heads, D_QK)
    uq = jnp.pad(uq, ((0, 0), (0, 0), (0, 2 * LANE - D_QK))).reshape(q_rank, heads * 2 * LANE).astype(BF16)
    ukv = w_ukv[l].reshape(D_KV_RANK, heads, D_NOPE + HEAD_DIM)
    ukv = jnp.concatenate([ukv[:, :, :D_NOPE].reshape(D_KV_RANK, heads * D_NOPE),
                           ukv[:, :, D_NOPE:].reshape(D_KV_RANK, heads * HEAD_DIM)], axis=1).astype(BF16)
    pad_g = lambda g: jnp.pad(g, (0, 2 * LANE - D_QK))
    return dict(w_main=w_main, w_kpe=w_kpe, w_uq=uq, w_ukv=ukv, g_qk_abc=g_qk_abc[l], g_cq=g_cq[l],
                g_ckv=g_ckv[l], gq_mla=pad_g(g_qk_mla[l, 0]), gk_mla=pad_g(g_qk_mla[l, 1]),
                nbr_table=_neighbourhood_table(rpb[l], seq), g_grp=g_grp[l], w_out=w_out[l].astype(BF16))


def _pad_axis(w, axis, mult):
    extra = (-w.shape[axis]) % mult
    if not extra:
        return w
    pads = [(0, 0)] * w.ndim
    pads[axis] = (0, extra)
    return jnp.pad(w, pads)


def kernel(x, g_mix, w_in, g_qk_abc, w_uq, w_ukv, g_cq, g_ckv, g_qk_mla, rpb, g_grp, w_out, g_ffn, w_gate,
           w_up, w_down, w_router, we_gate, we_up, we_down):
    batch, seq, d_model = x.shape
    depth = g_mix.shape[0]
    n_exp = we_gate.shape[1]
    tabs = dict(rope=_rope_tables(seq), mult=_dilation_multiplicity(seq))
    x2 = x.reshape(batch * seq, d_model)
    for l in range(depth):
        lw = _layer_weights(l, seq, g_qk_abc, w_in, w_uq, w_ukv, g_cq, g_ckv, g_qk_mla, rpb, g_grp, w_out)
        h = _rmsnorm([(x2, 0, d_model)], g_mix[l])
        x2 = _mixer(x2, h, lw, tabs, batch, seq)
        h = _rmsnorm([(x2, 0, d_model)], g_ffn[l])
        if l % 2 == 0:
            i = l // 2
            wg = _pad_axis(w_gate[i].astype(BF16), 1, 512)
            wu = _pad_axis(w_up[i].astype(BF16), 1, 512)
            wd = _pad_axis(w_down[i].astype(BF16), 0, 512)
            a = _gateup_dense(h, wg, wu)
            x2 = _matmul(a, wd, res=x2, bn=1024, bk=2816)
        else:
            i = l // 2
            wr = jnp.pad(w_router[i], ((0, 0), (0, LANE - n_exp))).astype(BF16)
            comb = _router(h, wr, n_exp)
            a = _gateup_experts(h, we_gate[i].astype(BF16), we_up[i].astype(BF16), comb)
            wd = we_down[i].astype(BF16).reshape(n_exp * we_down.shape[2], d_model)
            x2 = _matmul(a, wd, res=x2, bn=1024, bk=2048)
    return x2.reshape(batch, seq, d_model)
```

```python
import functools

import jax
import jax.numpy as jnp
from jax import lax
from jax.experimental import pallas as pl
from jax.experimental.pallas import tpu as pltpu

F32 = jnp.float32
BF16 = jnp.bfloat16

LANE = 128
HEAD_DIM = 128
N_MIXERS = 4
ROPE_THETA = 10000.0
EPS = 1e-6
NEG_INF = -1e30
LOG2_E = 1.4426950408889634
GRID_W = 64
A_PATTERNS = ((128, 1), (512, 4), (2048, 16))
C_WIN_R = 8
C_WIN_C = 16
C_QROWS = 4
C_KROWS = 12
ATTN_ROWS = 128
D_KV_RANK = 512
D_NOPE = 128
D_ROPE = 64
D_QK = D_NOPE + D_ROPE
TOP_K = 2
VMEM_LIMIT_BYTES = 56 * 1024 * 1024


def _blk(dim, target, mult=LANE):
    best = None
    d = mult
    while d <= min(dim, target):
        if dim % d == 0:
            best = d
        d += mult
    return best if best is not None else dim


def _params(n_grid):
    return pltpu.CompilerParams(dimension_semantics=("arbitrary",) * n_grid,
                                vmem_limit_bytes=VMEM_LIMIT_BYTES)


def _rmsnorm_kernel(*refs, n_in):
    x_refs = refs[:n_in]
    g_ref = refs[n_in]
    o_ref = refs[n_in + 1]
    off = 0
    for x_ref in x_refs:
        x = x_ref[...]
        w = x.shape[-1]
        ms = jnp.mean(x * x, axis=-1, keepdims=True)
        y = (x * lax.rsqrt(ms + EPS)) * g_ref[:, off:off + w]
        o_ref[:, off:off + w] = y.astype(o_ref.dtype)
        off += w


def _rmsnorm(xs, gain, bm=256, out_dtype=BF16):
    n = xs[0][0].shape[0]
    bm = _blk(n, bm, 8)
    total = sum(w for _, _, w in xs)
    in_specs = [pl.BlockSpec((bm, w), functools.partial(lambda i, c: (i, c), c=c)) for _, c, w in xs]
    in_specs.append(pl.BlockSpec((1, total), lambda i: (0, 0)))
    return pl.pallas_call(
        functools.partial(_rmsnorm_kernel, n_in=len(xs)),
        grid=(n // bm,),
        in_specs=in_specs,
        out_specs=pl.BlockSpec((bm, total), lambda i: (i, 0)),
        out_shape=jax.ShapeDtypeStruct((n, total), out_dtype),
        compiler_params=_params(1),
        name="rmsnorm",
    )(*[a for a, _, _ in xs], gain.reshape(1, total))


def _mm_kernel(a_ref, w_ref, *rest, nk, has_res):
    if has_res:
        r_ref, o_ref = rest
    else:
        (o_ref,) = rest
    part = jnp.dot(a_ref[...], w_ref[...], preferred_element_type=F32)
    if nk == 1:
        o_ref[...] = (r_ref[...] + part) if has_res else part
        return
    k = pl.program_id(2)

    @pl.when(k == 0)
    def _():
        o_ref[...] = (r_ref[...] + part) if has_res else part

    @pl.when(k > 0)
    def _():
        o_ref[...] += part


def _matmul(a, w, res=None, bm=1024, bn=512, bk=None):
    m, kdim = a.shape
    n = w.shape[1]
    bm = _blk(m, bm, 8)
    bn = _blk(n, bn)
    bk = kdim if bk is None else _blk(kdim, bk)
    nk = kdim // bk
    in_specs = [pl.BlockSpec((bm, bk), lambda i, j, k: (i, k)),
                pl.BlockSpec((bk, bn), lambda i, j, k: (k, j))]
    args = [a, w]
    if res is not None:
        in_specs.append(pl.BlockSpec((bm, bn), lambda i, j, k: (i, j)))
        args.append(res)
    return pl.pallas_call(
        functools.partial(_mm_kernel, nk=nk, has_res=res is not None),
        grid=(m // bm, n // bn, nk),
        in_specs=in_specs,
        out_specs=pl.BlockSpec((bm, bn), lambda i, j, k: (i, j)),
        out_shape=jax.ShapeDtypeStruct((m, n), F32),
        compiler_params=_params(3),
    )(*args)


def _gateup_kernel(h_ref, wg_ref, wu_ref, o_ref):
    h = h_ref[...]
    g = jnp.dot(h, wg_ref[...], preferred_element_type=F32)
    u = jnp.dot(h, wu_ref[...], preferred_element_type=F32)
    o_ref[...] = ((g / (1.0 + jnp.exp(-g))) * u).astype(o_ref.dtype)


def _gateup_dense(h, wg, wu, bm=1024, bn=512):
    m, d = h.shape
    f = wg.shape[1]
    bm = _blk(m, bm, 8)
    bn = _blk(f, bn)
    return pl.pallas_call(
        _gateup_kernel,
        grid=(m // bm, f // bn),
        in_specs=[pl.BlockSpec((bm, d), lambda i, j: (i, 0)),
                  pl.BlockSpec((d, bn), lambda i, j: (0, j)),
                  pl.BlockSpec((d, bn), lambda i, j: (0, j))],
        out_specs=pl.BlockSpec((bm, bn), lambda i, j: (i, j)),
        out_shape=jax.ShapeDtypeStruct((m, f), BF16),
        compiler_params=_params(2),
        name="ffn_gateup",
    )(h, wg, wu)


ROUTE_IDX1, ROUTE_IDX2, ROUTE_G1, ROUTE_G2, ROUTE_RANK1, ROUTE_RANK2 = range(6)


def _router_kernel(h_ref, w_ref, route_ref, cnt_ref, carry, *, n_exp):
    @pl.when(pl.program_id(0) == 0)
    def _():
        carry[...] = jnp.zeros_like(carry)

    logits = jnp.dot(h_ref[...].astype(BF16), w_ref[...], preferred_element_type=F32)
    bm = logits.shape[0]
    lane = lax.broadcasted_iota(jnp.int32, logits.shape, 1).astype(F32)
    logits = jnp.where(lane < n_exp, logits, -jnp.inf)
    v1 = jnp.max(logits, axis=-1, keepdims=True)
    i1 = jnp.min(jnp.where(logits == v1, lane, float(LANE)), axis=-1, keepdims=True)
    rest = jnp.where(lane == i1, -jnp.inf, logits)
    v2 = jnp.max(rest, axis=-1, keepdims=True)
    i2 = jnp.min(jnp.where(rest == v2, lane, float(LANE)), axis=-1, keepdims=True)
    e2 = jnp.exp(v2 - v1)
    den = 1.0 + e2
    g1 = 1.0 / den
    g2 = e2 / den
    chosen = jnp.where((lane == i1) | (lane == i2), 1.0, 0.0)
    row = lax.broadcasted_iota(jnp.int32, (bm, bm), 0)
    col = lax.broadcasted_iota(jnp.int32, (bm, bm), 1)
    earlier = jnp.where(col < row, 1.0, 0.0).astype(BF16)
    before = jnp.dot(earlier, chosen.astype(BF16), preferred_element_type=F32) + carry[...]
    rank1 = jnp.sum(jnp.where(lane == i1, before, 0.0), axis=-1, keepdims=True)
    rank2 = jnp.sum(jnp.where(lane == i2, before, 0.0), axis=-1, keepdims=True)
    carry[...] += jnp.sum(chosen, axis=0, keepdims=True)
    route = jnp.zeros_like(logits)
    for pos, val in ((ROUTE_IDX1, i1), (ROUTE_IDX2, i2), (ROUTE_G1, g1), (ROUTE_G2, g2),
                     (ROUTE_RANK1, rank1), (ROUTE_RANK2, rank2)):
        route = jnp.where(lane == pos, val, route)
    route_ref[...] = route
    cnt_ref[...] = jnp.broadcast_to(carry[...], cnt_ref.shape)


def _router(h32, w_router_padded, n_exp, bm=512):
    m, d = h32.shape
    bm = _blk(m, bm, 8)
    return pl.pallas_call(
        functools.partial(_router_kernel, n_exp=n_exp),
        grid=(m // bm,),
        in_specs=[pl.BlockSpec((bm, d), lambda i: (i, 0)),
                  pl.BlockSpec((d, LANE), lambda i: (0, 0))],
        out_specs=[pl.BlockSpec((bm, LANE), lambda i: (i, 0)),
                   pl.BlockSpec((8, LANE), lambda i: (0, 0))],
        out_shape=[jax.ShapeDtypeStruct((m, LANE), F32), jax.ShapeDtypeStruct((8, LANE), F32)],
        scratch_shapes=[pltpu.VMEM((1, LANE), F32)],
        compiler_params=_params(1),
        name="router",
    )(h32, w_router_padded)


def _invert_kernel(slot1_ref, slot2_ref, tok_ref, *, n_tok, n_slot):
    def clear(i, c):
        tok_ref[i] = 0
        return c

    lax.fori_loop(0, n_slot, clear, 0)

    def place(t, c):
        tok_ref[slot1_ref[t]] = t
        tok_ref[slot2_ref[t]] = t
        return c

    lax.fori_loop(0, n_tok, place, 0)


def _invert(slot1, slot2, n_slot):
    n_tok = slot1.shape[0]
    smem = pl.BlockSpec(memory_space=pltpu.SMEM)
    return pl.pallas_call(
        functools.partial(_invert_kernel, n_tok=n_tok, n_slot=n_slot),
        in_specs=[smem, smem],
        out_specs=smem,
        out_shape=jax.ShapeDtypeStruct((n_slot,), jnp.int32),
        name="invert_slots",
    )(slot1, slot2)


def _gather_kernel(tok_ref, src_ref, o_ref, buf, sem, *, tb, n_tiles):
    i = pl.program_id(0)

    def row_copy(tile, r, slot, t):
        return pltpu.make_async_copy(src_ref.at[pl.ds(t, 1)], buf.at[slot, pl.ds(r, 1)], sem.at[slot])

    def issue(tile, slot):
        def body(r, c):
            row_copy(tile, r, slot, tok_ref[tile * tb + r]).start()
            return c

        lax.fori_loop(0, tb, body, 0)

    @pl.when(i == 0)
    def _():
        issue(0, 0)

    @pl.when(i + 1 < n_tiles)
    def _():
        issue(i + 1, (i + 1) % 2)

    slot = i % 2

    def wait(r, c):
        row_copy(i, r, slot, 0).wait()
        return c

    lax.fori_loop(0, tb, wait, 0)
    o_ref[...] = buf[slot].astype(o_ref.dtype)


def _gather_rows(tok_of_slot, src, tb=256):
    n_slot = tok_of_slot.shape[0]
    d = src.shape[1]
    n_tiles = n_slot // tb
    return pl.pallas_call(
        functools.partial(_gather_kernel, tb=tb, n_tiles=n_tiles),
        grid_spec=pltpu.PrefetchScalarGridSpec(
            num_scalar_prefetch=1,
            grid=(n_tiles,),
            in_specs=[pl.BlockSpec(memory_space=pl.ANY)],
            out_specs=pl.BlockSpec((tb, d), lambda i, tok: (i, 0)),
            scratch_shapes=[pltpu.VMEM((2, tb, d), F32), pltpu.SemaphoreType.DMA((2,))]),
        out_shape=jax.ShapeDtypeStruct((n_slot, d), BF16),
        compiler_params=_params(1),
        name="gather_rows",
    )(tok_of_slot, src)


def _grouped_gateup_kernel(te_ref, nu_ref, x_ref, wg_ref, wu_ref, o_ref):
    @pl.when(pl.program_id(1) < nu_ref[0])
    def _():
        x = x_ref[...]
        g = jnp.dot(x, wg_ref[0], preferred_element_type=F32)
        u = jnp.dot(x, wu_ref[0], preferred_element_type=F32)
        o_ref[...] = ((g / (1.0 + jnp.exp(-g))) * u).astype(o_ref.dtype)

    @pl.when(pl.program_id(1) >= nu_ref[0])
    def _():
        o_ref[...] = jnp.zeros_like(o_ref)


def _grouped_gateup(tile_expert, n_used, xs, wg, wu, bm, bn=1024):
    p, d = xs.shape
    ef = wg.shape[2]
    bn = _blk(ef, bn)
    w_spec = pl.BlockSpec((1, d, bn), lambda j, i, te, nu: (te[i], 0, j))
    return pl.pallas_call(
        _grouped_gateup_kernel,
        grid_spec=pltpu.PrefetchScalarGridSpec(
            num_scalar_prefetch=2,
            grid=(ef // bn, p // bm),
            in_specs=[pl.BlockSpec((bm, d), lambda j, i, te, nu: (i, 0)), w_spec, w_spec],
            out_specs=pl.BlockSpec((bm, bn), lambda j, i, te, nu: (i, j))),
        out_shape=jax.ShapeDtypeStruct((p, ef), BF16),
        compiler_params=_params(2),
        name="expert_gateup",
    )(tile_expert, n_used, xs, wg, wu)


def _grouped_down_kernel(te_ref, nu_ref, a_ref, w_ref, o_ref):
    @pl.when(pl.program_id(1) < nu_ref[0])
    def _():
        o_ref[...] = jnp.dot(a_ref[...], w_ref[0], preferred_element_type=F32)

    @pl.when(pl.program_id(1) >= nu_ref[0])
    def _():
        o_ref[...] = jnp.zeros_like(o_ref)


def _grouped_down(tile_expert, n_used, a, wd, bm, bn=1024):
    p, ef = a.shape
    d = wd.shape[2]
    bn = _blk(d, bn)
    return pl.pallas_call(
        _grouped_down_kernel,
        grid_spec=pltpu.PrefetchScalarGridSpec(
            num_scalar_prefetch=2,
            grid=(d // bn, p // bm),
            in_specs=[pl.BlockSpec((bm, ef), lambda j, i, te, nu: (i, 0)),
                      pl.BlockSpec((1, ef, bn), lambda j, i, te, nu: (te[i], 0, j))],
            out_specs=pl.BlockSpec((bm, bn), lambda j, i, te, nu: (i, j))),
        out_shape=jax.ShapeDtypeStruct((p, d), F32),
        compiler_params=_params(2),
        name="expert_down",
    )(tile_expert, n_used, a, wd)


def _combine_kernel(s1_ref, s2_ref, y_ref, x_ref, route_ref, o_ref, buf, sem, *, tb, n_tiles):
    i = pl.program_id(0)

    def row_copy(k, r, slot, s):
        return pltpu.make_async_copy(y_ref.at[pl.ds(s, 1)], buf.at[slot, k, pl.ds(r, 1)], sem.at[slot])

    def issue(tile, slot):
        def body(r, c):
            t = tile * tb + r
            row_copy(0, r, slot, s1_ref[t]).start()
            row_copy(1, r, slot, s2_ref[t]).start()
            return c

        lax.fori_loop(0, tb, body, 0)

    @pl.when(i == 0)
    def _():
        issue(0, 0)

    @pl.when(i + 1 < n_tiles)
    def _():
        issue(i + 1, (i + 1) % 2)

    slot = i % 2

    def wait(r, c):
        row_copy(0, r, slot, 0).wait()
        row_copy(1, r, slot, 0).wait()
        return c

    lax.fori_loop(0, tb, wait, 0)
    route = route_ref[...]
    g1 = route[:, ROUTE_G1:ROUTE_G1 + 1]
    g2 = route[:, ROUTE_G2:ROUTE_G2 + 1]
    o_ref[...] = x_ref[...] + (g1 * buf[slot, 0] + g2 * buf[slot, 1])


def _combine(slot1, slot2, y, x2, route, tb=128):
    n, d = x2.shape
    tb = _blk(n, tb, 8)
    n_tiles = n // tb
    return pl.pallas_call(
        functools.partial(_combine_kernel, tb=tb, n_tiles=n_tiles),
        grid_spec=pltpu.PrefetchScalarGridSpec(
            num_scalar_prefetch=2,
            grid=(n_tiles,),
            in_specs=[pl.BlockSpec(memory_space=pl.ANY),
                      pl.BlockSpec((tb, d), lambda i, s1, s2: (i, 0)),
                      pl.BlockSpec((tb, LANE), lambda i, s1, s2: (i, 0))],
            out_specs=pl.BlockSpec((tb, d), lambda i, s1, s2: (i, 0)),
            scratch_shapes=[pltpu.VMEM((2, 2, tb, d), F32), pltpu.SemaphoreType.DMA((2,))]),
        out_shape=jax.ShapeDtypeStruct((n, d), F32),
        compiler_params=_params(1),
        name="expert_combine",
    )(slot1, slot2, y, x2, route)


def _moe(x2, h32, w_router_padded, wg, wu, wd, bm=512):
    n = x2.shape[0]
    n_exp = wg.shape[0]
    bm = _blk(n, bm, 8)
    route, cnt = _router(h32, w_router_padded, n_exp)
    counts = cnt[0, :n_exp].astype(jnp.int32)
    padded = ((counts + bm - 1) // bm) * bm
    ends = jnp.cumsum(padded)
    offs = ends - padded
    idx1 = route[:, ROUTE_IDX1].astype(jnp.int32)
    idx2 = route[:, ROUTE_IDX2].astype(jnp.int32)
    slot1 = offs[idx1] + route[:, ROUTE_RANK1].astype(jnp.int32)
    slot2 = offs[idx2] + route[:, ROUTE_RANK2].astype(jnp.int32)
    n_slot = n * TOP_K + n_exp * bm
    n_tiles = n_slot // bm
    tile_start = jnp.arange(n_tiles, dtype=jnp.int32) * bm
    tile_expert = jnp.minimum(jnp.sum((ends[None, :] <= tile_start[:, None]).astype(jnp.int32), axis=1), n_exp - 1)
    n_used = (ends[-1] // bm).reshape(1)
    tok_of_slot = _invert(slot1, slot2, n_slot)
    xs = _gather_rows(tok_of_slot, h32, tb=min(256, bm))
    a = _grouped_gateup(tile_expert, n_used, xs, wg, wu, bm)
    y = _grouped_down(tile_expert, n_used, a, wd, bm)
    return _combine(slot1, slot2, y, x2, route)


def _swap(y, kind):
    if kind == "half64":
        return pltpu.roll(y, 64, 1)
    lane = lax.broadcasted_iota(jnp.int32, y.shape, 1)
    lo = pltpu.roll(y, 96, 1)
    hi = pltpu.roll(y, 32, 1)
    return jnp.where((lane & 63) < 32, lo, hi)


def _norm_rope(chunks, gains, cos, sin, rope_flags, swap_kind, dim):
    ssq = None
    for c in chunks:
        t = jnp.sum(c * c, axis=-1, keepdims=True)
        ssq = t if ssq is None else ssq + t
    r = lax.rsqrt(ssq / dim + EPS)
    outs = []
    for c, g, flag in zip(chunks, gains, rope_flags):
        y = (c * r) * g
        if flag:
            y = y * cos + _swap(y, swap_kind) * sin
        outs.append(y.astype(BF16))
    return outs


def _attn_kernel(*refs, n_grp, n_chunk, rope_flags, swap_kind, dim, scale, bq, rc, seq, kw, window, tmode):
    has_rope = any(rope_flags)
    it = iter(refs)
    q_ref = next(it)
    k_refs = [next(it) for _ in range(n_chunk)]
    v_ref = next(it)
    gq_ref = next(it)
    gk_ref = next(it)
    cos_ref = next(it) if has_rope else None
    sin_ref = next(it) if has_rope else None
    t_ref = next(it) if tmode else None
    o_ref = next(it)
    k_scr = next(it)
    v_scr = next(it)
    qb = pl.program_id(2)

    @pl.when(qb == 0)
    def _():
        step = min(seq, 512)
        gains = [gk_ref[:, c * LANE:(c + 1) * LANE] for c in range(n_chunk)]
        for r0 in range(0, seq, step):
            chunks = [kr[r0:r0 + step, :] for kr in k_refs]
            cos = cos_ref[r0:r0 + step, :] if has_rope else None
            sin = sin_ref[r0:r0 + step, :] if has_rope else None
            outs = _norm_rope(chunks, gains, cos, sin, rope_flags, swap_kind, dim)
            for c, o in enumerate(outs):
                k_scr[r0:r0 + step, c * LANE:(c + 1) * LANE] = o
            v_scr[r0:r0 + step, :LANE] = v_ref[r0:r0 + step, :].astype(BF16)
            v_scr[r0:r0 + step, LANE:] = jnp.ones((step, LANE), BF16)

    if window == "full":
        kwin = k_scr[...]
        vwin = v_scr[...]
    else:
        rows = seq // GRID_W
        start_row = jnp.clip(qb * C_QROWS - C_WIN_R // 2, 0, rows - kw // GRID_W)
        start = pl.multiple_of(start_row * GRID_W, GRID_W)
        kwin = k_scr[pl.ds(start, kw), :]
        vwin = v_scr[pl.ds(start, kw), :]

    row0 = pl.multiple_of(qb * bq, bq)
    gains = [gq_ref[:, c * LANE:(c + 1) * LANE] for c in range(n_chunk)]
    for r0 in range(0, bq, rc):
        cos = cos_ref[pl.ds(row0 + r0, rc), :] if has_rope else None
        sin = sin_ref[pl.ds(row0 + r0, rc), :] if has_rope else None
        for g in range(n_grp):
            base = g * n_chunk
            chunks = [q_ref[r0:r0 + rc, (base + c) * LANE:(base + c + 1) * LANE] for c in range(n_chunk)]
            outs = _norm_rope(chunks, gains, cos, sin, rope_flags, swap_kind, dim)
            q = outs[0] if n_chunk == 1 else jnp.concatenate(outs, axis=1)
            s = lax.dot_general(q, kwin, (((1,), (1,)), ((), ())), preferred_element_type=F32)
            if tmode == "add":
                t = t_ref[0, 0, r0:r0 + rc, :]
                s = jnp.where(t > -1e29, s * scale + t, NEG_INF)
                p = jnp.exp(s - jnp.max(s, axis=-1, keepdims=True))
            else:
                if tmode == "mul":
                    t = t_ref[r0:r0 + rc, :]
                    s = jnp.where(t > 0.0, s, NEG_INF)
                p = jnp.exp2((s - jnp.max(s, axis=-1, keepdims=True)) * (scale * LOG2_E))
                if tmode == "mul":
                    p = p * t
            o = jnp.dot(p.astype(BF16), vwin, preferred_element_type=F32)
            o_ref[r0:r0 + rc, g * LANE:(g + 1) * LANE] = o[:, :LANE] / o[:, LANE:]


def _attention(q_arr, q_col0, k_arrs, v_arr, v_col0, gq, gk, cos, sin, table, *, batch, seq, n_kv, n_grp,
               rope_flags, swap_kind, dim, bq, rc, window, tmode, name):
    n_chunk = len(k_arrs)
    nqb = seq // bq
    kw = seq if window == "full" else min(C_KROWS, seq // GRID_W) * GRID_W
    qw = n_grp * n_chunk * LANE
    q_blk0 = q_col0 * LANE // qw
    assert q_blk0 * qw == q_col0 * LANE
    in_specs = [pl.BlockSpec((bq, qw), lambda b, g, i: (b * nqb + i, q_blk0 + g))]
    args = [q_arr]
    for arr, col0, per_head in k_arrs:
        in_specs.append(pl.BlockSpec((seq, LANE), functools.partial(
            lambda b, g, i, col0, per_head: (b, col0 + g * per_head), col0=col0, per_head=per_head)))
        args.append(arr)
    in_specs.append(pl.BlockSpec((seq, LANE), lambda b, g, i: (b, v_col0 + g)))
    args.append(v_arr)
    in_specs.append(pl.BlockSpec((1, n_chunk * LANE), lambda b, g, i: (0, 0)))
    args.append(gq.reshape(1, n_chunk * LANE))
    in_specs.append(pl.BlockSpec((1, n_chunk * LANE), lambda b, g, i: (0, 0)))
    args.append(gk.reshape(1, n_chunk * LANE))
    if any(rope_flags):
        in_specs += [pl.BlockSpec((seq, LANE), lambda b, g, i: (0, 0))] * 2
        args += [cos, sin]
    if tmode == "add":
        in_specs.append(pl.BlockSpec((1, 1, bq, kw), lambda b, g, i: (g, i, 0, 0)))
        args.append(table)
    elif tmode == "mul":
        in_specs.append(pl.BlockSpec((bq, kw), lambda b, g, i: (i, 0)))
        args.append(table)
    kern = functools.partial(
        _attn_kernel, n_grp=n_grp, n_chunk=n_chunk, rope_flags=rope_flags, swap_kind=swap_kind, dim=dim,
        scale=dim ** -0.5, bq=bq, rc=min(rc, bq), seq=seq, kw=kw, window=window, tmode=tmode)
    return pl.pallas_call(
        kern,
        name=name,
        grid=(batch, n_kv, nqb),
        in_specs=in_specs,
        out_specs=pl.BlockSpec((bq, n_grp * LANE), lambda b, g, i: (b * nqb + i, g)),
        out_shape=jax.ShapeDtypeStruct((batch * seq, n_kv * n_grp * LANE), F32),
        scratch_shapes=[pltpu.VMEM((seq, n_chunk * LANE), BF16), pltpu.VMEM((seq, 2 * LANE), BF16)],
        compiler_params=_params(3),
    )(*args)


def _rope_cs(pos, half):
    inv_freq = ROPE_THETA ** (-jnp.arange(half, dtype=F32) / half)
    ang = pos.astype(F32)[:, None] * inv_freq[None, :]
    return jnp.cos(ang), jnp.sin(ang)


def _rope_tables(seq):
    t = jnp.arange(seq)
    c, s = _rope_cs(t, HEAD_DIM // 2)
    a_cos, a_sin = jnp.concatenate([c, c], -1), jnp.concatenate([-s, s], -1)
    cr, sr = _rope_cs(t // GRID_W, HEAD_DIM // 4)
    cc, sc = _rope_cs(t % GRID_W, HEAD_DIM // 4)
    b_cos = jnp.concatenate([cr, cr, cc, cc], -1)
    b_sin = jnp.concatenate([-sr, sr, -sc, sc], -1)
    cd, sd = _rope_cs(t, D_ROPE // 2)
    z = jnp.zeros((seq, LANE - D_ROPE), F32)
    d_cos = jnp.concatenate([cd, cd, z], -1)
    d_sin = jnp.concatenate([-sd, sd, z], -1)
    return (a_cos, a_sin), (b_cos, b_sin), (d_cos, d_sin)


def _dilation_multiplicity(seq):
    t = jnp.arange(seq)
    delta = t[None, :] - t[:, None]
    mult = jnp.zeros((seq, seq), jnp.int32)
    for window, d in A_PATTERNS:
        half = window // (2 * d)
        mult = mult + ((delta % d == 0) & (jnp.abs(delta) <= half * d)).astype(jnp.int32)
    return mult.astype(F32)


def _neighbourhood_table(rpb_l, seq):
    heads = rpb_l.shape[0]
    rows = seq // GRID_W
    wr = min(C_WIN_R, rows)
    kr = min(C_KROWS, rows)
    nqb = rows // C_QROWS
    qb = jnp.arange(nqb)
    r = qb[:, None] * C_QROWS + jnp.arange(C_QROWS)[None, :]
    win0 = jnp.clip(qb * C_QROWS - C_WIN_R // 2, 0, rows - kr)
    krow = win0[:, None] + jnp.arange(kr)[None, :]
    row_start = jnp.clip(r - wr // 2, 0, rows - wr)
    row_ok = (krow[:, None, :] >= row_start[:, :, None]) & (krow[:, None, :] < row_start[:, :, None] + wr)
    drow = jnp.clip(krow[:, None, :] - r[:, :, None] + (C_WIN_R - 1), 0, 2 * C_WIN_R - 2)
    c = jnp.arange(GRID_W)
    col_start = jnp.clip(c - C_WIN_C // 2, 0, GRID_W - C_WIN_C)
    col_ok = (c[None, :] >= col_start[:, None]) & (c[None, :] < col_start[:, None] + C_WIN_C)
    dcol = jnp.clip(c[None, :] - c[:, None] + (C_WIN_C - 1), 0, 2 * C_WIN_C - 2)
    by_col = jnp.take(rpb_l, dcol.reshape(-1), axis=2).reshape(heads, 2 * C_WIN_R - 1, GRID_W, GRID_W)
    by_row = jnp.take(by_col, drow.reshape(-1), axis=1)
    bias = by_row.reshape(heads, nqb, C_QROWS, kr, GRID_W, GRID_W).transpose(0, 1, 2, 4, 3, 5)
    ok = row_ok[None, :, :, None, :, None] & col_ok[None, None, None, :, None, :]
    table = jnp.where(ok, bias, NEG_INF)
    return table.reshape(heads, nqb, C_QROWS * GRID_W, kr * GRID_W)


def _mixer(x2, h, lw, tabs, batch, seq):
    d_model = x2.shape[1]
    gw = d_model // N_MIXERS
    heads = gw // HEAD_DIM
    kv_heads = heads // 4
    nb = gw // LANE
    (a_cos, a_sin), (b_cos, b_sin), (d_cos, d_sin) = tabs["rope"]

    proj = _matmul(h, lw["w_main"])
    kpe = _matmul(h, lw["w_kpe"], bn=LANE)
    col = {"cq": 0, "qa": nb, "ka": 2 * nb, "va": 3 * nb, "qc": 4 * nb, "kc": 5 * nb, "vc": 6 * nb,
           "qb": 7 * nb}
    col["ckv"] = 8 * nb
    col["kb"] = col["ckv"] + D_KV_RANK // LANE
    col["vb"] = col["kb"] + kv_heads

    g_abc = lw["g_qk_abc"]
    common = dict(batch=batch, seq=seq)

    oa = _attention(proj, col["qa"], [(proj, col["ka"], 1)], proj, col["va"], g_abc[0, 0], g_abc[0, 1],
                    a_cos, a_sin, tabs["mult"], n_kv=heads, n_grp=1, rope_flags=(True,), swap_kind="half64",
                    dim=HEAD_DIM, bq=min(512, seq), rc=ATTN_ROWS, window="full", tmode="mul", name="attn_a",
                    **common)
    ob = _attention(proj, col["qb"], [(proj, col["kb"], 1)], proj, col["vb"], g_abc[1, 0], g_abc[1, 1],
                    b_cos, b_sin, None, n_kv=kv_heads, n_grp=4, rope_flags=(True,), swap_kind="half32",
                    dim=HEAD_DIM, bq=min(128, seq), rc=ATTN_ROWS, window="full", tmode=None, name="attn_b",
                    **common)
    oc = _attention(proj, col["qc"], [(proj, col["kc"], 1)], proj, col["vc"], g_abc[2, 0], g_abc[2, 1],
                    None, None, lw["nbr_table"], n_kv=heads, n_grp=1, rope_flags=(False,), swap_kind=None,
                    dim=HEAD_DIM, bq=C_QROWS * GRID_W, rc=ATTN_ROWS, window="rows", tmode="add", name="attn_c",
                    **common)

    cq_n = _rmsnorm([(proj, 0, gw)], lw["g_cq"])
    ckv_n = _rmsnorm([(proj, col["ckv"] * LANE // D_KV_RANK, D_KV_RANK)], lw["g_ckv"])
    qd = _matmul(cq_n, lw["w_uq"])
    kvd = _matmul(ckv_n, lw["w_ukv"])
    od = _attention(qd, 0, [(kvd, 0, 1), (kpe, 0, 0)], kvd, heads, lw["gq_mla"], lw["gk_mla"],
                    d_cos, d_sin, None, n_kv=heads, n_grp=1, rope_flags=(False, True), swap_kind="half32",
                    dim=D_QK, bq=min(512, seq), rc=ATTN_ROWS, window="full", tmode=None, name="attn_d",
                    **common)

    normed = _rmsnorm([(oa, 0, gw), (ob, 0, gw), (oc, 0, gw), (od, 0, gw)], lw["g_grp"])
    return _matmul(normed, lw["w_out"], res=x2)


def _layer_weights(l, seq, g_qk_abc, w_in, w_uq, w_ukv, g_cq, g_ckv, g_qk_mla, rpb, g_grp, w_out):
    d_model = w_in.shape[1]
    gw = d_model // N_MIXERS
    heads = gw // HEAD_DIM
    kvw = (heads // 4) * HEAD_DIM
    q_rank = w_uq.shape[1]
    sizes = (gw, gw, gw, gw, kvw, kvw, gw, gw, gw, q_rank, D_KV_RANK, D_ROPE)
    offs = [0]
    for s in sizes:
        offs.append(offs[-1] + s)
    w = w_in[l]
    qa, ka, va, qb, kb, vb, qc, kc, vc, cq, ckv, kpe = [w[:, offs[i]:offs[i + 1]] for i in range(len(sizes))]
    w_main = jnp.concatenate([cq, qa, ka, va, qc, kc, vc, qb, ckv, kb, vb], axis=1).astype(BF16)
    w_kpe = jnp.pad(kpe, ((0, 0), (0, LANE - D_ROPE))).astype(BF16)
    uq = w_uq[l].reshape(q_rank, heads, D_QK)
    uq = jnp.pad(uq, ((0, 0), (0, 0), (0, 2 * LANE - D_QK))).reshape(q_rank, heads * 2 * LANE).astype(BF16)
    ukv = w_ukv[l].reshape(D_KV_RANK, heads, D_NOPE + HEAD_DIM)
    ukv = jnp.concatenate([ukv[:, :, :D_NOPE].reshape(D_KV_RANK, heads * D_NOPE),
                           ukv[:, :, D_NOPE:].reshape(D_KV_RANK, heads * HEAD_DIM)], axis=1).astype(BF16)
    pad_g = lambda g: jnp.pad(g, (0, 2 * LANE - D_QK))
    return dict(w_main=w_main, w_kpe=w_kpe, w_uq=uq, w_ukv=ukv, g_qk_abc=g_qk_abc[l], g_cq=g_cq[l],
                g_ckv=g_ckv[l], gq_mla=pad_g(g_qk_mla[l, 0]), gk_mla=pad_g(g_qk_mla[l, 1]),
                nbr_table=_neighbourhood_table(rpb[l], seq), g_grp=g_grp[l], w_out=w_out[l].astype(BF16))


def _pad_axis(w, axis, mult):
    extra = (-w.shape[axis]) % mult
    if not extra:
        return w
    pads = [(0, 0)] * w.ndim
    pads[axis] = (0, extra)
    return jnp.pad(w, pads)


def kernel(x, g_mix, w_in, g_qk_abc, w_uq, w_ukv, g_cq, g_ckv, g_qk_mla, rpb, g_grp, w_out, g_ffn, w_gate,
           w_up, w_down, w_router, we_gate, we_up, we_down):
    batch, seq, d_model = x.shape
    depth = g_mix.shape[0]
    n_exp = we_gate.shape[1]
    tabs = dict(rope=_rope_tables(seq), mult=_dilation_multiplicity(seq))
    x2 = x.reshape(batch * seq, d_model)
    for l in range(depth):
        lw = _layer_weights(l, seq, g_qk_abc, w_in, w_uq, w_ukv, g_cq, g_ckv, g_qk_mla, rpb, g_grp, w_out)
        h = _rmsnorm([(x2, 0, d_model)], g_mix[l])
        x2 = _mixer(x2, h, lw, tabs, batch, seq)
        i = l // 2
        if l % 2 == 0:
            h = _rmsnorm([(x2, 0, d_model)], g_ffn[l])
            wg = _pad_axis(w_gate[i].astype(BF16), 1, 512)
            wu = _pad_axis(w_up[i].astype(BF16), 1, 512)
            wd = _pad_axis(w_down[i].astype(BF16), 0, 512)
            a = _gateup_dense(h, wg, wu)
            x2 = _matmul(a, wd, res=x2, bn=1024, bk=2816)
        else:
            h32 = _rmsnorm([(x2, 0, d_model)], g_ffn[l], out_dtype=F32)
            wr = jnp.pad(w_router[i], ((0, 0), (0, LANE - n_exp))).astype(BF16)
            x2 = _moe(x2, h32, wr, we_gate[i].astype(BF16), we_up[i].astype(BF16), we_down[i].astype(BF16))
    return x2.reshape(batch, seq, d_model)
```

```python
import functools
import math

import jax
import jax.numpy as jnp
from jax import lax
from jax.experimental import pallas as pl
from jax.experimental.pallas import tpu as pltpu

F32 = jnp.float32
BF16 = jnp.bfloat16

LANE = 128
HEAD_DIM = 128
N_MIXERS = 4
ROPE_THETA = 10000.0
EPS = 1e-6
NEG_INF = -1e30
LOG2_E = 1.4426950408889634
GRID_W = 64
A_PATTERNS = ((128, 1), (512, 4), (2048, 16))
C_WIN_R = 8
C_WIN_C = 16
C_QROWS = 4
C_KROWS = 12
ATTN_ROWS = 128
D_KV_RANK = 512
D_NOPE = 128
D_ROPE = 64
D_QK = D_NOPE + D_ROPE
TOP_K = 2
CAST_ROWS = 512
VMEM_LIMIT_BYTES = 56 * 1024 * 1024


def _blk(dim, target, mult=LANE):
    best = None
    d = mult
    while d <= min(dim, target):
        if dim % d == 0:
            best = d
        d += mult
    return best if best is not None else dim


def _params(n_grid):
    return pltpu.CompilerParams(dimension_semantics=("arbitrary",) * n_grid,
                                vmem_limit_bytes=VMEM_LIMIT_BYTES)


def _stage_bf16(w_ref, scr):
    rows = scr.shape[0]
    step = min(rows, CAST_ROWS)
    for r0 in range(0, rows, step):
        scr[r0:r0 + step, :] = w_ref[0, r0:r0 + step, :].astype(BF16)


def _rmsnorm_kernel(*refs, group_sizes, split_rows):
    n_in = sum(group_sizes)
    x_refs = refs[:n_in]
    g_ref = refs[n_in]
    o_ref = refs[n_in + 1]
    off = 0
    k = 0
    for size in group_sizes:
        pieces = [x_refs[k + p][...] for p in range(size)]
        k += size
        width = sum(p.shape[-1] for p in pieces)
        ssq = None
        for p in pieces:
            t = jnp.sum(p * p, axis=-1, keepdims=True)
            ssq = t if ssq is None else ssq + t
        r = lax.rsqrt(ssq / width + EPS)
        for p in pieces:
            w = p.shape[-1]
            y = (p * r) * g_ref[:, off:off + w]
            if split_rows:
                o_ref[...] = y.reshape(o_ref.shape)
            else:
                o_ref[:, off:off + w] = y.astype(o_ref.dtype)
            off += w


def _pieces(arr, col_off, width):
    pw = math.gcd(col_off, width) if col_off else width
    return [(arr, col_off // pw + p, pw) for p in range(width // pw)]


def _rmsnorm(groups, gain, bm=256, out_dtype=BF16, split_rows=False):
    flat = [p for g in groups for p in g]
    n = flat[0][0].shape[0]
    bm = _blk(n, bm, 8)
    total = sum(w for _, _, w in flat)
    in_specs = [pl.BlockSpec((bm, w), functools.partial(lambda i, c: (i, c), c=c)) for _, c, w in flat]
    in_specs.append(pl.BlockSpec((1, total), lambda i: (0, 0)))
    if split_rows:
        assert len(flat) == 1
        out_spec = pl.BlockSpec((bm, total // LANE, LANE), lambda i: (i, 0, 0))
        out_shape = jax.ShapeDtypeStruct((n, total // LANE, LANE), F32)
    else:
        out_spec = pl.BlockSpec((bm, total), lambda i: (i, 0))
        out_shape = jax.ShapeDtypeStruct((n, total), out_dtype)
    return pl.pallas_call(
        functools.partial(_rmsnorm_kernel, group_sizes=tuple(len(g) for g in groups), split_rows=split_rows),
        grid=(n // bm,),
        in_specs=in_specs,
        out_specs=out_spec,
        out_shape=out_shape,
        compiler_params=_params(1),
        name="rmsnorm",
    )(*[a for a, _, _ in flat], gain.reshape(1, total))


def _mm_kernel(a_ref, w_ref, *rest, nk, has_res):
    if has_res:
        r_ref, o_ref = rest
    else:
        (o_ref,) = rest
    part = jnp.dot(a_ref[...], w_ref[...], preferred_element_type=F32)
    if nk == 1:
        o_ref[...] = (r_ref[...] + part) if has_res else part
        return
    k = pl.program_id(2)

    @pl.when(k == 0)
    def _():
        o_ref[...] = (r_ref[...] + part) if has_res else part

    @pl.when(k > 0)
    def _():
        o_ref[...] += part


def _matmul(a, w, res=None, bm=1024, bn=512, bk=None, name="matmul"):
    m, kdim = a.shape
    n = w.shape[1]
    bm = _blk(m, bm, 8)
    bn = _blk(n, bn)
    bk = kdim if bk is None else _blk(kdim, bk)
    nk = kdim // bk
    in_specs = [pl.BlockSpec((bm, bk), lambda i, j, k: (i, k)),
                pl.BlockSpec((bk, bn), lambda i, j, k: (k, j))]
    args = [a, w]
    if res is not None:
        in_specs.append(pl.BlockSpec((bm, bn), lambda i, j, k: (i, j)))
        args.append(res)
    return pl.pallas_call(
        functools.partial(_mm_kernel, nk=nk, has_res=res is not None),
        grid=(m // bm, n // bn, nk),
        in_specs=in_specs,
        out_specs=pl.BlockSpec((bm, bn), lambda i, j, k: (i, j)),
        out_shape=jax.ShapeDtypeStruct((m, n), F32),
        compiler_params=_params(3),
        name=name,
    )(*args)


def _mm_ws_kernel(a_ref, w_ref, *rest, has_res):
    if has_res:
        r_ref, o_ref, w_scr = rest
    else:
        o_ref, w_scr = rest

    @pl.when(pl.program_id(1) == 0)
    def _():
        _stage_bf16(w_ref, w_scr)

    part = jnp.dot(a_ref[...], w_scr[...], preferred_element_type=F32)
    o_ref[...] = (r_ref[...] + part) if has_res else part


def _matmul_ws(a, w3, layer, n_cols, res=None, bm=1024, bn=512, name="matmul_ws"):
    m, kdim = a.shape
    bm = _blk(m, bm, 8)
    bn = _blk(n_cols, bn)
    in_specs = [pl.BlockSpec((bm, kdim), lambda j, i: (i, 0)),
                pl.BlockSpec((1, kdim, bn), lambda j, i: (layer, 0, j))]
    args = [a, w3]
    if res is not None:
        in_specs.append(pl.BlockSpec((bm, bn), lambda j, i: (i, j)))
        args.append(res)
    return pl.pallas_call(
        functools.partial(_mm_ws_kernel, has_res=res is not None),
        grid=(n_cols // bn, m // bm),
        in_specs=in_specs,
        out_specs=pl.BlockSpec((bm, bn), lambda j, i: (i, j)),
        out_shape=jax.ShapeDtypeStruct((m, n_cols), F32),
        scratch_shapes=[pltpu.VMEM((kdim, bn), BF16)],
        compiler_params=_params(2),
        name=name,
    )(*args)


def _swiglu(g, u):
    return (g / (1.0 + jnp.exp(-g))) * u


def _gateup_kernel(h_ref, wg_ref, wu_ref, o_ref, wg_scr, wu_scr):
    @pl.when(pl.program_id(1) == 0)
    def _():
        _stage_bf16(wg_ref, wg_scr)
        _stage_bf16(wu_ref, wu_scr)

    h = h_ref[...]
    g = jnp.dot(h, wg_scr[...], preferred_element_type=F32)
    u = jnp.dot(h, wu_scr[...], preferred_element_type=F32)
    o_ref[...] = _swiglu(g, u).astype(o_ref.dtype)


def _gateup_dense(h, wg3, wu3, layer, bm=1024, bn=256):
    m, d = h.shape
    f = wg3.shape[2]
    bm = _blk(m, bm, 8)
    bn = _blk(f, bn)
    w_spec = pl.BlockSpec((1, d, bn), lambda j, i: (layer, 0, j))
    return pl.pallas_call(
        _gateup_kernel,
        grid=(f // bn, m // bm),
        in_specs=[pl.BlockSpec((bm, d), lambda j, i: (i, 0)), w_spec, w_spec],
        out_specs=pl.BlockSpec((bm, bn), lambda j, i: (i, j)),
        out_shape=jax.ShapeDtypeStruct((m, f), BF16),
        scratch_shapes=[pltpu.VMEM((d, bn), BF16), pltpu.VMEM((d, bn), BF16)],
        compiler_params=_params(2),
        name="ffn_gateup",
    )(h, wg3, wu3)


ROUTE_IDX1, ROUTE_IDX2, ROUTE_G1, ROUTE_G2, ROUTE_RANK1, ROUTE_RANK2 = range(6)


def _router_kernel(h_ref, w_ref, route_ref, cnt_ref, carry, *, n_exp):
    @pl.when(pl.program_id(0) == 0)
    def _():
        carry[...] = jnp.zeros_like(carry)

    bm = h_ref.shape[0]
    h = h_ref[...].reshape(bm, h_ref.shape[1] * LANE).astype(BF16)
    logits = jnp.dot(h, w_ref[...], preferred_element_type=F32)
    lane = lax.broadcasted_iota(jnp.int32, logits.shape, 1).astype(F32)
    logits = jnp.where(lane < n_exp, logits, -jnp.inf)
    v1 = jnp.max(logits, axis=-1, keepdims=True)
    i1 = jnp.min(jnp.where(logits == v1, lane, float(LANE)), axis=-1, keepdims=True)
    rest = jnp.where(lane == i1, -jnp.inf, logits)
    v2 = jnp.max(rest, axis=-1, keepdims=True)
    i2 = jnp.min(jnp.where(rest == v2, lane, float(LANE)), axis=-1, keepdims=True)
    e2 = jnp.exp(v2 - v1)
    den = 1.0 + e2
    g1 = 1.0 / den
    g2 = e2 / den
    chosen = jnp.where((lane == i1) | (lane == i2), 1.0, 0.0)
    row = lax.broadcasted_iota(jnp.int32, (bm, bm), 0)
    col = lax.broadcasted_iota(jnp.int32, (bm, bm), 1)
    earlier = jnp.where(col < row, 1.0, 0.0).astype(BF16)
    before = jnp.dot(earlier, chosen.astype(BF16), preferred_element_type=F32) + carry[...]
    rank1 = jnp.sum(jnp.where(lane == i1, before, 0.0), axis=-1, keepdims=True)
    rank2 = jnp.sum(jnp.where(lane == i2, before, 0.0), axis=-1, keepdims=True)
    carry[...] += jnp.sum(chosen, axis=0, keepdims=True)
    route = jnp.zeros_like(logits)
    for pos, val in ((ROUTE_IDX1, i1), (ROUTE_IDX2, i2), (ROUTE_G1, g1), (ROUTE_G2, g2),
                     (ROUTE_RANK1, rank1), (ROUTE_RANK2, rank2)):
        route = jnp.where(lane == pos, val, route)
    route_ref[...] = route
    cnt_ref[...] = jnp.broadcast_to(carry[...], cnt_ref.shape)


def _router(h3, w_router_padded, n_exp, bm=512):
    m, dl, _ = h3.shape
    bm = _blk(m, bm, 8)
    return pl.pallas_call(
        functools.partial(_router_kernel, n_exp=n_exp),
        grid=(m // bm,),
        in_specs=[pl.BlockSpec((bm, dl, LANE), lambda i: (i, 0, 0)),
                  pl.BlockSpec((dl * LANE, LANE), lambda i: (0, 0))],
        out_specs=[pl.BlockSpec((bm, LANE), lambda i: (i, 0)),
                   pl.BlockSpec((8, LANE), lambda i: (0, 0))],
        out_shape=[jax.ShapeDtypeStruct((m, LANE), F32), jax.ShapeDtypeStruct((8, LANE), F32)],
        scratch_shapes=[pltpu.VMEM((1, LANE), F32)],
        compiler_params=_params(1),
        name="router",
    )(h3, w_router_padded)


def _gather_kernel(s1_ref, s2_ref, te_ref, offs_ref, cnt_ref, nu_ref, src_ref, o_ref, tok_scr, buf, sem, *,
                   tb, bm, n_tiles, n_tok):
    i = pl.program_id(0)

    @pl.when(i == 0)
    def _():
        def place(t, c):
            tok_scr[s1_ref[t]] = t
            tok_scr[s2_ref[t]] = t
            return c

        lax.fori_loop(0, n_tok, place, 0, unroll=8)

    def used(tile):
        return tile * tb < nu_ref[0] * bm

    def row_copy(r, slot, t):
        return pltpu.make_async_copy(src_ref.at[t], buf.at[slot, r], sem.at[slot])

    def issue(tile, slot):
        e = te_ref[(tile * tb) // bm]
        first = offs_ref[e]
        last = first + cnt_ref[e] - 1

        def body(r, c):
            s = tile * tb + r
            s = jnp.where(s <= last, s, first)
            row_copy(r, slot, tok_scr[s]).start()
            return c

        lax.fori_loop(0, tb, body, 0, unroll=8)

    @pl.when((i == 0) & used(0))
    def _():
        issue(0, 0)

    @pl.when((i + 1 < n_tiles) & used(i + 1))
    def _():
        issue(i + 1, (i + 1) % 2)

    slot = i % 2

    @pl.when(used(i))
    def _():
        def wait(r, c):
            row_copy(r, slot, 0).wait()
            return c

        lax.fori_loop(0, tb, wait, 0, unroll=8)
        o_ref[...] = buf[slot].reshape(o_ref.shape).astype(o_ref.dtype)

    @pl.when(jnp.logical_not(used(i)))
    def _():
        o_ref[...] = jnp.zeros_like(o_ref)


def _gather_rows(slot1, slot2, tile_expert, offs, counts, n_used, src3, n_slot, bm, tb=256):
    n_tok, dl, _ = src3.shape
    tb = min(tb, bm)
    n_tiles = n_slot // tb
    return pl.pallas_call(
        functools.partial(_gather_kernel, tb=tb, bm=bm, n_tiles=n_tiles, n_tok=n_tok),
        grid_spec=pltpu.PrefetchScalarGridSpec(
            num_scalar_prefetch=6,
            grid=(n_tiles,),
            in_specs=[pl.BlockSpec(memory_space=pl.ANY)],
            out_specs=pl.BlockSpec((tb, dl * LANE), lambda i, *_: (i, 0)),
            scratch_shapes=[pltpu.SMEM((n_slot,), jnp.int32), pltpu.VMEM((2, tb, dl, LANE), F32),
                            pltpu.SemaphoreType.DMA((2,))]),
        out_shape=jax.ShapeDtypeStruct((n_slot, dl * LANE), BF16),
        compiler_params=_params(1),
        name="gather_rows",
    )(slot1, slot2, tile_expert, offs, counts, n_used, src3)


def _new_expert(te_ref, i):
    return (i == 0) | (te_ref[i] != te_ref[jnp.maximum(i - 1, 0)])


def _expert_gateup_kernel(te_ref, nu_ref, x_ref, wg_ref, wu_ref, o_ref, wg_scr, wu_scr):
    i = pl.program_id(1)

    @pl.when(_new_expert(te_ref, i))
    def _():
        _stage_bf16(wg_ref, wg_scr)
        _stage_bf16(wu_ref, wu_scr)

    @pl.when(i < nu_ref[0])
    def _():
        x = x_ref[...]
        g = jnp.dot(x, wg_scr[...], preferred_element_type=F32)
        u = jnp.dot(x, wu_scr[...], preferred_element_type=F32)
        o_ref[...] = _swiglu(g, u).astype(o_ref.dtype)

    @pl.when(i >= nu_ref[0])
    def _():
        o_ref[...] = jnp.zeros_like(o_ref)


def _expert_gateup(tile_expert, n_used, xs, wg4, wu4, layer, bm, bn=256):
    p, d = xs.shape
    ef = wg4.shape[3]
    bn = _blk(ef, bn)
    w_spec = pl.BlockSpec((None, 1, d, bn), lambda j, i, te, nu: (layer, te[i], 0, j))
    return pl.pallas_call(
        _expert_gateup_kernel,
        grid_spec=pltpu.PrefetchScalarGridSpec(
            num_scalar_prefetch=2,
            grid=(ef // bn, p // bm),
            in_specs=[pl.BlockSpec((bm, d), lambda j, i, te, nu: (i, 0)), w_spec, w_spec],
            out_specs=pl.BlockSpec((bm, bn), lambda j, i, te, nu: (i, j)),
            scratch_shapes=[pltpu.VMEM((d, bn), BF16), pltpu.VMEM((d, bn), BF16)]),
        out_shape=jax.ShapeDtypeStruct((p, ef), BF16),
        compiler_params=_params(2),
        name="expert_gateup",
    )(tile_expert, n_used, xs, wg4, wu4)


def _expert_down_kernel(te_ref, nu_ref, a_ref, w_ref, o_ref, w_scr):
    i = pl.program_id(1)

    @pl.when(_new_expert(te_ref, i))
    def _():
        _stage_bf16(w_ref, w_scr)

    @pl.when(i < nu_ref[0])
    def _():
        y = jnp.dot(a_ref[...], w_scr[...], preferred_element_type=F32)
        o_ref[...] = y.reshape(o_ref.shape)

    @pl.when(i >= nu_ref[0])
    def _():
        o_ref[...] = jnp.zeros_like(o_ref)


def _expert_down(tile_expert, n_used, a, wd4, layer, bm, bn=1024):
    p, ef = a.shape
    d = wd4.shape[3]
    bn = _blk(d, bn, 8 * LANE)
    return pl.pallas_call(
        _expert_down_kernel,
        grid_spec=pltpu.PrefetchScalarGridSpec(
            num_scalar_prefetch=2,
            grid=(d // bn, p // bm),
            in_specs=[pl.BlockSpec((bm, ef), lambda j, i, te, nu: (i, 0)),
                      pl.BlockSpec((None, 1, ef, bn), lambda j, i, te, nu: (layer, te[i], 0, j))],
            out_specs=pl.BlockSpec((bm, bn // LANE, LANE), lambda j, i, te, nu: (i, j, 0)),
            scratch_shapes=[pltpu.VMEM((ef, bn), BF16)]),
        out_shape=jax.ShapeDtypeStruct((p, d // LANE, LANE), F32),
        compiler_params=_params(2),
        name="expert_down",
    )(tile_expert, n_used, a, wd4)


def _combine_kernel(s1_ref, s2_ref, y_ref, x_ref, route_ref, o_ref, buf, sem, *, tb, n_tiles):
    i = pl.program_id(0)

    def row_copy(k, r, slot, s):
        return pltpu.make_async_copy(y_ref.at[s], buf.at[slot, k, r], sem.at[slot])

    def issue(tile, slot):
        def body(r, c):
            t = tile * tb + r
            row_copy(0, r, slot, s1_ref[t]).start()
            row_copy(1, r, slot, s2_ref[t]).start()
            return c

        lax.fori_loop(0, tb, body, 0, unroll=4)

    @pl.when(i == 0)
    def _():
        issue(0, 0)

    @pl.when(i + 1 < n_tiles)
    def _():
        issue(i + 1, (i + 1) % 2)

    slot = i % 2

    def wait(r, c):
        row_copy(0, r, slot, 0).wait()
        row_copy(1, r, slot, 0).wait()
        return c

    lax.fori_loop(0, tb, wait, 0, unroll=4)
    route = route_ref[...]
    g1 = route[:, ROUTE_G1:ROUTE_G1 + 1]
    g2 = route[:, ROUTE_G2:ROUTE_G2 + 1]
    y1 = buf[slot, 0].reshape(o_ref.shape)
    y2 = buf[slot, 1].reshape(o_ref.shape)
    o_ref[...] = x_ref[...] + (g1 * y1 + g2 * y2)


def _combine(slot1, slot2, y3, x2, route, tb=128):
    n, d = x2.shape
    dl = d // LANE
    tb = _blk(n, tb, 8)
    n_tiles = n // tb
    return pl.pallas_call(
        functools.partial(_combine_kernel, tb=tb, n_tiles=n_tiles),
        grid_spec=pltpu.PrefetchScalarGridSpec(
            num_scalar_prefetch=2,
            grid=(n_tiles,),
            in_specs=[pl.BlockSpec(memory_space=pl.ANY),
                      pl.BlockSpec((tb, d), lambda i, s1, s2: (i, 0)),
                      pl.BlockSpec((tb, LANE), lambda i, s1, s2: (i, 0))],
            out_specs=pl.BlockSpec((tb, d), lambda i, s1, s2: (i, 0)),
            scratch_shapes=[pltpu.VMEM((2, 2, tb, dl, LANE), F32), pltpu.SemaphoreType.DMA((2,))]),
        out_shape=jax.ShapeDtypeStruct((n, d), F32),
        compiler_params=_params(1),
        name="expert_combine",
    )(slot1, slot2, y3, x2, route)


def _moe(x2, h3, w_router_padded, wg4, wu4, wd4, layer, bm=512):
    n = x2.shape[0]
    n_exp = wg4.shape[1]
    bm = _blk(n, bm, 8)
    route, cnt = _router(h3, w_router_padded, n_exp)
    counts = cnt[0, :n_exp].astype(jnp.int32)
    padded = ((counts + bm - 1) // bm) * bm
    ends = jnp.cumsum(padded)
    offs = ends - padded
    idx1 = route[:, ROUTE_IDX1].astype(jnp.int32)
    idx2 = route[:, ROUTE_IDX2].astype(jnp.int32)
    slot1 = offs[idx1] + route[:, ROUTE_RANK1].astype(jnp.int32)
    slot2 = offs[idx2] + route[:, ROUTE_RANK2].astype(jnp.int32)
    n_slot = n * TOP_K + n_exp * bm
    n_tiles = n_slot // bm
    tile_start = jnp.arange(n_tiles, dtype=jnp.int32) * bm
    tile_expert = jnp.minimum(jnp.sum((ends[None, :] <= tile_start[:, None]).astype(jnp.int32), axis=1), n_exp - 1)
    n_used = (ends[-1] // bm).reshape(1)
    xs = _gather_rows(slot1, slot2, tile_expert, offs, counts, n_used, h3, n_slot, bm)
    a = _expert_gateup(tile_expert, n_used, xs, wg4, wu4, layer, bm)
    y3 = _expert_down(tile_expert, n_used, a, wd4, layer, bm)
    return _combine(slot1, slot2, y3, x2, route)


def _swap(y, kind):
    if kind == "half64":
        return pltpu.roll(y, 64, 1)
    lane = lax.broadcasted_iota(jnp.int32, y.shape, 1)
    lo = pltpu.roll(y, 96, 1)
    hi = pltpu.roll(y, 32, 1)
    return jnp.where((lane & 63) < 32, lo, hi)


def _norm_rope(chunks, gains, cos, sin, rope_flags, swap_kind, dim):
    ssq = None
    for c in chunks:
        t = jnp.sum(c * c, axis=-1, keepdims=True)
        ssq = t if ssq is None else ssq + t
    r = lax.rsqrt(ssq / dim + EPS)
    outs = []
    for c, g, flag in zip(chunks, gains, rope_flags):
        y = (c * r) * g
        if flag:
            y = y * cos + _swap(y, swap_kind) * sin
        outs.append(y.astype(BF16))
    return outs


def _attn_kernel(*refs, n_grp, n_chunk, rope_flags, swap_kind, dim, scale, bq, rc, seq, kw, window, tmode):
    has_rope = any(rope_flags)
    it = iter(refs)
    q_ref = next(it)
    k_refs = [next(it) for _ in range(n_chunk)]
    v_ref = next(it)
    gq_ref = next(it)
    gk_ref = next(it)
    cos_ref = next(it) if has_rope else None
    sin_ref = next(it) if has_rope else None
    t_ref = next(it) if tmode else None
    o_ref = next(it)
    k_scr = next(it)
    v_scr = next(it)
    qb = pl.program_id(2)

    @pl.when(qb == 0)
    def _():
        step = min(seq, 512)
        gains = [gk_ref[:, c * LANE:(c + 1) * LANE] for c in range(n_chunk)]
        for r0 in range(0, seq, step):
            chunks = [kr[r0:r0 + step, :] for kr in k_refs]
            cos = cos_ref[r0:r0 + step, :] if has_rope else None
            sin = sin_ref[r0:r0 + step, :] if has_rope else None
            outs = _norm_rope(chunks, gains, cos, sin, rope_flags, swap_kind, dim)
            for c, o in enumerate(outs):
                k_scr[r0:r0 + step, c * LANE:(c + 1) * LANE] = o
            v_scr[r0:r0 + step, :LANE] = v_ref[r0:r0 + step, :].astype(BF16)
            v_scr[r0:r0 + step, LANE:] = jnp.ones((step, LANE), BF16)

    if window == "full":
        kwin = k_scr[...]
        vwin = v_scr[...]
    else:
        rows = seq // GRID_W
        start_row = jnp.clip(qb * C_QROWS - C_WIN_R // 2, 0, rows - kw // GRID_W)
        start = pl.multiple_of(start_row * GRID_W, GRID_W)
        kwin = k_scr[pl.ds(start, kw), :]
        vwin = v_scr[pl.ds(start, kw), :]

    row0 = pl.multiple_of(qb * bq, bq)
    gains = [gq_ref[:, c * LANE:(c + 1) * LANE] for c in range(n_chunk)]
    for r0 in range(0, bq, rc):
        cos = cos_ref[pl.ds(row0 + r0, rc), :] if has_rope else None
        sin = sin_ref[pl.ds(row0 + r0, rc), :] if has_rope else None
        for g in range(n_grp):
            base = g * n_chunk
            chunks = [q_ref[r0:r0 + rc, (base + c) * LANE:(base + c + 1) * LANE] for c in range(n_chunk)]
            outs = _norm_rope(chunks, gains, cos, sin, rope_flags, swap_kind, dim)
            q = outs[0] if n_chunk == 1 else jnp.concatenate(outs, axis=1)
            s = lax.dot_general(q, kwin, (((1,), (1,)), ((), ())), preferred_element_type=F32)
            if tmode == "add":
                t = t_ref[0, 0, r0:r0 + rc, :]
                s = jnp.where(t > -1e29, s * scale + t, NEG_INF)
                p = jnp.exp(s - jnp.max(s, axis=-1, keepdims=True))
            else:
                if tmode == "mul":
                    t = t_ref[r0:r0 + rc, :]
                    s = jnp.where(t > 0.0, s, NEG_INF)
                p = jnp.exp2((s - jnp.max(s, axis=-1, keepdims=True)) * (scale * LOG2_E))
                if tmode == "mul":
                    p = p * t
            o = jnp.dot(p.astype(BF16), vwin, preferred_element_type=F32)
            o_ref[r0:r0 + rc, g * LANE:(g + 1) * LANE] = o[:, :LANE] / o[:, LANE:]


def _attention(q_arr, q_col0, k_arrs, v_arr, v_col0, gq, gk, cos, sin, table, *, batch, seq, n_kv, n_grp,
               rope_flags, swap_kind, dim, bq, rc, window, tmode, name):
    n_chunk = len(k_arrs)
    nqb = seq // bq
    kw = seq if window == "full" else min(C_KROWS, seq // GRID_W) * GRID_W
    qw = n_grp * n_chunk * LANE
    q_blk0 = q_col0 * LANE // qw
    assert q_blk0 * qw == q_col0 * LANE
    in_specs = [pl.BlockSpec((bq, qw), lambda b, g, i: (b * nqb + i, q_blk0 + g))]
    args = [q_arr]
    for arr, col0, per_head in k_arrs:
        in_specs.append(pl.BlockSpec((seq, LANE), functools.partial(
            lambda b, g, i, col0, per_head: (b, col0 + g * per_head), col0=col0, per_head=per_head)))
        args.append(arr)
    in_specs.append(pl.BlockSpec((seq, LANE), lambda b, g, i: (b, v_col0 + g)))
    args.append(v_arr)
    in_specs.append(pl.BlockSpec((1, n_chunk * LANE), lambda b, g, i: (0, 0)))
    args.append(gq.reshape(1, n_chunk * LANE))
    in_specs.append(pl.BlockSpec((1, n_chunk * LANE), lambda b, g, i: (0, 0)))
    args.append(gk.reshape(1, n_chunk * LANE))
    if any(rope_flags):
        in_specs += [pl.BlockSpec((seq, LANE), lambda b, g, i: (0, 0))] * 2
        args += [cos, sin]
    if tmode == "add":
        in_specs.append(pl.BlockSpec((1, 1, bq, kw), lambda b, g, i: (g, i, 0, 0)))
        args.append(table)
    elif tmode == "mul":
        in_specs.append(pl.BlockSpec((bq, kw), lambda b, g, i: (i, 0)))
        args.append(table)
    kern = functools.partial(
        _attn_kernel, n_grp=n_grp, n_chunk=n_chunk, rope_flags=rope_flags, swap_kind=swap_kind, dim=dim,
        scale=dim ** -0.5, bq=bq, rc=min(rc, bq), seq=seq, kw=kw, window=window, tmode=tmode)
    return pl.pallas_call(
        kern,
        name=name,
        grid=(batch, n_kv, nqb),
        in_specs=in_specs,
        out_specs=pl.BlockSpec((bq, n_grp * LANE), lambda b, g, i: (b * nqb + i, g)),
        out_shape=jax.ShapeDtypeStruct((batch * seq, n_kv * n_grp * LANE), F32),
        scratch_shapes=[pltpu.VMEM((seq, n_chunk * LANE), BF16), pltpu.VMEM((seq, 2 * LANE), BF16)],
        compiler_params=_params(3),
    )(*args)


def _rope_cs(pos, half):
    inv_freq = ROPE_THETA ** (-jnp.arange(half, dtype=F32) / half)
    ang = pos.astype(F32)[:, None] * inv_freq[None, :]
    return jnp.cos(ang), jnp.sin(ang)


def _rope_tables(seq):
    t = jnp.arange(seq)
    c, s = _rope_cs(t, HEAD_DIM // 2)
    a_cos, a_sin = jnp.concatenate([c, c], -1), jnp.concatenate([-s, s], -1)
    cr, sr = _rope_cs(t // GRID_W, HEAD_DIM // 4)
    cc, sc = _rope_cs(t % GRID_W, HEAD_DIM // 4)
    b_cos = jnp.concatenate([cr, cr, cc, cc], -1)
    b_sin = jnp.concatenate([-sr, sr, -sc, sc], -1)
    cd, sd = _rope_cs(t, D_ROPE // 2)
    z = jnp.zeros((seq, LANE - D_ROPE), F32)
    d_cos = jnp.concatenate([cd, cd, z], -1)
    d_sin = jnp.concatenate([-sd, sd, z], -1)
    return (a_cos, a_sin), (b_cos, b_sin), (d_cos, d_sin)


def _dilation_multiplicity(seq):
    t = jnp.arange(seq)
    delta = t[None, :] - t[:, None]
    mult = jnp.zeros((seq, seq), jnp.int32)
    for window, d in A_PATTERNS:
        half = window // (2 * d)
        mult = mult + ((delta % d == 0) & (jnp.abs(delta) <= half * d)).astype(jnp.int32)
    return mult.astype(F32)


def _neighbourhood_table(rpb_l, seq):
    heads = rpb_l.shape[0]
    rows = seq // GRID_W
    wr = min(C_WIN_R, rows)
    kr = min(C_KROWS, rows)
    nqb = rows // C_QROWS
    qb = jnp.arange(nqb)
    r = qb[:, None] * C_QROWS + jnp.arange(C_QROWS)[None, :]
    win0 = jnp.clip(qb * C_QROWS - C_WIN_R // 2, 0, rows - kr)
    krow = win0[:, None] + jnp.arange(kr)[None, :]
    row_start = jnp.clip(r - wr // 2, 0, rows - wr)
    row_ok = (krow[:, None, :] >= row_start[:, :, None]) & (krow[:, None, :] < row_start[:, :, None] + wr)
    drow = jnp.clip(krow[:, None, :] - r[:, :, None] + (C_WIN_R - 1), 0, 2 * C_WIN_R - 2)
    c = jnp.arange(GRID_W)
    col_start = jnp.clip(c - C_WIN_C // 2, 0, GRID_W - C_WIN_C)
    col_ok = (c[None, :] >= col_start[:, None]) & (c[None, :] < col_start[:, None] + C_WIN_C)
    dcol = jnp.clip(c[None, :] - c[:, None] + (C_WIN_C - 1), 0, 2 * C_WIN_C - 2)
    by_col = jnp.take(rpb_l, dcol.reshape(-1), axis=2).reshape(heads, 2 * C_WIN_R - 1, GRID_W, GRID_W)
    by_row = jnp.take(by_col, drow.reshape(-1), axis=1)
    bias = by_row.reshape(heads, nqb, C_QROWS, kr, GRID_W, GRID_W).transpose(0, 1, 2, 4, 3, 5)
    ok = row_ok[None, :, :, None, :, None] & col_ok[None, None, None, :, None, :]
    table = jnp.where(ok, bias, NEG_INF)
    return table.reshape(heads, nqb, C_QROWS * GRID_W, kr * GRID_W)


def _mixer(x2, h, l, w_in, w_out, lw, tabs, batch, seq):
    d_model = x2.shape[1]
    gw = d_model // N_MIXERS
    heads = gw // HEAD_DIM
    kv_heads = heads // 4
    nb = gw // LANE
    q_rank = lw["g_cq"].shape[0]
    (a_cos, a_sin), (b_cos, b_sin), (d_cos, d_sin) = tabs["rope"]

    col = {"qa": 0, "ka": nb, "va": 2 * nb, "qb": 3 * nb, "kb": 4 * nb, "vb": 4 * nb + kv_heads}
    col["qc"] = 4 * nb + 2 * kv_heads
    col["kc"] = col["qc"] + nb
    col["vc"] = col["kc"] + nb
    col["cq"] = col["vc"] + nb
    col["ckv"] = col["cq"] + q_rank // LANE
    n_main = (col["ckv"] + D_KV_RANK // LANE) * LANE
    proj = _matmul_ws(h, w_in, l, n_main, name="in_proj")
    kpe = _matmul(h, lw["w_kpe"], bn=LANE, name="kpe_proj")

    g_abc = lw["g_qk_abc"]
    common = dict(batch=batch, seq=seq)

    oa = _attention(proj, col["qa"], [(proj, col["ka"], 1)], proj, col["va"], g_abc[0, 0], g_abc[0, 1],
                    a_cos, a_sin, tabs["mult"], n_kv=heads, n_grp=1, rope_flags=(True,), swap_kind="half64",
                    dim=HEAD_DIM, bq=min(512, seq), rc=ATTN_ROWS, window="full", tmode="mul", name="attn_a",
                    **common)
    ob = _attention(proj, col["qb"], [(proj, col["kb"], 1)], proj, col["vb"], g_abc[1, 0], g_abc[1, 1],
                    b_cos, b_sin, None, n_kv=kv_heads, n_grp=4, rope_flags=(True,), swap_kind="half32",
                    dim=HEAD_DIM, bq=min(128, seq), rc=ATTN_ROWS, window="full", tmode=None, name="attn_b",
                    **common)
    oc = _attention(proj, col["qc"], [(proj, col["kc"], 1)], proj, col["vc"], g_abc[2, 0], g_abc[2, 1],
                    None, None, lw["nbr_table"], n_kv=heads, n_grp=1, rope_flags=(False,), swap_kind=None,
                    dim=HEAD_DIM, bq=C_QROWS * GRID_W, rc=ATTN_ROWS, window="rows", tmode="add", name="attn_c",
                    **common)

    cq_n = _rmsnorm([_pieces(proj, col["cq"] * LANE, q_rank)], lw["g_cq"])
    ckv_n = _rmsnorm([_pieces(proj, col["ckv"] * LANE, D_KV_RANK)], lw["g_ckv"])
    qd = _matmul(cq_n, lw["w_uq"], name="uq_proj")
    kvd = _matmul(ckv_n, lw["w_ukv"], name="ukv_proj")
    od = _attention(qd, 0, [(kvd, 0, 1), (kpe, 0, 0)], kvd, heads, lw["gq_mla"], lw["gk_mla"],
                    d_cos, d_sin, None, n_kv=heads, n_grp=1, rope_flags=(False, True), swap_kind="half32",
                    dim=D_QK, bq=min(512, seq), rc=ATTN_ROWS, window="full", tmode=None, name="attn_d",
                    **common)

    normed = _rmsnorm([_pieces(o, 0, gw) for o in (oa, ob, oc, od)], lw["g_grp"])
    return _matmul_ws(normed, w_out, l, d_model, res=x2, name="out_proj")


def _layer_weights(l, seq, g_qk_abc, w_in, w_uq, w_ukv, g_cq, g_ckv, g_qk_mla, rpb, g_grp):
    d_model = w_in.shape[1]
    heads = d_model // N_MIXERS // HEAD_DIM
    q_rank = w_uq.shape[1]
    w_kpe = jnp.pad(w_in[l][:, w_in.shape[2] - D_ROPE:], ((0, 0), (0, LANE - D_ROPE))).astype(BF16)
    uq = w_uq[l].reshape(q_rank, heads, D_QK)
    uq = jnp.pad(uq, ((0, 0), (0, 0), (0, 2 * LANE - D_QK))).reshape(q_rank, heads * 2 * LANE).astype(BF16)
    ukv = w_ukv[l].reshape(D_KV_RANK, heads, D_NOPE + HEAD_DIM)
    ukv = jnp.concatenate([ukv[:, :, :D_NOPE].reshape(D_KV_RANK, heads * D_NOPE),
                           ukv[:, :, D_NOPE:].reshape(D_KV_RANK, heads * HEAD_DIM)], axis=1).astype(BF16)
    pad_g = lambda g: jnp.pad(g, (0, 2 * LANE - D_QK))
    return dict(w_kpe=w_kpe, w_uq=uq, w_ukv=ukv, g_qk_abc=g_qk_abc[l], g_cq=g_cq[l], g_ckv=g_ckv[l],
                gq_mla=pad_g(g_qk_mla[l, 0]), gk_mla=pad_g(g_qk_mla[l, 1]),
                nbr_table=_neighbourhood_table(rpb[l], seq), g_grp=g_grp[l])


def kernel(x, g_mix, w_in, g_qk_abc, w_uq, w_ukv, g_cq, g_ckv, g_qk_mla, rpb, g_grp, w_out, g_ffn, w_gate,
           w_up, w_down, w_router, we_gate, we_up, we_down):
    batch, seq, d_model = x.shape
    depth = g_mix.shape[0]
    n_exp = we_gate.shape[1]
    tabs = dict(rope=_rope_tables(seq), mult=_dilation_multiplicity(seq))
    x2 = x.reshape(batch * seq, d_model)
    for l in range(depth):
        lw = _layer_weights(l, seq, g_qk_abc, w_in, w_uq, w_ukv, g_cq, g_ckv, g_qk_mla, rpb, g_grp)
        h = _rmsnorm([_pieces(x2, 0, d_model)], g_mix[l])
        x2 = _mixer(x2, h, l, w_in, w_out, lw, tabs, batch, seq)
        i = l // 2
        if l % 2 == 0:
            h = _rmsnorm([_pieces(x2, 0, d_model)], g_ffn[l])
            a = _gateup_dense(h, w_gate, w_up, i)
            x2 = _matmul(a, w_down[i].astype(BF16), res=x2, bk=w_down.shape[1] // 2, name="ffn_down")
        else:
            h3 = _rmsnorm([_pieces(x2, 0, d_model)], g_ffn[l], split_rows=True)
            wr = jnp.pad(w_router[i], ((0, 0), (0, LANE - n_exp))).astype(BF16)
            x2 = _moe(x2, h3, wr, we_gate, we_up, we_down, i)
    return x2.reshape(batch, seq, d_model)
```

```python
import functools
import math

import jax
import jax.numpy as jnp
from jax import lax
from jax.experimental import pallas as pl
from jax.experimental.pallas import tpu as pltpu

F32 = jnp.float32
BF16 = jnp.bfloat16

LANE = 128
HEAD_DIM = 128
N_MIXERS = 4
ROPE_THETA = 10000.0
EPS = 1e-6
NEG_INF = -1e30
LOG2_E = 1.4426950408889634
GRID_W = 64
A_PATTERNS = ((128, 1), (512, 4), (2048, 16))
C_WIN_R = 8
C_WIN_C = 16
C_QROWS = 4
C_KROWS = 12
ATTN_ROWS = 128
NBR_D_LO = (C_WIN_R - 1) - (C_KROWS - C_QROWS) - (C_QROWS - 1)
NBR_D_HI = (C_WIN_R - 1) + (C_KROWS - 2)
D_KV_RANK = 512
D_NOPE = 128
D_ROPE = 64
D_QK = D_NOPE + D_ROPE
TOP_K = 2
CAST_ROWS = 512
VMEM_LIMIT_BYTES = 56 * 1024 * 1024


def _blk(dim, target, mult=LANE):
    best = None
    d = mult
    while d <= min(dim, target):
        if dim % d == 0:
            best = d
        d += mult
    return best if best is not None else dim


def _params(n_grid):
    return pltpu.CompilerParams(dimension_semantics=("arbitrary",) * n_grid,
                                vmem_limit_bytes=VMEM_LIMIT_BYTES)


def _stage_bf16(w_ref, scr):
    rows = scr.shape[0]
    step = min(rows, CAST_ROWS)
    for r0 in range(0, rows, step):
        scr[r0:r0 + step, :] = w_ref[0, r0:r0 + step, :].astype(BF16)


def _rmsnorm_kernel(*refs, group_sizes, split_rows):
    n_in = sum(group_sizes)
    x_refs = refs[:n_in]
    g_ref = refs[n_in]
    o_ref = refs[n_in + 1]
    off = 0
    k = 0
    for size in group_sizes:
        pieces = [x_refs[k + p][...] for p in range(size)]
        k += size
        width = sum(p.shape[-1] for p in pieces)
        ssq = None
        for p in pieces:
            t = jnp.sum(p * p, axis=-1, keepdims=True)
            ssq = t if ssq is None else ssq + t
        r = lax.rsqrt(ssq / width + EPS)
        for p in pieces:
            w = p.shape[-1]
            y = (p * r) * g_ref[:, off:off + w]
            if split_rows:
                o_ref[...] = y.reshape(o_ref.shape)
            else:
                o_ref[:, off:off + w] = y.astype(o_ref.dtype)
            off += w


def _pieces(arr, col_off, width):
    pw = math.gcd(col_off, width) if col_off else width
    return [(arr, col_off // pw + p, pw) for p in range(width // pw)]


def _rmsnorm(groups, gain, bm=256, out_dtype=BF16, split_rows=False):
    flat = [p for g in groups for p in g]
    n = flat[0][0].shape[0]
    bm = _blk(n, bm, 8)
    total = sum(w for _, _, w in flat)
    in_specs = [pl.BlockSpec((bm, w), functools.partial(lambda i, c: (i, c), c=c)) for _, c, w in flat]
    in_specs.append(pl.BlockSpec((1, total), lambda i: (0, 0)))
    if split_rows:
        assert len(flat) == 1
        out_spec = pl.BlockSpec((bm, total // LANE, LANE), lambda i: (i, 0, 0))
        out_shape = jax.ShapeDtypeStruct((n, total // LANE, LANE), F32)
    else:
        out_spec = pl.BlockSpec((bm, total), lambda i: (i, 0))
        out_shape = jax.ShapeDtypeStruct((n, total), out_dtype)
    return pl.pallas_call(
        functools.partial(_rmsnorm_kernel, group_sizes=tuple(len(g) for g in groups), split_rows=split_rows),
        grid=(n // bm,),
        in_specs=in_specs,
        out_specs=out_spec,
        out_shape=out_shape,
        compiler_params=_params(1),
        name="rmsnorm",
    )(*[a for a, _, _ in flat], gain.reshape(1, total))


def _mm_kernel(a_ref, w_ref, *rest, nk, has_res):
    if has_res:
        r_ref, o_ref = rest
    else:
        (o_ref,) = rest
    part = jnp.dot(a_ref[...], w_ref[...], preferred_element_type=F32)
    if nk == 1:
        o_ref[...] = (r_ref[...] + part) if has_res else part
        return
    k = pl.program_id(2)

    @pl.when(k == 0)
    def _():
        o_ref[...] = (r_ref[...] + part) if has_res else part

    @pl.when(k > 0)
    def _():
        o_ref[...] += part


def _matmul(a, w, res=None, bm=1024, bn=512, bk=None, name="matmul"):
    m, kdim = a.shape
    n = w.shape[1]
    bm = _blk(m, bm, 8)
    bn = _blk(n, bn)
    bk = kdim if bk is None else _blk(kdim, bk)
    nk = kdim // bk
    in_specs = [pl.BlockSpec((bm, bk), lambda i, j, k: (i, k)),
                pl.BlockSpec((bk, bn), lambda i, j, k: (k, j))]
    args = [a, w]
    if res is not None:
        in_specs.append(pl.BlockSpec((bm, bn), lambda i, j, k: (i, j)))
        args.append(res)
    return pl.pallas_call(
        functools.partial(_mm_kernel, nk=nk, has_res=res is not None),
        grid=(m // bm, n // bn, nk),
        in_specs=in_specs,
        out_specs=pl.BlockSpec((bm, bn), lambda i, j, k: (i, j)),
        out_shape=jax.ShapeDtypeStruct((m, n), F32),
        compiler_params=_params(3),
        name=name,
    )(*args)


def _mm_ws_kernel(a_ref, w_ref, *rest, has_res, valid_cols):
    if has_res:
        r_ref, o_ref, w_scr = rest
    else:
        o_ref, w_scr = rest

    @pl.when(pl.program_id(1) == 0)
    def _():
        _stage_bf16(w_ref, w_scr)

    part = jnp.dot(a_ref[...], w_scr[...], preferred_element_type=F32)
    if valid_cols < part.shape[1]:
        lane = lax.broadcasted_iota(jnp.int32, part.shape, 1)
        part = jnp.where(lane < valid_cols, part, 0.0)
    o_ref[...] = (r_ref[...] + part) if has_res else part


def _matmul_ws(a, w3, layer, n_cols, res=None, bm=1024, bn=512, col0=0, name="matmul_ws"):
    m, kdim = a.shape
    bm = _blk(m, bm, 8)
    bn = _blk(n_cols, bn)
    blk0 = col0 // bn
    assert blk0 * bn == col0
    valid_cols = min(n_cols, w3.shape[2] - col0)
    assert valid_cols == n_cols or n_cols == bn
    in_specs = [pl.BlockSpec((bm, kdim), lambda j, i: (i, 0)),
                pl.BlockSpec((1, kdim, bn), lambda j, i: (layer, 0, blk0 + j))]
    args = [a, w3]
    if res is not None:
        in_specs.append(pl.BlockSpec((bm, bn), lambda j, i: (i, j)))
        args.append(res)
    return pl.pallas_call(
        functools.partial(_mm_ws_kernel, has_res=res is not None, valid_cols=valid_cols),
        grid=(n_cols // bn, m // bm),
        in_specs=in_specs,
        out_specs=pl.BlockSpec((bm, bn), lambda j, i: (i, j)),
        out_shape=jax.ShapeDtypeStruct((m, n_cols), F32),
        scratch_shapes=[pltpu.VMEM((kdim, bn), BF16)],
        compiler_params=_params(2),
        name=name,
    )(*args)


def _swiglu(g, u):
    return (g / (1.0 + jnp.exp(-g))) * u


def _gateup_kernel(h_ref, wg_ref, wu_ref, o_ref, wg_scr, wu_scr):
    @pl.when(pl.program_id(1) == 0)
    def _():
        _stage_bf16(wg_ref, wg_scr)
        _stage_bf16(wu_ref, wu_scr)

    h = h_ref[...]
    g = jnp.dot(h, wg_scr[...], preferred_element_type=F32)
    u = jnp.dot(h, wu_scr[...], preferred_element_type=F32)
    o_ref[...] = _swiglu(g, u).astype(o_ref.dtype)


def _gateup_dense(h, wg3, wu3, layer, bm=1024, bn=256):
    m, d = h.shape
    f = wg3.shape[2]
    bm = _blk(m, bm, 8)
    bn = _blk(f, bn)
    w_spec = pl.BlockSpec((1, d, bn), lambda j, i: (layer, 0, j))
    return pl.pallas_call(
        _gateup_kernel,
        grid=(f // bn, m // bm),
        in_specs=[pl.BlockSpec((bm, d), lambda j, i: (i, 0)), w_spec, w_spec],
        out_specs=pl.BlockSpec((bm, bn), lambda j, i: (i, j)),
        out_shape=jax.ShapeDtypeStruct((m, f), BF16),
        scratch_shapes=[pltpu.VMEM((d, bn), BF16), pltpu.VMEM((d, bn), BF16)],
        compiler_params=_params(2),
        name="ffn_gateup",
    )(h, wg3, wu3)


ROUTE_IDX1, ROUTE_IDX2, ROUTE_G1, ROUTE_G2, ROUTE_RANK1, ROUTE_RANK2 = range(6)


def _router_kernel(h_ref, w_ref, route_ref, cnt_ref, carry, *, n_exp):
    @pl.when(pl.program_id(0) == 0)
    def _():
        carry[...] = jnp.zeros_like(carry)

    bm = h_ref.shape[0]
    h = h_ref[...].reshape(bm, h_ref.shape[1] * LANE).astype(BF16)
    logits = jnp.dot(h, w_ref[...], preferred_element_type=F32)
    lane = lax.broadcasted_iota(jnp.int32, logits.shape, 1).astype(F32)
    logits = jnp.where(lane < n_exp, logits, -jnp.inf)
    v1 = jnp.max(logits, axis=-1, keepdims=True)
    i1 = jnp.min(jnp.where(logits == v1, lane, float(LANE)), axis=-1, keepdims=True)
    rest = jnp.where(lane == i1, -jnp.inf, logits)
    v2 = jnp.max(rest, axis=-1, keepdims=True)
    i2 = jnp.min(jnp.where(rest == v2, lane, float(LANE)), axis=-1, keepdims=True)
    e2 = jnp.exp(v2 - v1)
    den = 1.0 + e2
    g1 = 1.0 / den
    g2 = e2 / den
    chosen = jnp.where((lane == i1) | (lane == i2), 1.0, 0.0)
    row = lax.broadcasted_iota(jnp.int32, (bm, bm), 0)
    col = lax.broadcasted_iota(jnp.int32, (bm, bm), 1)
    earlier = jnp.where(col < row, 1.0, 0.0).astype(BF16)
    before = jnp.dot(earlier, chosen.astype(BF16), preferred_element_type=F32) + carry[...]
    rank1 = jnp.sum(jnp.where(lane == i1, before, 0.0), axis=-1, keepdims=True)
    rank2 = jnp.sum(jnp.where(lane == i2, before, 0.0), axis=-1, keepdims=True)
    carry[...] += jnp.sum(chosen, axis=0, keepdims=True)
    route = jnp.zeros_like(logits)
    for pos, val in ((ROUTE_IDX1, i1), (ROUTE_IDX2, i2), (ROUTE_G1, g1), (ROUTE_G2, g2),
                     (ROUTE_RANK1, rank1), (ROUTE_RANK2, rank2)):
        route = jnp.where(lane == pos, val, route)
    route_ref[...] = route
    cnt_ref[...] = jnp.broadcast_to(carry[...], cnt_ref.shape)


def _router(h3, w_router_padded, n_exp, bm=512):
    m, dl, _ = h3.shape
    bm = _blk(m, bm, 8)
    return pl.pallas_call(
        functools.partial(_router_kernel, n_exp=n_exp),
        grid=(m // bm,),
        in_specs=[pl.BlockSpec((bm, dl, LANE), lambda i: (i, 0, 0)),
                  pl.BlockSpec((dl * LANE, LANE), lambda i: (0, 0))],
        out_specs=[pl.BlockSpec((bm, LANE), lambda i: (i, 0)),
                   pl.BlockSpec((8, LANE), lambda i: (0, 0))],
        out_shape=[jax.ShapeDtypeStruct((m, LANE), F32), jax.ShapeDtypeStruct((8, LANE), F32)],
        scratch_shapes=[pltpu.VMEM((1, LANE), F32)],
        compiler_params=_params(1),
        name="router",
    )(h3, w_router_padded)


def _gather_kernel(s1_ref, s2_ref, te_ref, offs_ref, cnt_ref, nu_ref, src_ref, o_ref, tok_scr, buf, sem, *,
                   tb, bm, n_tiles, n_tok):
    i = pl.program_id(0)

    @pl.when(i == 0)
    def _():
        def place(t, c):
            tok_scr[s1_ref[t]] = t
            tok_scr[s2_ref[t]] = t
            return c

        lax.fori_loop(0, n_tok, place, 0, unroll=8)

    def used(tile):
        return tile * tb < nu_ref[0] * bm

    def row_copy(r, slot, t):
        return pltpu.make_async_copy(src_ref.at[t], buf.at[slot, r], sem.at[slot])

    def issue(tile, slot):
        e = te_ref[(tile * tb) // bm]
        first = offs_ref[e]
        last = first + cnt_ref[e] - 1

        def body(r, c):
            s = tile * tb + r
            s = jnp.where(s <= last, s, first)
            row_copy(r, slot, tok_scr[s]).start()
            return c

        lax.fori_loop(0, tb, body, 0, unroll=8)

    @pl.when((i == 0) & used(0))
    def _():
        issue(0, 0)

    @pl.when((i + 1 < n_tiles) & used(i + 1))
    def _():
        issue(i + 1, (i + 1) % 2)

    slot = i % 2

    @pl.when(used(i))
    def _():
        def wait(r, c):
            row_copy(r, slot, 0).wait()
            return c

        lax.fori_loop(0, tb, wait, 0, unroll=8)
        o_ref[...] = buf[slot].reshape(o_ref.shape).astype(o_ref.dtype)

    @pl.when(jnp.logical_not(used(i)))
    def _():
        o_ref[...] = jnp.zeros_like(o_ref)


def _gather_rows(slot1, slot2, tile_expert, offs, counts, n_used, src3, n_slot, bm, tb=256):
    n_tok, dl, _ = src3.shape
    tb = min(tb, bm)
    n_tiles = n_slot // tb
    return pl.pallas_call(
        functools.partial(_gather_kernel, tb=tb, bm=bm, n_tiles=n_tiles, n_tok=n_tok),
        grid_spec=pltpu.PrefetchScalarGridSpec(
            num_scalar_prefetch=6,
            grid=(n_tiles,),
            in_specs=[pl.BlockSpec(memory_space=pl.ANY)],
            out_specs=pl.BlockSpec((tb, dl * LANE), lambda i, *_: (i, 0)),
            scratch_shapes=[pltpu.SMEM((n_slot,), jnp.int32), pltpu.VMEM((2, tb, dl, LANE), F32),
                            pltpu.SemaphoreType.DMA((2,))]),
        out_shape=jax.ShapeDtypeStruct((n_slot, dl * LANE), BF16),
        compiler_params=_params(1),
        name="gather_rows",
    )(slot1, slot2, tile_expert, offs, counts, n_used, src3)


def _new_expert(te_ref, i):
    return (i == 0) | (te_ref[i] != te_ref[jnp.maximum(i - 1, 0)])


def _expert_gateup_kernel(te_ref, nu_ref, x_ref, wg_ref, wu_ref, o_ref, wg_scr, wu_scr):
    i = pl.program_id(1)

    @pl.when(_new_expert(te_ref, i))
    def _():
        _stage_bf16(wg_ref, wg_scr)
        _stage_bf16(wu_ref, wu_scr)

    @pl.when(i < nu_ref[0])
    def _():
        x = x_ref[...]
        g = jnp.dot(x, wg_scr[...], preferred_element_type=F32)
        u = jnp.dot(x, wu_scr[...], preferred_element_type=F32)
        o_ref[...] = _swiglu(g, u).astype(o_ref.dtype)

    @pl.when(i >= nu_ref[0])
    def _():
        o_ref[...] = jnp.zeros_like(o_ref)


def _expert_gateup(tile_expert, n_used, xs, wg4, wu4, layer, bm, bn=512):
    p, d = xs.shape
    ef = wg4.shape[3]
    bn = _blk(ef, bn)
    w_spec = pl.BlockSpec((None, 1, d, bn), lambda j, i, te, nu: (layer, te[i], 0, j))
    return pl.pallas_call(
        _expert_gateup_kernel,
        grid_spec=pltpu.PrefetchScalarGridSpec(
            num_scalar_prefetch=2,
            grid=(ef // bn, p // bm),
            in_specs=[pl.BlockSpec((bm, d), lambda j, i, te, nu: (i, 0)), w_spec, w_spec],
            out_specs=pl.BlockSpec((bm, bn), lambda j, i, te, nu: (i, j)),
            scratch_shapes=[pltpu.VMEM((d, bn), BF16), pltpu.VMEM((d, bn), BF16)]),
        out_shape=jax.ShapeDtypeStruct((p, ef), BF16),
        compiler_params=_params(2),
        name="expert_gateup",
    )(tile_expert, n_used, xs, wg4, wu4)


def _expert_down_kernel(te_ref, nu_ref, a_ref, w_ref, o_ref, w_scr):
    i = pl.program_id(1)

    @pl.when(_new_expert(te_ref, i))
    def _():
        _stage_bf16(w_ref, w_scr)

    @pl.when(i < nu_ref[0])
    def _():
        y = jnp.dot(a_ref[...], w_scr[...], preferred_element_type=F32)
        o_ref[...] = y.reshape(o_ref.shape)

    @pl.when(i >= nu_ref[0])
    def _():
        o_ref[...] = jnp.zeros_like(o_ref)


def _expert_down(tile_expert, n_used, a, wd4, layer, bm, bn=1024):
    p, ef = a.shape
    d = wd4.shape[3]
    bn = _blk(d, bn, 8 * LANE)
    return pl.pallas_call(
        _expert_down_kernel,
        grid_spec=pltpu.PrefetchScalarGridSpec(
            num_scalar_prefetch=2,
            grid=(d // bn, p // bm),
            in_specs=[pl.BlockSpec((bm, ef), lambda j, i, te, nu: (i, 0)),
                      pl.BlockSpec((None, 1, ef, bn), lambda j, i, te, nu: (layer, te[i], 0, j))],
            out_specs=pl.BlockSpec((bm, bn // LANE, LANE), lambda j, i, te, nu: (i, j, 0)),
            scratch_shapes=[pltpu.VMEM((ef, bn), BF16)]),
        out_shape=jax.ShapeDtypeStruct((p, d // LANE, LANE), F32),
        compiler_params=_params(2),
        name="expert_down",
    )(tile_expert, n_used, a, wd4)


def _combine_kernel(s1_ref, s2_ref, y_ref, x_ref, route_ref, o_ref, buf, sem, *, tb, n_tiles):
    i = pl.program_id(0)

    def row_copy(k, r, slot, s):
        return pltpu.make_async_copy(y_ref.at[s], buf.at[slot, k, r], sem.at[slot])

    def issue(tile, slot):
        def body(r, c):
            t = tile * tb + r
            row_copy(0, r, slot, s1_ref[t]).start()
            row_copy(1, r, slot, s2_ref[t]).start()
            return c

        lax.fori_loop(0, tb, body, 0, unroll=4)

    @pl.when(i == 0)
    def _():
        issue(0, 0)

    @pl.when(i + 1 < n_tiles)
    def _():
        issue(i + 1, (i + 1) % 2)

    slot = i % 2

    def wait(r, c):
        row_copy(0, r, slot, 0).wait()
        row_copy(1, r, slot, 0).wait()
        return c

    lax.fori_loop(0, tb, wait, 0, unroll=4)
    route = route_ref[...]
    g1 = route[:, ROUTE_G1:ROUTE_G1 + 1]
    g2 = route[:, ROUTE_G2:ROUTE_G2 + 1]
    y1 = buf[slot, 0].reshape(o_ref.shape)
    y2 = buf[slot, 1].reshape(o_ref.shape)
    o_ref[...] = x_ref[...] + (g1 * y1 + g2 * y2)


def _combine(slot1, slot2, y3, x2, route, tb=128):
    n, d = x2.shape
    dl = d // LANE
    tb = _blk(n, tb, 8)
    n_tiles = n // tb
    return pl.pallas_call(
        functools.partial(_combine_kernel, tb=tb, n_tiles=n_tiles),
        grid_spec=pltpu.PrefetchScalarGridSpec(
            num_scalar_prefetch=2,
            grid=(n_tiles,),
            in_specs=[pl.BlockSpec(memory_space=pl.ANY),
                      pl.BlockSpec((tb, d), lambda i, s1, s2: (i, 0)),
                      pl.BlockSpec((tb, LANE), lambda i, s1, s2: (i, 0))],
            out_specs=pl.BlockSpec((tb, d), lambda i, s1, s2: (i, 0)),
            scratch_shapes=[pltpu.VMEM((2, 2, tb, dl, LANE), F32), pltpu.SemaphoreType.DMA((2,))]),
        out_shape=jax.ShapeDtypeStruct((n, d), F32),
        compiler_params=_params(1),
        name="expert_combine",
    )(slot1, slot2, y3, x2, route)


def _moe(x2, h3, w_router_padded, wg4, wu4, wd4, layer, bm=512):
    n = x2.shape[0]
    n_exp = wg4.shape[1]
    bm = _blk(n, bm, 8)
    route, cnt = _router(h3, w_router_padded, n_exp)
    counts = cnt[0, :n_exp].astype(jnp.int32)
    padded = ((counts + bm - 1) // bm) * bm
    ends = jnp.cumsum(padded)
    offs = ends - padded
    idx1 = route[:, ROUTE_IDX1].astype(jnp.int32)
    idx2 = route[:, ROUTE_IDX2].astype(jnp.int32)
    slot1 = offs[idx1] + route[:, ROUTE_RANK1].astype(jnp.int32)
    slot2 = offs[idx2] + route[:, ROUTE_RANK2].astype(jnp.int32)
    n_slot = n * TOP_K + n_exp * bm
    n_tiles = n_slot // bm
    tile_start = jnp.arange(n_tiles, dtype=jnp.int32) * bm
    tile_expert = jnp.minimum(jnp.sum((ends[None, :] <= tile_start[:, None]).astype(jnp.int32), axis=1), n_exp - 1)
    n_used = (ends[-1] // bm).reshape(1)
    xs = _gather_rows(slot1, slot2, tile_expert, offs, counts, n_used, h3, n_slot, bm)
    a = _expert_gateup(tile_expert, n_used, xs, wg4, wu4, layer, bm)
    y3 = _expert_down(tile_expert, n_used, a, wd4, layer, bm)
    return _combine(slot1, slot2, y3, x2, route)


def _swap(y, kind):
    if kind == "half64":
        return pltpu.roll(y, 64, 1)
    lane = lax.broadcasted_iota(jnp.int32, y.shape, 1)
    lo = pltpu.roll(y, 96, 1)
    hi = pltpu.roll(y, 32, 1)
    return jnp.where((lane & 63) < 32, lo, hi)


def _norm_rope(chunks, gains, cos, sin, rope_flags, swap_kind, dim, transpose=False):
    ssq = None
    for c in chunks:
        t = jnp.sum(c * c, axis=-1, keepdims=True)
        ssq = t if ssq is None else ssq + t
    r = lax.rsqrt(ssq / dim + EPS)
    outs = []
    for c, g, flag in zip(chunks, gains, rope_flags):
        y = (c * r) * g
        if flag:
            y = y * cos + _swap(y, swap_kind) * sin
        outs.append((y.T if transpose else y).astype(BF16))
    return outs


def _nbr_bias(t_ref, rm_ref, win_minus_q, r0, rc, kw):
    rows = []
    for jj in range(rc // GRID_W):
        j = r0 // GRID_W + jj
        slabs = []
        for c in range(kw // LANE):
            d = win_minus_q + (C_WIN_R - 1) + 2 * c - j
            slabs.append(t_ref[0, d - NBR_D_LO])
        rows.append(jnp.concatenate(slabs, axis=1) + rm_ref[0, j:j + 1, :])
    return rows[0] if len(rows) == 1 else jnp.concatenate(rows, axis=0)


def _attn_kernel(*refs, n_grp, n_chunk, rope_flags, swap_kind, dim, scale, bq, rc, seq, kw, window, tmode,
                 k_major):
    has_rope = any(rope_flags)
    it = iter(refs)
    q_ref = next(it)
    k_refs = [next(it) for _ in range(n_chunk)]
    v_ref = next(it)
    gq_ref = next(it)
    gk_ref = next(it)
    cos_ref = next(it) if has_rope else None
    sin_ref = next(it) if has_rope else None
    t_ref = next(it) if tmode else None
    rm_ref = next(it) if tmode == "nbr" else None
    o_ref = next(it)
    k_scr = next(it)
    v_scr = next(it)
    qb = pl.program_id(2)

    @pl.when(qb == 0)
    def _():
        step = min(seq, 512)
        gains = [gk_ref[:, c * LANE:(c + 1) * LANE] for c in range(n_chunk)]
        for r0 in range(0, seq, step):
            chunks = [kr[r0:r0 + step, :] for kr in k_refs]
            cos = cos_ref[r0:r0 + step, :] if has_rope else None
            sin = sin_ref[r0:r0 + step, :] if has_rope else None
            outs = _norm_rope(chunks, gains, cos, sin, rope_flags, swap_kind, dim, transpose=k_major)
            for c, o in enumerate(outs):
                if k_major:
                    k_scr[c * LANE:(c + 1) * LANE, r0:r0 + step] = o
                else:
                    k_scr[r0:r0 + step, c * LANE:(c + 1) * LANE] = o
            v_scr[r0:r0 + step, :LANE] = v_ref[r0:r0 + step, :].astype(BF16)
            v_scr[r0:r0 + step, LANE:] = jnp.ones((step, LANE), BF16)

    if window == "full":
        kwin = k_scr[...]
        vwin = v_scr[...]
    else:
        rows = seq // GRID_W
        start_row = jnp.clip(qb * C_QROWS - C_WIN_R // 2, 0, rows - kw // GRID_W)
        start = pl.multiple_of(start_row * GRID_W, GRID_W)
        kwin = k_scr[pl.ds(start, kw), :]
        vwin = v_scr[pl.ds(start, kw), :]

    row0 = pl.multiple_of(qb * bq, bq)
    gains = [gq_ref[:, c * LANE:(c + 1) * LANE] for c in range(n_chunk)]
    for r0 in range(0, bq, rc):
        cos = cos_ref[pl.ds(row0 + r0, rc), :] if has_rope else None
        sin = sin_ref[pl.ds(row0 + r0, rc), :] if has_rope else None
        for g in range(n_grp):
            base = g * n_chunk
            chunks = [q_ref[r0:r0 + rc, (base + c) * LANE:(base + c + 1) * LANE] for c in range(n_chunk)]
            outs = _norm_rope(chunks, gains, cos, sin, rope_flags, swap_kind, dim)
            q = outs[0] if n_chunk == 1 else jnp.concatenate(outs, axis=1)
            if k_major:
                s = jnp.dot(q, kwin, preferred_element_type=F32)
            else:
                s = lax.dot_general(q, kwin, (((1,), (1,)), ((), ())), preferred_element_type=F32)
            if tmode == "nbr":
                t = _nbr_bias(t_ref, rm_ref, start_row - qb * C_QROWS, r0, rc, kw)
                s = jnp.where(t > -1e29, s * scale + t, NEG_INF)
                p = jnp.exp(s - jnp.max(s, axis=-1, keepdims=True))
            else:
                if tmode == "mul":
                    t = t_ref[r0:r0 + rc, :]
                    s = jnp.where(t > 0.0, s, NEG_INF)
                p = jnp.exp2((s - jnp.max(s, axis=-1, keepdims=True)) * (scale * LOG2_E))
                if tmode == "mul":
                    p = p * t
            o = jnp.dot(p.astype(BF16), vwin, preferred_element_type=F32)
            o_ref[r0:r0 + rc, g * LANE:(g + 1) * LANE] = o[:, :LANE] / o[:, LANE:]


def _attention(q_arr, q_col0, k_arrs, v_arr, v_col0, gq, gk, cos, sin, table, *, batch, seq, n_kv, n_grp,
               rope_flags, swap_kind, dim, bq, rc, window, tmode, name):
    n_chunk = len(k_arrs)
    k_major = n_chunk > 1 and window == "full"
    nqb = seq // bq
    kw = seq if window == "full" else min(C_KROWS, seq // GRID_W) * GRID_W
    qw = n_grp * n_chunk * LANE
    q_blk0 = q_col0 * LANE // qw
    assert q_blk0 * qw == q_col0 * LANE
    in_specs = [pl.BlockSpec((bq, qw), lambda b, g, i: (b * nqb + i, q_blk0 + g))]
    args = [q_arr]
    for arr, col0, per_head in k_arrs:
        in_specs.append(pl.BlockSpec((seq, LANE), functools.partial(
            lambda b, g, i, col0, per_head: (b, col0 + g * per_head), col0=col0, per_head=per_head)))
        args.append(arr)
    in_specs.append(pl.BlockSpec((seq, LANE), lambda b, g, i: (b, v_col0 + g)))
    args.append(v_arr)
    in_specs.append(pl.BlockSpec((1, n_chunk * LANE), lambda b, g, i: (0, 0)))
    args.append(gq.reshape(1, n_chunk * LANE))
    in_specs.append(pl.BlockSpec((1, n_chunk * LANE), lambda b, g, i: (0, 0)))
    args.append(gk.reshape(1, n_chunk * LANE))
    if any(rope_flags):
        in_specs += [pl.BlockSpec((seq, LANE), lambda b, g, i: (0, 0))] * 2
        args += [cos, sin]
    if tmode == "nbr":
        slabs, layer, row_mask = table
        assert seq // GRID_W >= C_KROWS and 2 * GRID_W == LANE
        in_specs.append(pl.BlockSpec((None, 1) + slabs.shape[2:], lambda b, g, i: (layer, g, 0, 0, 0)))
        in_specs.append(pl.BlockSpec((1,) + row_mask.shape[1:], lambda b, g, i: (i, 0, 0)))
        args += [slabs, row_mask]
    elif tmode == "mul":
        in_specs.append(pl.BlockSpec((bq, kw), lambda b, g, i: (i, 0)))
        args.append(table)
    kern = functools.partial(
        _attn_kernel, n_grp=n_grp, n_chunk=n_chunk, rope_flags=rope_flags, swap_kind=swap_kind, dim=dim,
        scale=dim ** -0.5, bq=bq, rc=min(rc, bq), seq=seq, kw=kw, window=window, tmode=tmode,
        k_major=k_major)
    return pl.pallas_call(
        kern,
        name=name,
        grid=(batch, n_kv, nqb),
        in_specs=in_specs,
        out_specs=pl.BlockSpec((bq, n_grp * LANE), lambda b, g, i: (b * nqb + i, g)),
        out_shape=jax.ShapeDtypeStruct((batch * seq, n_kv * n_grp * LANE), F32),
        scratch_shapes=[pltpu.VMEM((n_chunk * LANE, seq) if k_major else (seq, n_chunk * LANE), BF16),
                        pltpu.VMEM((seq, 2 * LANE), BF16)],
        compiler_params=_params(3),
    )(*args)


def _rope_cs(pos, half):
    inv_freq = ROPE_THETA ** (-jnp.arange(half, dtype=F32) / half)
    ang = pos.astype(F32)[:, None] * inv_freq[None, :]
    return jnp.cos(ang), jnp.sin(ang)


def _rope_tables(seq):
    t = jnp.arange(seq)
    c, s = _rope_cs(t, HEAD_DIM // 2)
    a_cos, a_sin = jnp.concatenate([c, c], -1), jnp.concatenate([-s, s], -1)
    cr, sr = _rope_cs(t // GRID_W, HEAD_DIM // 4)
    cc, sc = _rope_cs(t % GRID_W, HEAD_DIM // 4)
    b_cos = jnp.concatenate([cr, cr, cc, cc], -1)
    b_sin = jnp.concatenate([-sr, sr, -sc, sc], -1)
    cd, sd = _rope_cs(t, D_ROPE // 2)
    z = jnp.zeros((seq, LANE - D_ROPE), F32)
    d_cos = jnp.concatenate([cd, cd, z], -1)
    d_sin = jnp.concatenate([-sd, sd, z], -1)
    return (a_cos, a_sin), (b_cos, b_sin), (d_cos, d_sin)


def _dilation_multiplicity(seq):
    t = jnp.arange(seq)
    delta = t[None, :] - t[:, None]
    mult = jnp.zeros((seq, seq), jnp.int32)
    for window, d in A_PATTERNS:
        half = window // (2 * d)
        mult = mult + ((delta % d == 0) & (jnp.abs(delta) <= half * d)).astype(jnp.int32)
    return mult.astype(F32)


def _neighbourhood_tables(rpb, seq):
    depth, heads = rpb.shape[:2]
    rows = seq // GRID_W
    wr = min(C_WIN_R, rows)
    kr = min(C_KROWS, rows)
    nqb = rows // C_QROWS
    c = jnp.arange(GRID_W)
    col_start = jnp.clip(c - C_WIN_C // 2, 0, GRID_W - C_WIN_C)
    col_ok = (c[None, :] >= col_start[:, None]) & (c[None, :] < col_start[:, None] + C_WIN_C)
    dcol = jnp.clip(c[None, :] - c[:, None] + (C_WIN_C - 1), 0, 2 * C_WIN_C - 2)
    by_col = jnp.take(rpb, dcol.reshape(-1), axis=3, mode="clip")
    by_col = by_col.reshape(depth, heads, 2 * C_WIN_R - 1, GRID_W, GRID_W)
    by_col = jnp.where(col_ok, by_col, NEG_INF)
    lo_pad = -NBR_D_LO
    hi_pad = NBR_D_HI + 1 - (2 * C_WIN_R - 2)
    by_col = jnp.pad(by_col, ((0, 0), (0, 0), (lo_pad, hi_pad), (0, 0), (0, 0)), constant_values=NEG_INF)
    slabs = jnp.concatenate([by_col[:, :, :-1], by_col[:, :, 1:]], axis=-1)

    qb = jnp.arange(nqb)
    r = qb[:, None] * C_QROWS + jnp.arange(C_QROWS)[None, :]
    win0 = jnp.clip(qb * C_QROWS - C_WIN_R // 2, 0, rows - kr)
    krow = win0[:, None] + jnp.arange(kr)[None, :]
    row_start = jnp.clip(r - wr // 2, 0, rows - wr)
    row_ok = (krow[:, None, :] >= row_start[:, :, None]) & (krow[:, None, :] < row_start[:, :, None] + wr)
    row_mask = jnp.where(row_ok, 0.0, NEG_INF).astype(F32)
    row_mask = jnp.repeat(row_mask, GRID_W, axis=2)
    row_mask = jnp.pad(row_mask, ((0, 0), (0, 8 - C_QROWS), (0, 0)))
    return slabs, row_mask


def _mixer(x2, h, l, w_in, w_out, lw, tabs, batch, seq):
    d_model = x2.shape[1]
    gw = d_model // N_MIXERS
    heads = gw // HEAD_DIM
    kv_heads = heads // 4
    nb = gw // LANE
    q_rank = lw["g_cq"].shape[0]
    (a_cos, a_sin), (b_cos, b_sin), (d_cos, d_sin) = tabs["rope"]

    col = {"qa": 0, "ka": nb, "va": 2 * nb, "qb": 3 * nb, "kb": 4 * nb, "vb": 4 * nb + kv_heads}
    col["qc"] = 4 * nb + 2 * kv_heads
    col["kc"] = col["qc"] + nb
    col["vc"] = col["kc"] + nb
    col["cq"] = col["vc"] + nb
    col["ckv"] = col["cq"] + q_rank // LANE
    n_main = (col["ckv"] + D_KV_RANK // LANE) * LANE
    proj = _matmul_ws(h, w_in, l, n_main, name="in_proj")
    kpe = _matmul_ws(h, w_in, l, LANE, bn=LANE, col0=n_main, name="kpe_proj")

    g_abc = lw["g_qk_abc"]
    common = dict(batch=batch, seq=seq)

    oa = _attention(proj, col["qa"], [(proj, col["ka"], 1)], proj, col["va"], g_abc[0, 0], g_abc[0, 1],
                    a_cos, a_sin, tabs["mult"], n_kv=heads, n_grp=1, rope_flags=(True,), swap_kind="half64",
                    dim=HEAD_DIM, bq=min(512, seq), rc=ATTN_ROWS, window="full", tmode="mul", name="attn_a",
                    **common)
    ob = _attention(proj, col["qb"], [(proj, col["kb"], 1)], proj, col["vb"], g_abc[1, 0], g_abc[1, 1],
                    b_cos, b_sin, None, n_kv=kv_heads, n_grp=4, rope_flags=(True,), swap_kind="half32",
                    dim=HEAD_DIM, bq=min(128, seq), rc=ATTN_ROWS, window="full", tmode=None, name="attn_b",
                    **common)
    oc = _attention(proj, col["qc"], [(proj, col["kc"], 1)], proj, col["vc"], g_abc[2, 0], g_abc[2, 1],
                    None, None, (tabs["nbr"][0], l, tabs["nbr"][1]), n_kv=heads, n_grp=1, rope_flags=(False,),
                    swap_kind=None, dim=HEAD_DIM, bq=C_QROWS * GRID_W, rc=ATTN_ROWS, window="rows", tmode="nbr",
                    name="attn_c", **common)

    cq_n = _rmsnorm([_pieces(proj, col["cq"] * LANE, q_rank)], lw["g_cq"])
    ckv_n = _rmsnorm([_pieces(proj, col["ckv"] * LANE, D_KV_RANK)], lw["g_ckv"])
    qd = _matmul(cq_n, lw["w_uq"], name="uq_proj")
    kvd = _matmul(ckv_n, lw["w_ukv"], name="ukv_proj")
    od = _attention(qd, 0, [(kvd, 0, 1), (kpe, 0, 0)], kvd, heads, lw["gq_mla"], lw["gk_mla"],
                    d_cos, d_sin, None, n_kv=heads, n_grp=1, rope_flags=(False, True), swap_kind="half32",
                    dim=D_QK, bq=min(512, seq), rc=ATTN_ROWS, window="full", tmode=None, name="attn_d",
                    **common)

    normed = _rmsnorm([_pieces(o, 0, gw) for o in (oa, ob, oc, od)], lw["g_grp"])
    return _matmul_ws(normed, w_out, l, d_model, res=x2, name="out_proj")


def _layer_weights(l, d_model, g_qk_abc, w_uq, w_ukv, g_cq, g_ckv, g_qk_mla, g_grp):
    heads = d_model // N_MIXERS // HEAD_DIM
    q_rank = w_uq.shape[1]
    uq = w_uq[l].reshape(q_rank, heads, D_QK)
    uq = jnp.pad(uq, ((0, 0), (0, 0), (0, 2 * LANE - D_QK))).reshape(q_rank, heads * 2 * LANE).astype(BF16)
    ukv = w_ukv[l].reshape(D_KV_RANK, heads, D_NOPE + HEAD_DIM)
    ukv = jnp.concatenate([ukv[:, :, :D_NOPE].reshape(D_KV_RANK, heads * D_NOPE),
                           ukv[:, :, D_NOPE:].reshape(D_KV_RANK, heads * HEAD_DIM)], axis=1).astype(BF16)
    pad_g = lambda g: jnp.pad(g, (0, 2 * LANE - D_QK))
    return dict(w_uq=uq, w_ukv=ukv, g_qk_abc=g_qk_abc[l], g_cq=g_cq[l], g_ckv=g_ckv[l],
                gq_mla=pad_g(g_qk_mla[l, 0]), gk_mla=pad_g(g_qk_mla[l, 1]), g_grp=g_grp[l])


def kernel(x, g_mix, w_in, g_qk_abc, w_uq, w_ukv, g_cq, g_ckv, g_qk_mla, rpb, g_grp, w_out, g_ffn, w_gate,
           w_up, w_down, w_router, we_gate, we_up, we_down):
    batch, seq, d_model = x.shape
    depth = g_mix.shape[0]
    n_exp = we_gate.shape[1]
    tabs = dict(rope=_rope_tables(seq), mult=_dilation_multiplicity(seq), nbr=_neighbourhood_tables(rpb, seq))
    x2 = x.reshape(batch * seq, d_model)
    for l in range(depth):
        lw = _layer_weights(l, d_model, g_qk_abc, w_uq, w_ukv, g_cq, g_ckv, g_qk_mla, g_grp)
        h = _rmsnorm([_pieces(x2, 0, d_model)], g_mix[l])
        x2 = _mixer(x2, h, l, w_in, w_out, lw, tabs, batch, seq)
        i = l // 2
        if l % 2 == 0:
            h = _rmsnorm([_pieces(x2, 0, d_model)], g_ffn[l])
            a = _gateup_dense(h, w_gate, w_up, i)
            x2 = _matmul(a, w_down[i].astype(BF16), res=x2, bk=w_down.shape[1] // 2, name="ffn_down")
        else:
            h3 = _rmsnorm([_pieces(x2, 0, d_model)], g_ffn[l], split_rows=True)
            wr = jnp.pad(w_router[i], ((0, 0), (0, LANE - n_exp))).astype(BF16)
            x2 = _moe(x2, h3, wr, we_gate, we_up, we_down, i)
    return x2.reshape(batch, seq, d_model)
```

```python
import functools
import math

import jax
import jax.numpy as jnp
from jax import lax
from jax.experimental import pallas as pl
from jax.experimental.pallas import tpu as pltpu

F32 = jnp.float32
BF16 = jnp.bfloat16

LANE = 128
HEAD_DIM = 128
N_MIXERS = 4
ROPE_THETA = 10000.0
EPS = 1e-6
NEG_INF = -1e30
LOG2_E = 1.4426950408889634
GRID_W = 64
A_PATTERNS = ((128, 1), (512, 4), (2048, 16))
C_WIN_R = 8
C_WIN_C = 16
C_QROWS = 4
C_KROWS = 12
ATTN_ROWS = 128
NBR_D_LO = (C_WIN_R - 1) - (C_KROWS - C_QROWS) - (C_QROWS - 1)
NBR_D_HI = (C_WIN_R - 1) + (C_KROWS - 2)
D_KV_RANK = 512
D_NOPE = 128
D_ROPE = 64
D_QK = D_NOPE + D_ROPE
TOP_K = 2
CAST_ROWS = 512
VMEM_LIMIT_BYTES = 56 * 1024 * 1024


def _blk(dim, target, mult=LANE):
    best = None
    d = mult
    while d <= min(dim, target):
        if dim % d == 0:
            best = d
        d += mult
    return best if best is not None else dim


def _params(n_grid):
    return pltpu.CompilerParams(dimension_semantics=("arbitrary",) * n_grid,
                                vmem_limit_bytes=VMEM_LIMIT_BYTES)


def _stage_bf16(w_ref, scr):
    rows = scr.shape[0]
    step = min(rows, CAST_ROWS)
    for r0 in range(0, rows, step):
        scr[r0:r0 + step, :] = w_ref[0, r0:r0 + step, :].astype(BF16)


def _rmsnorm_kernel(*refs, group_sizes, split_rows):
    n_in = sum(group_sizes)
    x_refs = refs[:n_in]
    g_ref = refs[n_in]
    o_ref = refs[n_in + 1]
    off = 0
    k = 0
    for size in group_sizes:
        pieces = [x_refs[k + p][...] for p in range(size)]
        k += size
        width = sum(p.shape[-1] for p in pieces)
        ssq = None
        for p in pieces:
            t = jnp.sum(p * p, axis=-1, keepdims=True)
            ssq = t if ssq is None else ssq + t
        r = lax.rsqrt(ssq / width + EPS)
        for p in pieces:
            w = p.shape[-1]
            y = (p * r) * g_ref[:, off:off + w]
            if split_rows:
                o_ref[...] = y.reshape(o_ref.shape)
            else:
                o_ref[:, off:off + w] = y.astype(o_ref.dtype)
            off += w


def _pieces(arr, col_off, width):
    pw = math.gcd(col_off, width) if col_off else width
    return [(arr, col_off // pw + p, pw) for p in range(width // pw)]


def _rmsnorm(groups, gain, bm=256, out_dtype=BF16, split_rows=False):
    flat = [p for g in groups for p in g]
    n = flat[0][0].shape[0]
    bm = _blk(n, bm, 8)
    total = sum(w for _, _, w in flat)
    in_specs = [pl.BlockSpec((bm, w), functools.partial(lambda i, c: (i, c), c=c)) for _, c, w in flat]
    in_specs.append(pl.BlockSpec((1, total), lambda i: (0, 0)))
    if split_rows:
        assert len(flat) == 1
        out_spec = pl.BlockSpec((bm, total // LANE, LANE), lambda i: (i, 0, 0))
        out_shape = jax.ShapeDtypeStruct((n, total // LANE, LANE), F32)
    else:
        out_spec = pl.BlockSpec((bm, total), lambda i: (i, 0))
        out_shape = jax.ShapeDtypeStruct((n, total), out_dtype)
    return pl.pallas_call(
        functools.partial(_rmsnorm_kernel, group_sizes=tuple(len(g) for g in groups), split_rows=split_rows),
        grid=(n // bm,),
        in_specs=in_specs,
        out_specs=out_spec,
        out_shape=out_shape,
        compiler_params=_params(1),
        name="rmsnorm",
    )(*[a for a, _, _ in flat], gain.reshape(1, total))


def _mm_kernel(a_ref, w_ref, *rest, nk, has_res):
    if has_res:
        r_ref, o_ref = rest
    else:
        (o_ref,) = rest
    part = jnp.dot(a_ref[...], w_ref[...], preferred_element_type=F32)
    if nk == 1:
        o_ref[...] = (r_ref[...] + part) if has_res else part
        return
    k = pl.program_id(2)

    @pl.when(k == 0)
    def _():
        o_ref[...] = (r_ref[...] + part) if has_res else part

    @pl.when(k > 0)
    def _():
        o_ref[...] += part


def _matmul(a, w, res=None, bm=1024, bn=512, bk=None, name="matmul"):
    m, kdim = a.shape
    n = w.shape[1]
    bm = _blk(m, bm, 8)
    bn = _blk(n, bn)
    bk = kdim if bk is None else _blk(kdim, bk)
    nk = kdim // bk
    in_specs = [pl.BlockSpec((bm, bk), lambda i, j, k: (i, k)),
                pl.BlockSpec((bk, bn), lambda i, j, k: (k, j))]
    args = [a, w]
    if res is not None:
        in_specs.append(pl.BlockSpec((bm, bn), lambda i, j, k: (i, j)))
        args.append(res)
    return pl.pallas_call(
        functools.partial(_mm_kernel, nk=nk, has_res=res is not None),
        grid=(m // bm, n // bn, nk),
        in_specs=in_specs,
        out_specs=pl.BlockSpec((bm, bn), lambda i, j, k: (i, j)),
        out_shape=jax.ShapeDtypeStruct((m, n), F32),
        compiler_params=_params(3),
        name=name,
    )(*args)


def _norm_mm_kernel(*refs, n_pieces):
    x_refs = refs[:n_pieces]
    g_ref, w_ref, o_ref = refs[n_pieces:]
    pieces = [r[...] for r in x_refs]
    width = sum(p.shape[-1] for p in pieces)
    ssq = None
    for p in pieces:
        t = jnp.sum(p * p, axis=-1, keepdims=True)
        ssq = t if ssq is None else ssq + t
    r = lax.rsqrt(ssq / width + EPS)
    acc = None
    off = 0
    for p in pieces:
        w = p.shape[-1]
        y = ((p * r) * g_ref[:, off:off + w]).astype(BF16)
        part = jnp.dot(y, w_ref[off:off + w, :], preferred_element_type=F32)
        acc = part if acc is None else acc + part
        off += w
    o_ref[...] = acc


def _norm_matmul(pieces, gain, w, bm=512, name="norm_matmul"):
    m = pieces[0][0].shape[0]
    kdim, n = w.shape
    bm = _blk(m, bm, 8)
    in_specs = [pl.BlockSpec((bm, pw), functools.partial(lambda i, c: (i, c), c=c)) for _, c, pw in pieces]
    in_specs += [pl.BlockSpec((1, kdim), lambda i: (0, 0)), pl.BlockSpec((kdim, n), lambda i: (0, 0))]
    return pl.pallas_call(
        functools.partial(_norm_mm_kernel, n_pieces=len(pieces)),
        grid=(m // bm,),
        in_specs=in_specs,
        out_specs=pl.BlockSpec((bm, n), lambda i: (i, 0)),
        out_shape=jax.ShapeDtypeStruct((m, n), F32),
        compiler_params=_params(1),
        name=name,
    )(*[a for a, _, _ in pieces], gain.reshape(1, kdim), w)


def _mm_ws_kernel(a_ref, w_ref, *rest, has_res, valid_cols, w_rows_are_cols):
    if has_res:
        r_ref, o_ref, w_scr = rest
    else:
        o_ref, w_scr = rest

    @pl.when(pl.program_id(1) == 0)
    def _():
        _stage_bf16(w_ref, w_scr)

    if w_rows_are_cols:
        part = lax.dot_general(a_ref[...], w_scr[...], (((1,), (1,)), ((), ())), preferred_element_type=F32)
    else:
        part = jnp.dot(a_ref[...], w_scr[...], preferred_element_type=F32)
    if valid_cols < part.shape[1]:
        lane = lax.broadcasted_iota(jnp.int32, part.shape, 1)
        part = jnp.where(lane < valid_cols, part, 0.0)
    o_ref[...] = (r_ref[...] + part) if has_res else part


def _matmul_ws(a, w3, layer, n_cols, res=None, bm=1024, bn=512, col0=0, w_rows_are_cols=False, name="matmul_ws"):
    m, kdim = a.shape
    bm = _blk(m, bm, 8)
    bn = _blk(n_cols, bn)
    blk0 = col0 // bn
    assert blk0 * bn == col0
    valid_cols = min(n_cols, w3.shape[1 if w_rows_are_cols else 2] - col0)
    assert valid_cols == n_cols or n_cols == bn
    if w_rows_are_cols:
        w_spec = pl.BlockSpec((1, bn, kdim), lambda j, i: (layer, blk0 + j, 0))
        w_scr = pltpu.VMEM((bn, kdim), BF16)
    else:
        w_spec = pl.BlockSpec((1, kdim, bn), lambda j, i: (layer, 0, blk0 + j))
        w_scr = pltpu.VMEM((kdim, bn), BF16)
    in_specs = [pl.BlockSpec((bm, kdim), lambda j, i: (i, 0)), w_spec]
    args = [a, w3]
    if res is not None:
        in_specs.append(pl.BlockSpec((bm, bn), lambda j, i: (i, j)))
        args.append(res)
    return pl.pallas_call(
        functools.partial(_mm_ws_kernel, has_res=res is not None, valid_cols=valid_cols,
                          w_rows_are_cols=w_rows_are_cols),
        grid=(n_cols // bn, m // bm),
        in_specs=in_specs,
        out_specs=pl.BlockSpec((bm, bn), lambda j, i: (i, j)),
        out_shape=jax.ShapeDtypeStruct((m, n_cols), F32),
        scratch_shapes=[w_scr],
        compiler_params=_params(2),
        name=name,
    )(*args)


def _swiglu(g, u):
    return (g / (1.0 + jnp.exp(-g))) * u


def _gateup_kernel(h_ref, wg_ref, wu_ref, o_ref, wg_scr, wu_scr):
    @pl.when(pl.program_id(1) == 0)
    def _():
        _stage_bf16(wg_ref, wg_scr)
        _stage_bf16(wu_ref, wu_scr)

    h = h_ref[...]
    g = jnp.dot(h, wg_scr[...], preferred_element_type=F32)
    u = jnp.dot(h, wu_scr[...], preferred_element_type=F32)
    o_ref[...] = _swiglu(g, u).astype(o_ref.dtype)


def _gateup_dense(h, wg3, wu3, layer, bm=1024, bn=256):
    m, d = h.shape
    f = wg3.shape[2]
    bm = _blk(m, bm, 8)
    bn = _blk(f, bn)
    w_spec = pl.BlockSpec((1, d, bn), lambda j, i: (layer, 0, j))
    return pl.pallas_call(
        _gateup_kernel,
        grid=(f // bn, m // bm),
        in_specs=[pl.BlockSpec((bm, d), lambda j, i: (i, 0)), w_spec, w_spec],
        out_specs=pl.BlockSpec((bm, bn), lambda j, i: (i, j)),
        out_shape=jax.ShapeDtypeStruct((m, f), BF16),
        scratch_shapes=[pltpu.VMEM((d, bn), BF16), pltpu.VMEM((d, bn), BF16)],
        compiler_params=_params(2),
        name="ffn_gateup",
    )(h, wg3, wu3)


ROUTE_IDX1, ROUTE_IDX2, ROUTE_G1, ROUTE_G2, ROUTE_RANK1, ROUTE_RANK2 = range(6)


def _router_kernel(h_ref, w_ref, route_ref, cnt_ref, carry, *, n_exp):
    @pl.when(pl.program_id(0) == 0)
    def _():
        carry[...] = jnp.zeros_like(carry)

    bm = h_ref.shape[0]
    h = h_ref[...].reshape(bm, h_ref.shape[1] * LANE).astype(BF16)
    logits = jnp.dot(h, w_ref[...], preferred_element_type=F32)
    lane = lax.broadcasted_iota(jnp.int32, logits.shape, 1).astype(F32)
    logits = jnp.where(lane < n_exp, logits, -jnp.inf)
    v1 = jnp.max(logits, axis=-1, keepdims=True)
    i1 = jnp.min(jnp.where(logits == v1, lane, float(LANE)), axis=-1, keepdims=True)
    rest = jnp.where(lane == i1, -jnp.inf, logits)
    v2 = jnp.max(rest, axis=-1, keepdims=True)
    i2 = jnp.min(jnp.where(rest == v2, lane, float(LANE)), axis=-1, keepdims=True)
    e2 = jnp.exp(v2 - v1)
    den = 1.0 + e2
    g1 = 1.0 / den
    g2 = e2 / den
    chosen = jnp.where((lane == i1) | (lane == i2), 1.0, 0.0)
    row = lax.broadcasted_iota(jnp.int32, (bm, bm), 0)
    col = lax.broadcasted_iota(jnp.int32, (bm, bm), 1)
    earlier = jnp.where(col < row, 1.0, 0.0).astype(BF16)
    before = jnp.dot(earlier, chosen.astype(BF16), preferred_element_type=F32) + carry[...]
    rank1 = jnp.sum(jnp.where(lane == i1, before, 0.0), axis=-1, keepdims=True)
    rank2 = jnp.sum(jnp.where(lane == i2, before, 0.0), axis=-1, keepdims=True)
    carry[...] += jnp.sum(chosen, axis=0, keepdims=True)
    route = jnp.zeros_like(logits)
    for pos, val in ((ROUTE_IDX1, i1), (ROUTE_IDX2, i2), (ROUTE_G1, g1), (ROUTE_G2, g2),
                     (ROUTE_RANK1, rank1), (ROUTE_RANK2, rank2)):
        route = jnp.where(lane == pos, val, route)
    route_ref[...] = route
    cnt_ref[...] = jnp.broadcast_to(carry[...], cnt_ref.shape)


def _router(h3, w_router_padded, n_exp, bm=512):
    m, dl, _ = h3.shape
    bm = _blk(m, bm, 8)
    return pl.pallas_call(
        functools.partial(_router_kernel, n_exp=n_exp),
        grid=(m // bm,),
        in_specs=[pl.BlockSpec((bm, dl, LANE), lambda i: (i, 0, 0)),
                  pl.BlockSpec((dl * LANE, LANE), lambda i: (0, 0))],
        out_specs=[pl.BlockSpec((bm, LANE), lambda i: (i, 0)),
                   pl.BlockSpec((8, LANE), lambda i: (0, 0))],
        out_shape=[jax.ShapeDtypeStruct((m, LANE), F32), jax.ShapeDtypeStruct((8, LANE), F32)],
        scratch_shapes=[pltpu.VMEM((1, LANE), F32)],
        compiler_params=_params(1),
        name="router",
    )(h3, w_router_padded)


def _gather_kernel(s1_ref, s2_ref, te_ref, offs_ref, cnt_ref, nu_ref, src_ref, o_ref, tok_scr, buf, sem, *,
                   tb, bm, n_tiles, n_tok):
    i = pl.program_id(0)

    @pl.when(i == 0)
    def _():
        def place(t, c):
            tok_scr[s1_ref[t]] = t
            tok_scr[s2_ref[t]] = t
            return c

        lax.fori_loop(0, n_tok, place, 0, unroll=8)

    def used(tile):
        return tile * tb < nu_ref[0] * bm

    def row_copy(r, slot, t):
        return pltpu.make_async_copy(src_ref.at[t], buf.at[slot, r], sem.at[slot])

    def issue(tile, slot):
        e = te_ref[(tile * tb) // bm]
        first = offs_ref[e]
        last = first + cnt_ref[e] - 1

        def body(r, c):
            s = tile * tb + r
            s = jnp.where(s <= last, s, first)
            row_copy(r, slot, tok_scr[s]).start()
            return c

        lax.fori_loop(0, tb, body, 0, unroll=8)

    @pl.when((i == 0) & used(0))
    def _():
        issue(0, 0)

    @pl.when((i + 1 < n_tiles) & used(i + 1))
    def _():
        issue(i + 1, (i + 1) % 2)

    slot = i % 2

    @pl.when(used(i))
    def _():
        def wait(r, c):
            row_copy(r, slot, 0).wait()
            return c

        lax.fori_loop(0, tb, wait, 0, unroll=8)
        o_ref[...] = buf[slot].reshape(o_ref.shape).astype(o_ref.dtype)

    @pl.when(jnp.logical_not(used(i)))
    def _():
        o_ref[...] = jnp.zeros_like(o_ref)


def _gather_rows(slot1, slot2, tile_expert, offs, counts, n_used, src3, n_slot, bm, tb=256):
    n_tok, dl, _ = src3.shape
    tb = min(tb, bm)
    n_tiles = n_slot // tb
    return pl.pallas_call(
        functools.partial(_gather_kernel, tb=tb, bm=bm, n_tiles=n_tiles, n_tok=n_tok),
        grid_spec=pltpu.PrefetchScalarGridSpec(
            num_scalar_prefetch=6,
            grid=(n_tiles,),
            in_specs=[pl.BlockSpec(memory_space=pl.ANY)],
            out_specs=pl.BlockSpec((tb, dl * LANE), lambda i, *_: (i, 0)),
            scratch_shapes=[pltpu.SMEM((n_slot,), jnp.int32), pltpu.VMEM((2, tb, dl, LANE), F32),
                            pltpu.SemaphoreType.DMA((2,))]),
        out_shape=jax.ShapeDtypeStruct((n_slot, dl * LANE), BF16),
        compiler_params=_params(1),
        name="gather_rows",
    )(slot1, slot2, tile_expert, offs, counts, n_used, src3)


def _new_expert(te_ref, i):
    return (i == 0) | (te_ref[i] != te_ref[jnp.maximum(i - 1, 0)])


def _expert_gateup_kernel(te_ref, nu_ref, x_ref, wg_ref, wu_ref, o_ref, wg_scr, wu_scr):
    i = pl.program_id(1)

    @pl.when(_new_expert(te_ref, i))
    def _():
        _stage_bf16(wg_ref, wg_scr)
        _stage_bf16(wu_ref, wu_scr)

    @pl.when(i < nu_ref[0])
    def _():
        x = x_ref[...]
        g = jnp.dot(x, wg_scr[...], preferred_element_type=F32)
        u = jnp.dot(x, wu_scr[...], preferred_element_type=F32)
        o_ref[...] = _swiglu(g, u).astype(o_ref.dtype)

    @pl.when(i >= nu_ref[0])
    def _():
        o_ref[...] = jnp.zeros_like(o_ref)


def _expert_gateup(tile_expert, n_used, xs, wg4, wu4, layer, bm, bn=512):
    p, d = xs.shape
    ef = wg4.shape[3]
    bn = _blk(ef, bn)
    w_spec = pl.BlockSpec((None, 1, d, bn), lambda j, i, te, nu: (layer, te[i], 0, j))
    return pl.pallas_call(
        _expert_gateup_kernel,
        grid_spec=pltpu.PrefetchScalarGridSpec(
            num_scalar_prefetch=2,
            grid=(ef // bn, p // bm),
            in_specs=[pl.BlockSpec((bm, d), lambda j, i, te, nu: (i, 0)), w_spec, w_spec],
            out_specs=pl.BlockSpec((bm, bn), lambda j, i, te, nu: (i, j)),
            scratch_shapes=[pltpu.VMEM((d, bn), BF16), pltpu.VMEM((d, bn), BF16)]),
        out_shape=jax.ShapeDtypeStruct((p, ef), BF16),
        compiler_params=_params(2),
        name="expert_gateup",
    )(tile_expert, n_used, xs, wg4, wu4)


def _expert_down_kernel(te_ref, nu_ref, a_ref, w_ref, o_ref, w_scr):
    i = pl.program_id(1)

    @pl.when(_new_expert(te_ref, i))
    def _():
        _stage_bf16(w_ref, w_scr)

    @pl.when(i < nu_ref[0])
    def _():
        y = jnp.dot(a_ref[...], w_scr[...], preferred_element_type=F32)
        o_ref[...] = y.reshape(o_ref.shape)

    @pl.when(i >= nu_ref[0])
    def _():
        o_ref[...] = jnp.zeros_like(o_ref)


def _expert_down(tile_expert, n_used, a, wd4, layer, bm, bn=1024):
    p, ef = a.shape
    d = wd4.shape[3]
    bn = _blk(d, bn, 8 * LANE)
    return pl.pallas_call(
        _expert_down_kernel,
        grid_spec=pltpu.PrefetchScalarGridSpec(
            num_scalar_prefetch=2,
            grid=(d // bn, p // bm),
            in_specs=[pl.BlockSpec((bm, ef), lambda j, i, te, nu: (i, 0)),
                      pl.BlockSpec((None, 1, ef, bn), lambda j, i, te, nu: (layer, te[i], 0, j))],
            out_specs=pl.BlockSpec((bm, bn // LANE, LANE), lambda j, i, te, nu: (i, j, 0)),
            scratch_shapes=[pltpu.VMEM((ef, bn), BF16)]),
        out_shape=jax.ShapeDtypeStruct((p, d // LANE, LANE), F32),
        compiler_params=_params(2),
        name="expert_down",
    )(tile_expert, n_used, a, wd4)


def _combine_kernel(s1_ref, s2_ref, y_ref, x_ref, route_ref, o_ref, buf, sem, *, tb, n_tiles):
    i = pl.program_id(0)

    def row_copy(k, r, slot, s):
        return pltpu.make_async_copy(y_ref.at[s], buf.at[slot, k, r], sem.at[slot])

    def issue(tile, slot):
        def body(r, c):
            t = tile * tb + r
            row_copy(0, r, slot, s1_ref[t]).start()
            row_copy(1, r, slot, s2_ref[t]).start()
            return c

        lax.fori_loop(0, tb, body, 0, unroll=4)

    @pl.when(i == 0)
    def _():
        issue(0, 0)

    @pl.when(i + 1 < n_tiles)
    def _():
        issue(i + 1, (i + 1) % 2)

    slot = i % 2

    def wait(r, c):
        row_copy(0, r, slot, 0).wait()
        row_copy(1, r, slot, 0).wait()
        return c

    lax.fori_loop(0, tb, wait, 0, unroll=4)
    route = route_ref[...]
    g1 = route[:, ROUTE_G1:ROUTE_G1 + 1]
    g2 = route[:, ROUTE_G2:ROUTE_G2 + 1]
    y1 = buf[slot, 0].reshape(o_ref.shape)
    y2 = buf[slot, 1].reshape(o_ref.shape)
    o_ref[...] = x_ref[...] + (g1 * y1 + g2 * y2)


def _combine(slot1, slot2, y3, x2, route, tb=128):
    n, d = x2.shape
    dl = d // LANE
    tb = _blk(n, tb, 8)
    n_tiles = n // tb
    return pl.pallas_call(
        functools.partial(_combine_kernel, tb=tb, n_tiles=n_tiles),
        grid_spec=pltpu.PrefetchScalarGridSpec(
            num_scalar_prefetch=2,
            grid=(n_tiles,),
            in_specs=[pl.BlockSpec(memory_space=pl.ANY),
                      pl.BlockSpec((tb, d), lambda i, s1, s2: (i, 0)),
                      pl.BlockSpec((tb, LANE), lambda i, s1, s2: (i, 0))],
            out_specs=pl.BlockSpec((tb, d), lambda i, s1, s2: (i, 0)),
            scratch_shapes=[pltpu.VMEM((2, 2, tb, dl, LANE), F32), pltpu.SemaphoreType.DMA((2,))]),
        out_shape=jax.ShapeDtypeStruct((n, d), F32),
        compiler_params=_params(1),
        name="expert_combine",
    )(slot1, slot2, y3, x2, route)


def _moe(x2, h3, w_router_padded, wg4, wu4, wd4, layer, bm=512):
    n = x2.shape[0]
    n_exp = wg4.shape[1]
    bm = _blk(n, bm, 8)
    route, cnt = _router(h3, w_router_padded, n_exp)
    counts = cnt[0, :n_exp].astype(jnp.int32)
    padded = ((counts + bm - 1) // bm) * bm
    ends = jnp.cumsum(padded)
    offs = ends - padded
    idx1 = route[:, ROUTE_IDX1].astype(jnp.int32)
    idx2 = route[:, ROUTE_IDX2].astype(jnp.int32)
    slot1 = offs[idx1] + route[:, ROUTE_RANK1].astype(jnp.int32)
    slot2 = offs[idx2] + route[:, ROUTE_RANK2].astype(jnp.int32)
    n_slot = n * TOP_K + n_exp * bm
    n_tiles = n_slot // bm
    tile_start = jnp.arange(n_tiles, dtype=jnp.int32) * bm
    tile_expert = jnp.minimum(jnp.sum((ends[None, :] <= tile_start[:, None]).astype(jnp.int32), axis=1), n_exp - 1)
    n_used = (ends[-1] // bm).reshape(1)
    xs = _gather_rows(slot1, slot2, tile_expert, offs, counts, n_used, h3, n_slot, bm)
    a = _expert_gateup(tile_expert, n_used, xs, wg4, wu4, layer, bm)
    y3 = _expert_down(tile_expert, n_used, a, wd4, layer, bm)
    return _combine(slot1, slot2, y3, x2, route)


def _swap(y, kind):
    if kind == "half64":
        return pltpu.roll(y, 64, 1)
    lane = lax.broadcasted_iota(jnp.int32, y.shape, 1)
    lo = pltpu.roll(y, 96, 1)
    hi = pltpu.roll(y, 32, 1)
    return jnp.where((lane & 63) < 32, lo, hi)


def _norm_rope(chunks, gains, cos, sin, rope_flags, swap_kind, dim, transpose=False):
    ssq = None
    for c in chunks:
        t = jnp.sum(c * c, axis=-1, keepdims=True)
        ssq = t if ssq is None else ssq + t
    r = lax.rsqrt(ssq / dim + EPS)
    outs = []
    for c, g, flag in zip(chunks, gains, rope_flags):
        y = (c * r) * g
        if flag:
            y = y * cos + _swap(y, swap_kind) * sin
        outs.append((y.T if transpose else y).astype(BF16))
    return outs


def _nbr_bias(t_ref, rm_ref, win_minus_q, r0, rc, kw):
    rows = []
    for jj in range(rc // GRID_W):
        j = r0 // GRID_W + jj
        slabs = []
        for c in range(kw // LANE):
            d = win_minus_q + (C_WIN_R - 1) + 2 * c - j
            slabs.append(t_ref[0, d - NBR_D_LO])
        rows.append(jnp.concatenate(slabs, axis=1) + rm_ref[0, j:j + 1, :])
    return rows[0] if len(rows) == 1 else jnp.concatenate(rows, axis=0)


def _attn_kernel(*refs, n_grp, n_chunk, rope_flags, swap_kind, dim, scale, bq, rc, seq, kw, window, tmode,
                 k_major):
    has_rope = any(rope_flags)
    it = iter(refs)
    q_ref = next(it)
    k_refs = [next(it) for _ in range(n_chunk)]
    v_ref = next(it)
    gq_ref = next(it)
    gk_ref = next(it)
    cos_ref = next(it) if has_rope else None
    sin_ref = next(it) if has_rope else None
    t_ref = next(it) if tmode else None
    rm_ref = next(it) if tmode == "nbr" else None
    o_ref = next(it)
    k_scr = next(it)
    v_scr = next(it)
    qb = pl.program_id(2)

    @pl.when(qb == 0)
    def _():
        step = min(seq, 512)
        gains = [gk_ref[:, c * LANE:(c + 1) * LANE] for c in range(n_chunk)]
        for r0 in range(0, seq, step):
            chunks = [kr[r0:r0 + step, :] for kr in k_refs]
            cos = cos_ref[r0:r0 + step, :] if has_rope else None
            sin = sin_ref[r0:r0 + step, :] if has_rope else None
            outs = _norm_rope(chunks, gains, cos, sin, rope_flags, swap_kind, dim, transpose=k_major)
            for c, o in enumerate(outs):
                if k_major:
                    k_scr[c * LANE:(c + 1) * LANE, r0:r0 + step] = o
                else:
                    k_scr[r0:r0 + step, c * LANE:(c + 1) * LANE] = o
            v_scr[r0:r0 + step, :LANE] = v_ref[r0:r0 + step, :].astype(BF16)
            v_scr[r0:r0 + step, LANE:] = jnp.ones((step, LANE), BF16)

    if window == "full":
        kwin = k_scr[...]
        vwin = v_scr[...]
    else:
        rows = seq // GRID_W
        start_row = jnp.clip(qb * C_QROWS - C_WIN_R // 2, 0, rows - kw // GRID_W)
        start = pl.multiple_of(start_row * GRID_W, GRID_W)
        kwin = k_scr[pl.ds(start, kw), :]
        vwin = v_scr[pl.ds(start, kw), :]

    row0 = pl.multiple_of(qb * bq, bq)
    gains = [gq_ref[:, c * LANE:(c + 1) * LANE] for c in range(n_chunk)]
    for r0 in range(0, bq, rc):
        cos = cos_ref[pl.ds(row0 + r0, rc), :] if has_rope else None
        sin = sin_ref[pl.ds(row0 + r0, rc), :] if has_rope else None
        for g in range(n_grp):
            base = g * n_chunk
            chunks = [q_ref[r0:r0 + rc, (base + c) * LANE:(base + c + 1) * LANE] for c in range(n_chunk)]
            outs = _norm_rope(chunks, gains, cos, sin, rope_flags, swap_kind, dim)
            q = outs[0] if n_chunk == 1 else jnp.concatenate(outs, axis=1)
            if k_major:
                s = jnp.dot(q, kwin, preferred_element_type=F32)
            else:
                s = lax.dot_general(q, kwin, (((1,), (1,)), ((), ())), preferred_element_type=F32)
            if tmode == "nbr":
                t = _nbr_bias(t_ref, rm_ref, start_row - qb * C_QROWS, r0, rc, kw)
                s = jnp.where(t > -1e29, s * scale + t, NEG_INF)
                p = jnp.exp(s - jnp.max(s, axis=-1, keepdims=True))
            else:
                if tmode == "mul":
                    t = t_ref[r0:r0 + rc, :]
                    s = jnp.where(t > 0.0, s, NEG_INF)
                p = jnp.exp2((s - jnp.max(s, axis=-1, keepdims=True)) * (scale * LOG2_E))
                if tmode == "mul":
                    p = p * t
            o = jnp.dot(p.astype(BF16), vwin, preferred_element_type=F32)
            o_ref[r0:r0 + rc, g * LANE:(g + 1) * LANE] = o[:, :LANE] / o[:, LANE:]


def _attention(q_arr, q_col0, k_arrs, v_arr, v_col0, gq, gk, cos, sin, table, *, batch, seq, n_kv, n_grp,
               rope_flags, swap_kind, dim, bq, rc, window, tmode, name):
    n_chunk = len(k_arrs)
    k_major = n_chunk > 1 and window == "full"
    nqb = seq // bq
    kw = seq if window == "full" else min(C_KROWS, seq // GRID_W) * GRID_W
    qw = n_grp * n_chunk * LANE
    q_blk0 = q_col0 * LANE // qw
    assert q_blk0 * qw == q_col0 * LANE
    in_specs = [pl.BlockSpec((bq, qw), lambda b, g, i: (b * nqb + i, q_blk0 + g))]
    args = [q_arr]
    for arr, col0, per_head in k_arrs:
        in_specs.append(pl.BlockSpec((seq, LANE), functools.partial(
            lambda b, g, i, col0, per_head: (b, col0 + g * per_head), col0=col0, per_head=per_head)))
        args.append(arr)
    in_specs.append(pl.BlockSpec((seq, LANE), lambda b, g, i: (b, v_col0 + g)))
    args.append(v_arr)
    in_specs.append(pl.BlockSpec((1, n_chunk * LANE), lambda b, g, i: (0, 0)))
    args.append(gq.reshape(1, n_chunk * LANE))
    in_specs.append(pl.BlockSpec((1, n_chunk * LANE), lambda b, g, i: (0, 0)))
    args.append(gk.reshape(1, n_chunk * LANE))
    if any(rope_flags):
        in_specs += [pl.BlockSpec((seq, LANE), lambda b, g, i: (0, 0))] * 2
        args += [cos, sin]
    if tmode == "nbr":
        slabs, layer, row_mask = table
        assert seq // GRID_W >= C_KROWS and 2 * GRID_W == LANE
        in_specs.append(pl.BlockSpec((None, 1) + slabs.shape[2:], lambda b, g, i: (layer, g, 0, 0, 0)))
        in_specs.append(pl.BlockSpec((1,) + row_mask.shape[1:], lambda b, g, i: (i, 0, 0)))
        args += [slabs, row_mask]
    elif tmode == "mul":
        in_specs.append(pl.BlockSpec((bq, kw), lambda b, g, i: (i, 0)))
        args.append(table)
    kern = functools.partial(
        _attn_kernel, n_grp=n_grp, n_chunk=n_chunk, rope_flags=rope_flags, swap_kind=swap_kind, dim=dim,
        scale=dim ** -0.5, bq=bq, rc=min(rc, bq), seq=seq, kw=kw, window=window, tmode=tmode,
        k_major=k_major)
    return pl.pallas_call(
        kern,
        name=name,
        grid=(batch, n_kv, nqb),
        in_specs=in_specs,
        out_specs=pl.BlockSpec((bq, n_grp * LANE), lambda b, g, i: (b * nqb + i, g)),
        out_shape=jax.ShapeDtypeStruct((batch * seq, n_kv * n_grp * LANE), F32),
        scratch_shapes=[pltpu.VMEM((n_chunk * LANE, seq) if k_major else (seq, n_chunk * LANE), BF16),
                        pltpu.VMEM((seq, 2 * LANE), BF16)],
        compiler_params=_params(3),
    )(*args)


def _rope_cs(pos, half):
    inv_freq = ROPE_THETA ** (-jnp.arange(half, dtype=F32) / half)
    ang = pos.astype(F32)[:, None] * inv_freq[None, :]
    return jnp.cos(ang), jnp.sin(ang)


def _rope_tables(seq):
    t = jnp.arange(seq)
    c, s = _rope_cs(t, HEAD_DIM // 2)
    a_cos, a_sin = jnp.concatenate([c, c], -1), jnp.concatenate([-s, s], -1)
    cr, sr = _rope_cs(t // GRID_W, HEAD_DIM // 4)
    cc, sc = _rope_cs(t % GRID_W, HEAD_DIM // 4)
    b_cos = jnp.concatenate([cr, cr, cc, cc], -1)
    b_sin = jnp.concatenate([-sr, sr, -sc, sc], -1)
    cd, sd = _rope_cs(t, D_ROPE // 2)
    z = jnp.zeros((seq, LANE - D_ROPE), F32)
    d_cos = jnp.concatenate([cd, cd, z], -1)
    d_sin = jnp.concatenate([-sd, sd, z], -1)
    return (a_cos, a_sin), (b_cos, b_sin), (d_cos, d_sin)


def _dilation_multiplicity(seq):
    t = jnp.arange(seq)
    delta = t[None, :] - t[:, None]
    mult = jnp.zeros((seq, seq), jnp.int32)
    for window, d in A_PATTERNS:
        half = window // (2 * d)
        mult = mult + ((delta % d == 0) & (jnp.abs(delta) <= half * d)).astype(jnp.int32)
    return mult.astype(F32)


def _neighbourhood_tables(rpb, seq):
    depth, heads = rpb.shape[:2]
    rows = seq // GRID_W
    wr = min(C_WIN_R, rows)
    kr = min(C_KROWS, rows)
    nqb = rows // C_QROWS
    c = jnp.arange(GRID_W)
    col_start = jnp.clip(c - C_WIN_C // 2, 0, GRID_W - C_WIN_C)
    col_ok = (c[None, :] >= col_start[:, None]) & (c[None, :] < col_start[:, None] + C_WIN_C)
    dcol = jnp.clip(c[None, :] - c[:, None] + (C_WIN_C - 1), 0, 2 * C_WIN_C - 2)
    by_col = jnp.take(rpb, dcol.reshape(-1), axis=3, mode="clip")
    by_col = by_col.reshape(depth, heads, 2 * C_WIN_R - 1, GRID_W, GRID_W)
    by_col = jnp.where(col_ok, by_col, NEG_INF)
    lo_pad = -NBR_D_LO
    hi_pad = NBR_D_HI + 1 - (2 * C_WIN_R - 2)
    by_col = jnp.pad(by_col, ((0, 0), (0, 0), (lo_pad, hi_pad), (0, 0), (0, 0)), constant_values=NEG_INF)
    slabs = jnp.concatenate([by_col[:, :, :-1], by_col[:, :, 1:]], axis=-1)

    qb = jnp.arange(nqb)
    r = qb[:, None] * C_QROWS + jnp.arange(C_QROWS)[None, :]
    win0 = jnp.clip(qb * C_QROWS - C_WIN_R // 2, 0, rows - kr)
    krow = win0[:, None] + jnp.arange(kr)[None, :]
    row_start = jnp.clip(r - wr // 2, 0, rows - wr)
    row_ok = (krow[:, None, :] >= row_start[:, :, None]) & (krow[:, None, :] < row_start[:, :, None] + wr)
    row_mask = jnp.where(row_ok, 0.0, NEG_INF).astype(F32)
    row_mask = jnp.repeat(row_mask, GRID_W, axis=2)
    row_mask = jnp.pad(row_mask, ((0, 0), (0, 8 - C_QROWS), (0, 0)))
    return slabs, row_mask


def _mixer(x2, h, l, w_in_t, w_out, lw, tabs, batch, seq):
    d_model = x2.shape[1]
    gw = d_model // N_MIXERS
    heads = gw // HEAD_DIM
    kv_heads = heads // 4
    nb = gw // LANE
    q_rank = lw["g_cq"].shape[0]
    (a_cos, a_sin), (b_cos, b_sin), (d_cos, d_sin) = tabs["rope"]

    col = {"qa": 0, "ka": nb, "va": 2 * nb, "qb": 3 * nb, "kb": 4 * nb, "vb": 4 * nb + kv_heads}
    col["qc"] = 4 * nb + 2 * kv_heads
    col["kc"] = col["qc"] + nb
    col["vc"] = col["kc"] + nb
    col["cq"] = col["vc"] + nb
    col["ckv"] = col["cq"] + q_rank // LANE
    n_main = (col["ckv"] + D_KV_RANK // LANE) * LANE
    proj = _matmul_ws(h, w_in_t, l, n_main, w_rows_are_cols=True, name="in_proj")
    kpe = _matmul_ws(h, w_in_t, l, LANE, bn=LANE, col0=n_main, w_rows_are_cols=True, name="kpe_proj")

    g_abc = lw["g_qk_abc"]
    common = dict(batch=batch, seq=seq)

    oa = _attention(proj, col["qa"], [(proj, col["ka"], 1)], proj, col["va"], g_abc[0, 0], g_abc[0, 1],
                    a_cos, a_sin, tabs["mult"], n_kv=heads, n_grp=1, rope_flags=(True,), swap_kind="half64",
                    dim=HEAD_DIM, bq=min(512, seq), rc=ATTN_ROWS, window="full", tmode="mul", name="attn_a",
                    **common)
    ob = _attention(proj, col["qb"], [(proj, col["kb"], 1)], proj, col["vb"], g_abc[1, 0], g_abc[1, 1],
                    b_cos, b_sin, None, n_kv=kv_heads, n_grp=4, rope_flags=(True,), swap_kind="half32",
                    dim=HEAD_DIM, bq=min(128, seq), rc=ATTN_ROWS, window="full", tmode=None, name="attn_b",
                    **common)
    oc = _attention(proj, col["qc"], [(proj, col["kc"], 1)], proj, col["vc"], g_abc[2, 0], g_abc[2, 1],
                    None, None, (tabs["nbr"][0], l, tabs["nbr"][1]), n_kv=heads, n_grp=1, rope_flags=(False,),
                    swap_kind=None, dim=HEAD_DIM, bq=C_QROWS * GRID_W, rc=ATTN_ROWS, window="rows", tmode="nbr",
                    name="attn_c", **common)

    qd = _norm_matmul(_pieces(proj, col["cq"] * LANE, q_rank), lw["g_cq"], lw["w_uq"], name="uq_proj")
    kvd = _norm_matmul(_pieces(proj, col["ckv"] * LANE, D_KV_RANK), lw["g_ckv"], lw["w_ukv"], name="ukv_proj")
    od = _attention(qd, 0, [(kvd, 0, 1), (kpe, 0, 0)], kvd, heads, lw["gq_mla"], lw["gk_mla"],
                    d_cos, d_sin, None, n_kv=heads, n_grp=1, rope_flags=(False, True), swap_kind="half32",
                    dim=D_QK, bq=min(512, seq), rc=ATTN_ROWS, window="full", tmode=None, name="attn_d",
                    **common)

    normed = _rmsnorm([_pieces(o, 0, gw) for o in (oa, ob, oc, od)], lw["g_grp"])
    return _matmul_ws(normed, w_out, l, d_model, res=x2, name="out_proj")


def _layer_weights(l, d_model, g_qk_abc, w_uq, w_ukv, g_cq, g_ckv, g_qk_mla, g_grp):
    heads = d_model // N_MIXERS // HEAD_DIM
    q_rank = w_uq.shape[1]
    uq = w_uq[l].reshape(q_rank, heads, D_QK)
    uq = jnp.pad(uq, ((0, 0), (0, 0), (0, 2 * LANE - D_QK))).reshape(q_rank, heads * 2 * LANE).astype(BF16)
    ukv = w_ukv[l].reshape(D_KV_RANK, heads, D_NOPE + HEAD_DIM)
    ukv = jnp.concatenate([ukv[:, :, :D_NOPE].reshape(D_KV_RANK, heads * D_NOPE),
                           ukv[:, :, D_NOPE:].reshape(D_KV_RANK, heads * HEAD_DIM)], axis=1).astype(BF16)
    pad_g = lambda g: jnp.pad(g, (0, 2 * LANE - D_QK))
    return dict(w_uq=uq, w_ukv=ukv, g_qk_abc=g_qk_abc[l], g_cq=g_cq[l], g_ckv=g_ckv[l],
                gq_mla=pad_g(g_qk_mla[l, 0]), gk_mla=pad_g(g_qk_mla[l, 1]), g_grp=g_grp[l])


def kernel(x, g_mix, w_in, g_qk_abc, w_uq, w_ukv, g_cq, g_ckv, g_qk_mla, rpb, g_grp, w_out, g_ffn, w_gate,
           w_up, w_down, w_router, we_gate, we_up, we_down):
    batch, seq, d_model = x.shape
    depth = g_mix.shape[0]
    n_exp = we_gate.shape[1]
    tabs = dict(rope=_rope_tables(seq), mult=_dilation_multiplicity(seq), nbr=_neighbourhood_tables(rpb, seq))
    x2 = x.reshape(batch * seq, d_model)
    w_in_t = jnp.swapaxes(w_in, 1, 2)
    for l in range(depth):
        lw = _layer_weights(l, d_model, g_qk_abc, w_uq, w_ukv, g_cq, g_ckv, g_qk_mla, g_grp)
        h = _rmsnorm([_pieces(x2, 0, d_model)], g_mix[l])
        x2 = _mixer(x2, h, l, w_in_t, w_out, lw, tabs, batch, seq)
        i = l // 2
        if l % 2 == 0:
            h = _rmsnorm([_pieces(x2, 0, d_model)], g_ffn[l])
            a = _gateup_dense(h, w_gate, w_up, i)
            x2 = _matmul(a, w_down[i].astype(BF16), res=x2, bk=w_down.shape[1] // 2, name="ffn_down")
        else:
            h3 = _rmsnorm([_pieces(x2, 0, d_model)], g_ffn[l], split_rows=True)
            wr = jnp.pad(w_router[i], ((0, 0), (0, LANE - n_exp))).astype(BF16)
            x2 = _moe(x2, h3, wr, we_gate, we_up, we_down, i)
    return x2.reshape(batch, seq, d_model)
```

```python
import functools
import math

import jax
import jax.numpy as jnp
from jax import lax
from jax.experimental import pallas as pl
from jax.experimental.pallas import tpu as pltpu

F32 = jnp.float32
BF16 = jnp.bfloat16

LANE = 128
HEAD_DIM = 128
N_MIXERS = 4
ROPE_THETA = 10000.0
EPS = 1e-6
NEG_INF = -1e30
LOG2_E = 1.4426950408889634
GRID_W = 64
A_PATTERNS = ((128, 1), (512, 4), (2048, 16))
C_WIN_R = 8
C_WIN_C = 16
C_QROWS = 8
C_KROWS = 16
ATTN_ROWS = 128
NBR_D_LO = (C_WIN_R - 1) - (C_KROWS - C_QROWS) - (C_QROWS - 1)
NBR_D_HI = (C_WIN_R - 1) + (C_KROWS - 2)
D_KV_RANK = 512
D_NOPE = 128
D_ROPE = 64
D_QK = D_NOPE + D_ROPE
TOP_K = 2
CAST_ROWS = 512
VMEM_LIMIT_BYTES = 56 * 1024 * 1024


def _blk(dim, target, mult=LANE):
    best = None
    d = mult
    while d <= min(dim, target):
        if dim % d == 0:
            best = d
        d += mult
    return best if best is not None else dim


def _params(n_grid):
    return pltpu.CompilerParams(dimension_semantics=("arbitrary",) * n_grid,
                                vmem_limit_bytes=VMEM_LIMIT_BYTES)


def _stage_bf16(w_ref, scr):
    rows = scr.shape[0]
    step = min(rows, CAST_ROWS)
    for r0 in range(0, rows, step):
        scr[r0:r0 + step, :] = w_ref[0, r0:r0 + step, :].astype(BF16)


def _rmsnorm_kernel(*refs, group_sizes, split_rows):
    n_in = sum(group_sizes)
    x_refs = refs[:n_in]
    g_ref = refs[n_in]
    o_ref = refs[n_in + 1]
    off = 0
    k = 0
    for size in group_sizes:
        pieces = [x_refs[k + p][...] for p in range(size)]
        k += size
        width = sum(p.shape[-1] for p in pieces)
        ssq = None
        for p in pieces:
            t = jnp.sum(p * p, axis=-1, keepdims=True)
            ssq = t if ssq is None else ssq + t
        r = lax.rsqrt(ssq / width + EPS)
        for p in pieces:
            w = p.shape[-1]
            y = (p * r) * g_ref[:, off:off + w]
            if split_rows:
                o_ref[...] = y.reshape(o_ref.shape)
            else:
                o_ref[:, off:off + w] = y.astype(o_ref.dtype)
            off += w


def _pieces(arr, col_off, width):
    pw = math.gcd(col_off, width) if col_off else width
    return [(arr, col_off // pw + p, pw) for p in range(width // pw)]


def _rmsnorm(groups, gain, bm=256, out_dtype=BF16, split_rows=False):
    flat = [p for g in groups for p in g]
    n = flat[0][0].shape[0]
    bm = _blk(n, bm, 8)
    total = sum(w for _, _, w in flat)
    in_specs = [pl.BlockSpec((bm, w), functools.partial(lambda i, c: (i, c), c=c)) for _, c, w in flat]
    in_specs.append(pl.BlockSpec((1, total), lambda i: (0, 0)))
    if split_rows:
        assert len(flat) == 1
        out_spec = pl.BlockSpec((bm, total // LANE, LANE), lambda i: (i, 0, 0))
        out_shape = jax.ShapeDtypeStruct((n, total // LANE, LANE), F32)
    else:
        out_spec = pl.BlockSpec((bm, total), lambda i: (i, 0))
        out_shape = jax.ShapeDtypeStruct((n, total), out_dtype)
    return pl.pallas_call(
        functools.partial(_rmsnorm_kernel, group_sizes=tuple(len(g) for g in groups), split_rows=split_rows),
        grid=(n // bm,),
        in_specs=in_specs,
        out_specs=out_spec,
        out_shape=out_shape,
        compiler_params=_params(1),
        name="rmsnorm",
    )(*[a for a, _, _ in flat], gain.reshape(1, total))


def _mm_kernel(a_ref, w_ref, *rest, nk, has_res):
    if has_res:
        r_ref, o_ref = rest
    else:
        (o_ref,) = rest
    part = jnp.dot(a_ref[...], w_ref[...], preferred_element_type=F32)
    if nk == 1:
        o_ref[...] = (r_ref[...] + part) if has_res else part
        return
    k = pl.program_id(2)

    @pl.when(k == 0)
    def _():
        o_ref[...] = (r_ref[...] + part) if has_res else part

    @pl.when(k > 0)
    def _():
        o_ref[...] += part


def _matmul(a, w, res=None, bm=1024, bn=512, bk=None, name="matmul"):
    m, kdim = a.shape
    n = w.shape[1]
    bm = _blk(m, bm, 8)
    bn = _blk(n, bn)
    bk = kdim if bk is None else _blk(kdim, bk)
    nk = kdim // bk
    in_specs = [pl.BlockSpec((bm, bk), lambda i, j, k: (i, k)),
                pl.BlockSpec((bk, bn), lambda i, j, k: (k, j))]
    args = [a, w]
    if res is not None:
        in_specs.append(pl.BlockSpec((bm, bn), lambda i, j, k: (i, j)))
        args.append(res)
    return pl.pallas_call(
        functools.partial(_mm_kernel, nk=nk, has_res=res is not None),
        grid=(m // bm, n // bn, nk),
        in_specs=in_specs,
        out_specs=pl.BlockSpec((bm, bn), lambda i, j, k: (i, j)),
        out_shape=jax.ShapeDtypeStruct((m, n), F32),
        compiler_params=_params(3),
        name=name,
    )(*args)


def _norm_mm_kernel(*refs, n_pieces):
    x_refs = refs[:n_pieces]
    g_ref, w_ref, o_ref = refs[n_pieces:]
    pieces = [r[...] for r in x_refs]
    width = sum(p.shape[-1] for p in pieces)
    ssq = None
    for p in pieces:
        t = jnp.sum(p * p, axis=-1, keepdims=True)
        ssq = t if ssq is None else ssq + t
    r = lax.rsqrt(ssq / width + EPS)
    acc = None
    off = 0
    for p in pieces:
        w = p.shape[-1]
        y = ((p * r) * g_ref[:, off:off + w]).astype(BF16)
        part = jnp.dot(y, w_ref[off:off + w, :], preferred_element_type=F32)
        acc = part if acc is None else acc + part
        off += w
    o_ref[...] = acc


def _norm_matmul(pieces, gain, w, bm=512, name="norm_matmul"):
    m = pieces[0][0].shape[0]
    kdim, n = w.shape
    bm = _blk(m, bm, 8)
    in_specs = [pl.BlockSpec((bm, pw), functools.partial(lambda i, c: (i, c), c=c)) for _, c, pw in pieces]
    in_specs += [pl.BlockSpec((1, kdim), lambda i: (0, 0)), pl.BlockSpec((kdim, n), lambda i: (0, 0))]
    return pl.pallas_call(
        functools.partial(_norm_mm_kernel, n_pieces=len(pieces)),
        grid=(m // bm,),
        in_specs=in_specs,
        out_specs=pl.BlockSpec((bm, n), lambda i: (i, 0)),
        out_shape=jax.ShapeDtypeStruct((m, n), F32),
        compiler_params=_params(1),
        name=name,
    )(*[a for a, _, _ in pieces], gain.reshape(1, kdim), w)


def _mm_ws_kernel(a_ref, w_ref, *rest, has_res, valid_cols, w_rows_are_cols):
    if has_res:
        r_ref, o_ref, w_scr = rest
    else:
        o_ref, w_scr = rest

    @pl.when(pl.program_id(1) == 0)
    def _():
        _stage_bf16(w_ref, w_scr)

    if w_rows_are_cols:
        part = lax.dot_general(a_ref[...], w_scr[...], (((1,), (1,)), ((), ())), preferred_element_type=F32)
    else:
        part = jnp.dot(a_ref[...], w_scr[...], preferred_element_type=F32)
    if valid_cols < part.shape[1]:
        lane = lax.broadcasted_iota(jnp.int32, part.shape, 1)
        part = jnp.where(lane < valid_cols, part, 0.0)
    o_ref[...] = (r_ref[...] + part) if has_res else part


def _matmul_ws(a, w3, layer, n_cols, res=None, bm=1024, bn=512, col0=0, w_rows_are_cols=False, name="matmul_ws"):
    m, kdim = a.shape
    bm = _blk(m, bm, 8)
    bn = _blk(n_cols, bn)
    blk0 = col0 // bn
    assert blk0 * bn == col0
    valid_cols = min(n_cols, w3.shape[1 if w_rows_are_cols else 2] - col0)
    assert valid_cols == n_cols or n_cols == bn
    if w_rows_are_cols:
        w_spec = pl.BlockSpec((1, bn, kdim), lambda j, i: (layer, blk0 + j, 0))
        w_scr = pltpu.VMEM((bn, kdim), BF16)
    else:
        w_spec = pl.BlockSpec((1, kdim, bn), lambda j, i: (layer, 0, blk0 + j))
        w_scr = pltpu.VMEM((kdim, bn), BF16)
    in_specs = [pl.BlockSpec((bm, kdim), lambda j, i: (i, 0)), w_spec]
    args = [a, w3]
    if res is not None:
        in_specs.append(pl.BlockSpec((bm, bn), lambda j, i: (i, j)))
        args.append(res)
    return pl.pallas_call(
        functools.partial(_mm_ws_kernel, has_res=res is not None, valid_cols=valid_cols,
                          w_rows_are_cols=w_rows_are_cols),
        grid=(n_cols // bn, m // bm),
        in_specs=in_specs,
        out_specs=pl.BlockSpec((bm, bn), lambda j, i: (i, j)),
        out_shape=jax.ShapeDtypeStruct((m, n_cols), F32),
        scratch_shapes=[w_scr],
        compiler_params=_params(2),
        name=name,
    )(*args)


def _swiglu(g, u):
    return (g / (1.0 + jnp.exp(-g))) * u


def _gateup_kernel(h_ref, wg_ref, wu_ref, o_ref, wg_scr, wu_scr):
    @pl.when(pl.program_id(1) == 0)
    def _():
        _stage_bf16(wg_ref, wg_scr)
        _stage_bf16(wu_ref, wu_scr)

    h = h_ref[...]
    g = jnp.dot(h, wg_scr[...], preferred_element_type=F32)
    u = jnp.dot(h, wu_scr[...], preferred_element_type=F32)
    o_ref[...] = _swiglu(g, u).astype(o_ref.dtype)


def _gateup_dense(h, wg3, wu3, layer, bm=1024, bn=256):
    m, d = h.shape
    f = wg3.shape[2]
    bm = _blk(m, bm, 8)
    bn = _blk(f, bn)
    w_spec = pl.BlockSpec((1, d, bn), lambda j, i: (layer, 0, j))
    return pl.pallas_call(
        _gateup_kernel,
        grid=(f // bn, m // bm),
        in_specs=[pl.BlockSpec((bm, d), lambda j, i: (i, 0)), w_spec, w_spec],
        out_specs=pl.BlockSpec((bm, bn), lambda j, i: (i, j)),
        out_shape=jax.ShapeDtypeStruct((m, f), BF16),
        scratch_shapes=[pltpu.VMEM((d, bn), BF16), pltpu.VMEM((d, bn), BF16)],
        compiler_params=_params(2),
        name="ffn_gateup",
    )(h, wg3, wu3)


ROUTE_IDX1, ROUTE_IDX2, ROUTE_G1, ROUTE_G2, ROUTE_RANK1, ROUTE_RANK2 = range(6)


def _router_kernel(h_ref, w_ref, route_ref, cnt_ref, carry, *, n_exp):
    @pl.when(pl.program_id(0) == 0)
    def _():
        carry[...] = jnp.zeros_like(carry)

    bm = h_ref.shape[0]
    h = h_ref[...].reshape(bm, h_ref.shape[1] * LANE).astype(BF16)
    logits = jnp.dot(h, w_ref[...], preferred_element_type=F32)
    lane = lax.broadcasted_iota(jnp.int32, logits.shape, 1).astype(F32)
    logits = jnp.where(lane < n_exp, logits, -jnp.inf)
    v1 = jnp.max(logits, axis=-1, keepdims=True)
    i1 = jnp.min(jnp.where(logits == v1, lane, float(LANE)), axis=-1, keepdims=True)
    rest = jnp.where(lane == i1, -jnp.inf, logits)
    v2 = jnp.max(rest, axis=-1, keepdims=True)
    i2 = jnp.min(jnp.where(rest == v2, lane, float(LANE)), axis=-1, keepdims=True)
    e2 = jnp.exp(v2 - v1)
    den = 1.0 + e2
    g1 = 1.0 / den
    g2 = e2 / den
    chosen = jnp.where((lane == i1) | (lane == i2), 1.0, 0.0)
    row = lax.broadcasted_iota(jnp.int32, (bm, bm), 0)
    col = lax.broadcasted_iota(jnp.int32, (bm, bm), 1)
    earlier = jnp.where(col < row, 1.0, 0.0).astype(BF16)
    before = jnp.dot(earlier, chosen.astype(BF16), preferred_element_type=F32) + carry[...]
    rank1 = jnp.sum(jnp.where(lane == i1, before, 0.0), axis=-1, keepdims=True)
    rank2 = jnp.sum(jnp.where(lane == i2, before, 0.0), axis=-1, keepdims=True)
    carry[...] += jnp.sum(chosen, axis=0, keepdims=True)
    route = jnp.zeros_like(logits)
    for pos, val in ((ROUTE_IDX1, i1), (ROUTE_IDX2, i2), (ROUTE_G1, g1), (ROUTE_G2, g2),
                     (ROUTE_RANK1, rank1), (ROUTE_RANK2, rank2)):
        route = jnp.where(lane == pos, val, route)
    route_ref[...] = route
    cnt_ref[...] = jnp.broadcast_to(carry[...], cnt_ref.shape)


def _router(h3, w_router_padded, n_exp, bm=512):
    m, dl, _ = h3.shape
    bm = _blk(m, bm, 8)
    return pl.pallas_call(
        functools.partial(_router_kernel, n_exp=n_exp),
        grid=(m // bm,),
        in_specs=[pl.BlockSpec((bm, dl, LANE), lambda i: (i, 0, 0)),
                  pl.BlockSpec((dl * LANE, LANE), lambda i: (0, 0))],
        out_specs=[pl.BlockSpec((bm, LANE), lambda i: (i, 0)),
                   pl.BlockSpec((8, LANE), lambda i: (0, 0))],
        out_shape=[jax.ShapeDtypeStruct((m, LANE), F32), jax.ShapeDtypeStruct((8, LANE), F32)],
        scratch_shapes=[pltpu.VMEM((1, LANE), F32)],
        compiler_params=_params(1),
        name="router",
    )(h3, w_router_padded)


def _gather_kernel(s1_ref, s2_ref, te_ref, offs_ref, cnt_ref, nu_ref, src_ref, o_ref, tok_scr, buf, sem, *,
                   tb, bm, n_tiles, n_tok):
    i = pl.program_id(0)

    @pl.when(i == 0)
    def _():
        def place(t, c):
            tok_scr[s1_ref[t]] = t
            tok_scr[s2_ref[t]] = t
            return c

        lax.fori_loop(0, n_tok, place, 0, unroll=8)

    def used(tile):
        return tile * tb < nu_ref[0] * bm

    def row_copy(r, slot, t):
        return pltpu.make_async_copy(src_ref.at[t], buf.at[slot, r], sem.at[slot])

    def issue(tile, slot):
        e = te_ref[(tile * tb) // bm]
        first = offs_ref[e]
        last = first + cnt_ref[e] - 1

        def body(r, c):
            s = tile * tb + r
            s = jnp.where(s <= last, s, first)
            row_copy(r, slot, tok_scr[s]).start()
            return c

        lax.fori_loop(0, tb, body, 0, unroll=8)

    @pl.when((i == 0) & used(0))
    def _():
        issue(0, 0)

    @pl.when((i + 1 < n_tiles) & used(i + 1))
    def _():
        issue(i + 1, (i + 1) % 2)

    slot = i % 2

    @pl.when(used(i))
    def _():
        def wait(r, c):
            row_copy(r, slot, 0).wait()
            return c

        lax.fori_loop(0, tb, wait, 0, unroll=8)
        o_ref[...] = buf[slot].reshape(o_ref.shape).astype(o_ref.dtype)

    @pl.when(jnp.logical_not(used(i)))
    def _():
        o_ref[...] = jnp.zeros_like(o_ref)


def _gather_rows(slot1, slot2, tile_expert, offs, counts, n_used, src3, n_slot, bm, tb=256):
    n_tok, dl, _ = src3.shape
    tb = min(tb, bm)
    n_tiles = n_slot // tb
    return pl.pallas_call(
        functools.partial(_gather_kernel, tb=tb, bm=bm, n_tiles=n_tiles, n_tok=n_tok),
        grid_spec=pltpu.PrefetchScalarGridSpec(
            num_scalar_prefetch=6,
            grid=(n_tiles,),
            in_specs=[pl.BlockSpec(memory_space=pl.ANY)],
            out_specs=pl.BlockSpec((tb, dl * LANE), lambda i, *_: (i, 0)),
            scratch_shapes=[pltpu.SMEM((n_slot,), jnp.int32), pltpu.VMEM((2, tb, dl, LANE), F32),
                            pltpu.SemaphoreType.DMA((2,))]),
        out_shape=jax.ShapeDtypeStruct((n_slot, dl * LANE), BF16),
        compiler_params=_params(1),
        name="gather_rows",
    )(slot1, slot2, tile_expert, offs, counts, n_used, src3)


def _new_expert(te_ref, i):
    return (i == 0) | (te_ref[i] != te_ref[jnp.maximum(i - 1, 0)])


def _expert_gateup_kernel(te_ref, nu_ref, x_ref, wg_ref, wu_ref, o_ref, wg_scr, wu_scr):
    i = pl.program_id(1)

    @pl.when(_new_expert(te_ref, i))
    def _():
        _stage_bf16(wg_ref, wg_scr)
        _stage_bf16(wu_ref, wu_scr)

    @pl.when(i < nu_ref[0])
    def _():
        x = x_ref[...]
        g = jnp.dot(x, wg_scr[...], preferred_element_type=F32)
        u = jnp.dot(x, wu_scr[...], preferred_element_type=F32)
        o_ref[...] = _swiglu(g, u).astype(o_ref.dtype)

    @pl.when(i >= nu_ref[0])
    def _():
        o_ref[...] = jnp.zeros_like(o_ref)


def _expert_gateup(tile_expert, n_used, xs, wg4, wu4, layer, bm, bn=512):
    p, d = xs.shape
    ef = wg4.shape[3]
    bn = _blk(ef, bn)
    w_spec = pl.BlockSpec((None, 1, d, bn), lambda j, i, te, nu: (layer, te[i], 0, j))
    return pl.pallas_call(
        _expert_gateup_kernel,
        grid_spec=pltpu.PrefetchScalarGridSpec(
            num_scalar_prefetch=2,
            grid=(ef // bn, p // bm),
            in_specs=[pl.BlockSpec((bm, d), lambda j, i, te, nu: (i, 0)), w_spec, w_spec],
            out_specs=pl.BlockSpec((bm, bn), lambda j, i, te, nu: (i, j)),
            scratch_shapes=[pltpu.VMEM((d, bn), BF16), pltpu.VMEM((d, bn), BF16)]),
        out_shape=jax.ShapeDtypeStruct((p, ef), BF16),
        compiler_params=_params(2),
        name="expert_gateup",
    )(tile_expert, n_used, xs, wg4, wu4)


def _expert_down_kernel(te_ref, nu_ref, a_ref, w_ref, o_ref, w_scr):
    i = pl.program_id(1)

    @pl.when(_new_expert(te_ref, i))
    def _():
        _stage_bf16(w_ref, w_scr)

    @pl.when(i < nu_ref[0])
    def _():
        y = jnp.dot(a_ref[...], w_scr[...], preferred_element_type=F32)
        o_ref[...] = y.reshape(o_ref.shape)

    @pl.when(i >= nu_ref[0])
    def _():
        o_ref[...] = jnp.zeros_like(o_ref)


def _expert_down(tile_expert, n_used, a, wd4, layer, bm, bn=1024):
    p, ef = a.shape
    d = wd4.shape[3]
    bn = _blk(d, bn, 8 * LANE)
    return pl.pallas_call(
        _expert_down_kernel,
        grid_spec=pltpu.PrefetchScalarGridSpec(
            num_scalar_prefetch=2,
            grid=(d // bn, p // bm),
            in_specs=[pl.BlockSpec((bm, ef), lambda j, i, te, nu: (i, 0)),
                      pl.BlockSpec((None, 1, ef, bn), lambda j, i, te, nu: (layer, te[i], 0, j))],
            out_specs=pl.BlockSpec((bm, bn // LANE, LANE), lambda j, i, te, nu: (i, j, 0)),
            scratch_shapes=[pltpu.VMEM((ef, bn), BF16)]),
        out_shape=jax.ShapeDtypeStruct((p, d // LANE, LANE), F32),
        compiler_params=_params(2),
        name="expert_down",
    )(tile_expert, n_used, a, wd4)


def _combine_kernel(s1_ref, s2_ref, y_ref, x_ref, route_ref, o_ref, buf, sem, *, tb, n_tiles):
    i = pl.program_id(0)

    def row_copy(k, r, slot, s):
        return pltpu.make_async_copy(y_ref.at[s], buf.at[slot, k, r], sem.at[slot])

    def issue(tile, slot):
        def body(r, c):
            t = tile * tb + r
            row_copy(0, r, slot, s1_ref[t]).start()
            row_copy(1, r, slot, s2_ref[t]).start()
            return c

        lax.fori_loop(0, tb, body, 0, unroll=4)

    @pl.when(i == 0)
    def _():
        issue(0, 0)

    @pl.when(i + 1 < n_tiles)
    def _():
        issue(i + 1, (i + 1) % 2)

    slot = i % 2

    def wait(r, c):
        row_copy(0, r, slot, 0).wait()
        row_copy(1, r, slot, 0).wait()
        return c

    lax.fori_loop(0, tb, wait, 0, unroll=4)
    route = route_ref[...]
    g1 = route[:, ROUTE_G1:ROUTE_G1 + 1]
    g2 = route[:, ROUTE_G2:ROUTE_G2 + 1]
    y1 = buf[slot, 0].reshape(o_ref.shape)
    y2 = buf[slot, 1].reshape(o_ref.shape)
    o_ref[...] = x_ref[...] + (g1 * y1 + g2 * y2)


def _combine(slot1, slot2, y3, x2, route, tb=128):
    n, d = x2.shape
    dl = d // LANE
    tb = _blk(n, tb, 8)
    n_tiles = n // tb
    return pl.pallas_call(
        functools.partial(_combine_kernel, tb=tb, n_tiles=n_tiles),
        grid_spec=pltpu.PrefetchScalarGridSpec(
            num_scalar_prefetch=2,
            grid=(n_tiles,),
            in_specs=[pl.BlockSpec(memory_space=pl.ANY),
                      pl.BlockSpec((tb, d), lambda i, s1, s2: (i, 0)),
                      pl.BlockSpec((tb, LANE), lambda i, s1, s2: (i, 0))],
            out_specs=pl.BlockSpec((tb, d), lambda i, s1, s2: (i, 0)),
            scratch_shapes=[pltpu.VMEM((2, 2, tb, dl, LANE), F32), pltpu.SemaphoreType.DMA((2,))]),
        out_shape=jax.ShapeDtypeStruct((n, d), F32),
        compiler_params=_params(1),
        name="expert_combine",
    )(slot1, slot2, y3, x2, route)


def _moe(x2, h3, w_router_padded, wg4, wu4, wd4, layer, bm=512):
    n = x2.shape[0]
    n_exp = wg4.shape[1]
    bm = _blk(n, bm, 8)
    route, cnt = _router(h3, w_router_padded, n_exp)
    counts = cnt[0, :n_exp].astype(jnp.int32)
    padded = ((counts + bm - 1) // bm) * bm
    ends = jnp.cumsum(padded)
    offs = ends - padded
    idx1 = route[:, ROUTE_IDX1].astype(jnp.int32)
    idx2 = route[:, ROUTE_IDX2].astype(jnp.int32)
    slot1 = offs[idx1] + route[:, ROUTE_RANK1].astype(jnp.int32)
    slot2 = offs[idx2] + route[:, ROUTE_RANK2].astype(jnp.int32)
    n_slot = n * TOP_K + n_exp * bm
    n_tiles = n_slot // bm
    tile_start = jnp.arange(n_tiles, dtype=jnp.int32) * bm
    tile_expert = jnp.minimum(jnp.sum((ends[None, :] <= tile_start[:, None]).astype(jnp.int32), axis=1), n_exp - 1)
    n_used = (ends[-1] // bm).reshape(1)
    xs = _gather_rows(slot1, slot2, tile_expert, offs, counts, n_used, h3, n_slot, bm)
    a = _expert_gateup(tile_expert, n_used, xs, wg4, wu4, layer, bm)
    y3 = _expert_down(tile_expert, n_used, a, wd4, layer, bm)
    return _combine(slot1, slot2, y3, x2, route)


def _swap(y, kind):
    if kind == "half64":
        return pltpu.roll(y, 64, 1)
    lane = lax.broadcasted_iota(jnp.int32, y.shape, 1)
    lo = pltpu.roll(y, 96, 1)
    hi = pltpu.roll(y, 32, 1)
    return jnp.where((lane & 63) < 32, lo, hi)


def _norm_rope(chunks, gains, cos, sin, rope_flags, swap_kind, dim, transpose=False):
    ssq = None
    for c in chunks:
        t = jnp.sum(c * c, axis=-1, keepdims=True)
        ssq = t if ssq is None else ssq + t
    r = lax.rsqrt(ssq / dim + EPS)
    outs = []
    for c, g, flag in zip(chunks, gains, rope_flags):
        y = (c * r) * g
        if flag:
            y = y * cos + _swap(y, swap_kind) * sin
        outs.append((y.T if transpose else y).astype(BF16))
    return outs


def _nbr_bias(t_ref, rm_ref, win_minus_q, r0, rc, kw):
    rows = []
    for jj in range(rc // GRID_W):
        j = r0 // GRID_W + jj
        slabs = []
        for c in range(kw // LANE):
            d = win_minus_q + (C_WIN_R - 1) + 2 * c - j
            slabs.append(t_ref[0, d - NBR_D_LO])
        rows.append(jnp.concatenate(slabs, axis=1) + rm_ref[0, j:j + 1, :])
    return rows[0] if len(rows) == 1 else jnp.concatenate(rows, axis=0)


def _attn_kernel(*refs, n_grp, n_chunk, rope_flags, swap_kind, dim, scale, bq, rc, seq, kw, window, tmode,
                 k_major):
    has_rope = any(rope_flags)
    it = iter(refs)
    q_ref = next(it)
    k_refs = [next(it) for _ in range(n_chunk)]
    v_ref = next(it)
    gq_ref = next(it)
    gk_ref = next(it)
    cos_ref = next(it) if has_rope else None
    sin_ref = next(it) if has_rope else None
    t_ref = next(it) if tmode else None
    rm_ref = next(it) if tmode == "nbr" else None
    o_ref = next(it)
    k_scr = next(it)
    v_scr = next(it)
    qb = pl.program_id(2)

    @pl.when(qb == 0)
    def _():
        step = min(seq, 512)
        gains = [gk_ref[:, c * LANE:(c + 1) * LANE] for c in range(n_chunk)]
        for r0 in range(0, seq, step):
            chunks = [kr[r0:r0 + step, :] for kr in k_refs]
            cos = cos_ref[r0:r0 + step, :] if has_rope else None
            sin = sin_ref[r0:r0 + step, :] if has_rope else None
            outs = _norm_rope(chunks, gains, cos, sin, rope_flags, swap_kind, dim, transpose=k_major)
            for c, o in enumerate(outs):
                if k_major:
                    k_scr[c * LANE:(c + 1) * LANE, r0:r0 + step] = o
                else:
                    k_scr[r0:r0 + step, c * LANE:(c + 1) * LANE] = o
            v_scr[r0:r0 + step, :LANE] = v_ref[r0:r0 + step, :].astype(BF16)
            v_scr[r0:r0 + step, LANE:] = jnp.ones((step, LANE), BF16)

    if window == "full":
        kwin = k_scr[...]
        vwin = v_scr[...]
    else:
        rows = seq // GRID_W
        start_row = jnp.clip(qb * C_QROWS - C_WIN_R // 2, 0, rows - kw // GRID_W)
        start = pl.multiple_of(start_row * GRID_W, GRID_W)
        kwin = k_scr[pl.ds(start, kw), :]
        vwin = v_scr[pl.ds(start, kw), :]

    row0 = pl.multiple_of(qb * bq, bq)
    gains = [gq_ref[:, c * LANE:(c + 1) * LANE] for c in range(n_chunk)]
    for r0 in range(0, bq, rc):
        cos = cos_ref[pl.ds(row0 + r0, rc), :] if has_rope else None
        sin = sin_ref[pl.ds(row0 + r0, rc), :] if has_rope else None
        for g in range(n_grp):
            base = g * n_chunk
            chunks = [q_ref[r0:r0 + rc, (base + c) * LANE:(base + c + 1) * LANE] for c in range(n_chunk)]
            outs = _norm_rope(chunks, gains, cos, sin, rope_flags, swap_kind, dim)
            q = outs[0] if n_chunk == 1 else jnp.concatenate(outs, axis=1)
            if k_major:
                s = jnp.dot(q, kwin, preferred_element_type=F32)
            else:
                s = lax.dot_general(q, kwin, (((1,), (1,)), ((), ())), preferred_element_type=F32)
            if tmode == "nbr":
                t = _nbr_bias(t_ref, rm_ref, start_row - qb * C_QROWS, r0, rc, kw)
                s = jnp.where(t > -1e29, s * scale + t, NEG_INF)
                p = jnp.exp(s - jnp.max(s, axis=-1, keepdims=True))
            else:
                if tmode == "mul":
                    t = t_ref[r0:r0 + rc, :]
                    s = jnp.where(t > 0.0, s, NEG_INF)
                p = jnp.exp2((s - jnp.max(s, axis=-1, keepdims=True)) * (scale * LOG2_E))
                if tmode == "mul":
                    p = p * t
            o = jnp.dot(p.astype(BF16), vwin, preferred_element_type=F32)
            o_ref[r0:r0 + rc, g * LANE:(g + 1) * LANE] = o[:, :LANE] / o[:, LANE:]


def _attention(q_arr, q_col0, k_arrs, v_arr, v_col0, gq, gk, cos, sin, table, *, batch, seq, n_kv, n_grp,
               rope_flags, swap_kind, dim, bq, rc, window, tmode, name):
    n_chunk = len(k_arrs)
    k_major = n_chunk > 1 and window == "full"
    nqb = seq // bq
    kw = seq if window == "full" else min(C_KROWS, seq // GRID_W) * GRID_W
    qw = n_grp * n_chunk * LANE
    q_blk0 = q_col0 * LANE // qw
    assert q_blk0 * qw == q_col0 * LANE
    in_specs = [pl.BlockSpec((bq, qw), lambda b, g, i: (b * nqb + i, q_blk0 + g))]
    args = [q_arr]
    for arr, col0, per_head in k_arrs:
        in_specs.append(pl.BlockSpec((seq, LANE), functools.partial(
            lambda b, g, i, col0, per_head: (b, col0 + g * per_head), col0=col0, per_head=per_head)))
        args.append(arr)
    in_specs.append(pl.BlockSpec((seq, LANE), lambda b, g, i: (b, v_col0 + g)))
    args.append(v_arr)
    in_specs.append(pl.BlockSpec((1, n_chunk * LANE), lambda b, g, i: (0, 0)))
    args.append(gq.reshape(1, n_chunk * LANE))
    in_specs.append(pl.BlockSpec((1, n_chunk * LANE), lambda b, g, i: (0, 0)))
    args.append(gk.reshape(1, n_chunk * LANE))
    if any(rope_flags):
        in_specs += [pl.BlockSpec((seq, LANE), lambda b, g, i: (0, 0))] * 2
        args += [cos, sin]
    if tmode == "nbr":
        slabs, layer, row_mask = table
        assert seq // GRID_W >= C_KROWS and 2 * GRID_W == LANE
        in_specs.append(pl.BlockSpec((None, 1) + slabs.shape[2:], lambda b, g, i: (layer, g, 0, 0, 0)))
        in_specs.append(pl.BlockSpec((1,) + row_mask.shape[1:], lambda b, g, i: (i, 0, 0)))
        args += [slabs, row_mask]
    elif tmode == "mul":
        in_specs.append(pl.BlockSpec((bq, kw), lambda b, g, i: (i, 0)))
        args.append(table)
    kern = functools.partial(
        _attn_kernel, n_grp=n_grp, n_chunk=n_chunk, rope_flags=rope_flags, swap_kind=swap_kind, dim=dim,
        scale=dim ** -0.5, bq=bq, rc=min(rc, bq), seq=seq, kw=kw, window=window, tmode=tmode,
        k_major=k_major)
    return pl.pallas_call(
        kern,
        name=name,
        grid=(batch, n_kv, nqb),
        in_specs=in_specs,
        out_specs=pl.BlockSpec((bq, n_grp * LANE), lambda b, g, i: (b * nqb + i, g)),
        out_shape=jax.ShapeDtypeStruct((batch * seq, n_kv * n_grp * LANE), F32),
        scratch_shapes=[pltpu.VMEM((n_chunk * LANE, seq) if k_major else (seq, n_chunk * LANE), BF16),
                        pltpu.VMEM((seq, 2 * LANE), BF16)],
        compiler_params=_params(3),
    )(*args)


def _rope_cs(pos, half):
    inv_freq = ROPE_THETA ** (-jnp.arange(half, dtype=F32) / half)
    ang = pos.astype(F32)[:, None] * inv_freq[None, :]
    return jnp.cos(ang), jnp.sin(ang)


def _rope_tables(seq):
    t = jnp.arange(seq)
    c, s = _rope_cs(t, HEAD_DIM // 2)
    a_cos, a_sin = jnp.concatenate([c, c], -1), jnp.concatenate([-s, s], -1)
    cr, sr = _rope_cs(t // GRID_W, HEAD_DIM // 4)
    cc, sc = _rope_cs(t % GRID_W, HEAD_DIM // 4)
    b_cos = jnp.concatenate([cr, cr, cc, cc], -1)
    b_sin = jnp.concatenate([-sr, sr, -sc, sc], -1)
    cd, sd = _rope_cs(t, D_ROPE // 2)
    z = jnp.zeros((seq, LANE - D_ROPE), F32)
    d_cos = jnp.concatenate([cd, cd, z], -1)
    d_sin = jnp.concatenate([-sd, sd, z], -1)
    return (a_cos, a_sin), (b_cos, b_sin), (d_cos, d_sin)


def _dilation_multiplicity(seq):
    t = jnp.arange(seq)
    delta = t[None, :] - t[:, None]
    mult = jnp.zeros((seq, seq), jnp.int32)
    for window, d in A_PATTERNS:
        half = window // (2 * d)
        mult = mult + ((delta % d == 0) & (jnp.abs(delta) <= half * d)).astype(jnp.int32)
    return mult.astype(F32)


def _neighbourhood_tables(rpb, seq):
    depth, heads = rpb.shape[:2]
    rows = seq // GRID_W
    wr = min(C_WIN_R, rows)
    kr = min(C_KROWS, rows)
    nqb = rows // C_QROWS
    c = jnp.arange(GRID_W)
    col_start = jnp.clip(c - C_WIN_C // 2, 0, GRID_W - C_WIN_C)
    col_ok = (c[None, :] >= col_start[:, None]) & (c[None, :] < col_start[:, None] + C_WIN_C)
    dcol = jnp.clip(c[None, :] - c[:, None] + (C_WIN_C - 1), 0, 2 * C_WIN_C - 2)
    by_col = jnp.take(rpb, dcol.reshape(-1), axis=3, mode="clip")
    by_col = by_col.reshape(depth, heads, 2 * C_WIN_R - 1, GRID_W, GRID_W)
    by_col = jnp.where(col_ok, by_col, NEG_INF)
    lo_pad = -NBR_D_LO
    hi_pad = NBR_D_HI + 1 - (2 * C_WIN_R - 2)
    by_col = jnp.pad(by_col, ((0, 0), (0, 0), (lo_pad, hi_pad), (0, 0), (0, 0)), constant_values=NEG_INF)
    slabs = jnp.concatenate([by_col[:, :, :-1], by_col[:, :, 1:]], axis=-1)

    qb = jnp.arange(nqb)
    r = qb[:, None] * C_QROWS + jnp.arange(C_QROWS)[None, :]
    win0 = jnp.clip(qb * C_QROWS - C_WIN_R // 2, 0, rows - kr)
    krow = win0[:, None] + jnp.arange(kr)[None, :]
    row_start = jnp.clip(r - wr // 2, 0, rows - wr)
    row_ok = (krow[:, None, :] >= row_start[:, :, None]) & (krow[:, None, :] < row_start[:, :, None] + wr)
    row_mask = jnp.where(row_ok, 0.0, NEG_INF).astype(F32)
    row_mask = jnp.repeat(row_mask, GRID_W, axis=2)
    row_mask = jnp.pad(row_mask, ((0, 0), (0, 8 - C_QROWS), (0, 0)))
    return slabs, row_mask


def _mixer(x2, h, l, w_in_t, w_out, lw, tabs, batch, seq):
    d_model = x2.shape[1]
    gw = d_model // N_MIXERS
    heads = gw // HEAD_DIM
    kv_heads = heads // 4
    nb = gw // LANE
    q_rank = lw["g_cq"].shape[0]
    (a_cos, a_sin), (b_cos, b_sin), (d_cos, d_sin) = tabs["rope"]

    col = {"qa": 0, "ka": nb, "va": 2 * nb, "qb": 3 * nb, "kb": 4 * nb, "vb": 4 * nb + kv_heads}
    col["qc"] = 4 * nb + 2 * kv_heads
    col["kc"] = col["qc"] + nb
    col["vc"] = col["kc"] + nb
    col["cq"] = col["vc"] + nb
    col["ckv"] = col["cq"] + q_rank // LANE
    n_main = (col["ckv"] + D_KV_RANK // LANE) * LANE
    proj = _matmul_ws(h, w_in_t, l, n_main, w_rows_are_cols=True, name="in_proj")
    kpe = _matmul_ws(h, w_in_t, l, LANE, bn=LANE, col0=n_main, w_rows_are_cols=True, name="kpe_proj")

    g_abc = lw["g_qk_abc"]
    common = dict(batch=batch, seq=seq)

    oa = _attention(proj, col["qa"], [(proj, col["ka"], 1)], proj, col["va"], g_abc[0, 0], g_abc[0, 1],
                    a_cos, a_sin, tabs["mult"], n_kv=heads, n_grp=1, rope_flags=(True,), swap_kind="half64",
                    dim=HEAD_DIM, bq=min(2048, seq), rc=ATTN_ROWS, window="full", tmode="mul", name="attn_a",
                    **common)
    ob = _attention(proj, col["qb"], [(proj, col["kb"], 1)], proj, col["vb"], g_abc[1, 0], g_abc[1, 1],
                    b_cos, b_sin, None, n_kv=kv_heads, n_grp=4, rope_flags=(True,), swap_kind="half32",
                    dim=HEAD_DIM, bq=min(1024, seq), rc=ATTN_ROWS, window="full", tmode=None, name="attn_b",
                    **common)
    oc = _attention(proj, col["qc"], [(proj, col["kc"], 1)], proj, col["vc"], g_abc[2, 0], g_abc[2, 1],
                    None, None, (tabs["nbr"][0], l, tabs["nbr"][1]), n_kv=heads, n_grp=1, rope_flags=(False,),
                    swap_kind=None, dim=HEAD_DIM, bq=C_QROWS * GRID_W, rc=ATTN_ROWS, window="rows", tmode="nbr",
                    name="attn_c", **common)

    qd = _norm_matmul(_pieces(proj, col["cq"] * LANE, q_rank), lw["g_cq"], lw["w_uq"], name="uq_proj")
    kvd = _norm_matmul(_pieces(proj, col["ckv"] * LANE, D_KV_RANK), lw["g_ckv"], lw["w_ukv"], name="ukv_proj")
    od = _attention(qd, 0, [(kvd, 0, 1), (kpe, 0, 0)], kvd, heads, lw["gq_mla"], lw["gk_mla"],
                    d_cos, d_sin, None, n_kv=heads, n_grp=1, rope_flags=(False, True), swap_kind="half32",
                    dim=D_QK, bq=min(2048, seq), rc=ATTN_ROWS, window="full", tmode=None, name="attn_d",
                    **common)

    normed = _rmsnorm([_pieces(o, 0, gw) for o in (oa, ob, oc, od)], lw["g_grp"])
    return _matmul_ws(normed, w_out, l, d_model, res=x2, name="out_proj")


def _layer_weights(l, d_model, g_qk_abc, w_uq, w_ukv, g_cq, g_ckv, g_qk_mla, g_grp):
    heads = d_model // N_MIXERS // HEAD_DIM
    q_rank = w_uq.shape[1]
    uq = w_uq[l].reshape(q_rank, heads, D_QK)
    uq = jnp.pad(uq, ((0, 0), (0, 0), (0, 2 * LANE - D_QK))).reshape(q_rank, heads * 2 * LANE).astype(BF16)
    ukv = w_ukv[l].reshape(D_KV_RANK, heads, D_NOPE + HEAD_DIM)
    ukv = jnp.concatenate([ukv[:, :, :D_NOPE].reshape(D_KV_RANK, heads * D_NOPE),
                           ukv[:, :, D_NOPE:].reshape(D_KV_RANK, heads * HEAD_DIM)], axis=1).astype(BF16)
    pad_g = lambda g: jnp.pad(g, (0, 2 * LANE - D_QK))
    return dict(w_uq=uq, w_ukv=ukv, g_qk_abc=g_qk_abc[l], g_cq=g_cq[l], g_ckv=g_ckv[l],
                gq_mla=pad_g(g_qk_mla[l, 0]), gk_mla=pad_g(g_qk_mla[l, 1]), g_grp=g_grp[l])


def kernel(x, g_mix, w_in, g_qk_abc, w_uq, w_ukv, g_cq, g_ckv, g_qk_mla, rpb, g_grp, w_out, g_ffn, w_gate,
           w_up, w_down, w_router, we_gate, we_up, we_down):
    batch, seq, d_model = x.shape
    depth = g_mix.shape[0]
    n_exp = we_gate.shape[1]
    tabs = dict(rope=_rope_tables(seq), mult=_dilation_multiplicity(seq), nbr=_neighbourhood_tables(rpb, seq))
    x2 = x.reshape(batch * seq, d_model)
    w_in_t = jnp.swapaxes(w_in, 1, 2)
    for l in range(depth):
        lw = _layer_weights(l, d_model, g_qk_abc, w_uq, w_ukv, g_cq, g_ckv, g_qk_mla, g_grp)
        h = _rmsnorm([_pieces(x2, 0, d_model)], g_mix[l])
        x2 = _mixer(x2, h, l, w_in_t, w_out, lw, tabs, batch, seq)
        i = l // 2
        if l % 2 == 0:
            h = _rmsnorm([_pieces(x2, 0, d_model)], g_ffn[l])
            a = _gateup_dense(h, w_gate, w_up, i)
            x2 = _matmul(a, w_down[i].astype(BF16), res=x2, bk=w_down.shape[1] // 2, name="ffn_down")
        else:
            h3 = _rmsnorm([_pieces(x2, 0, d_model)], g_ffn[l], split_rows=True)
            wr = jnp.pad(w_router[i], ((0, 0), (0, LANE - n_exp))).astype(BF16)
            x2 = _moe(x2, h3, wr, we_gate, we_up, we_down, i)
    return x2.reshape(batch, seq, d_model)
```

```python
import functools
import math

import jax
import jax.numpy as jnp
from jax import lax
from jax.experimental import pallas as pl
from jax.experimental.pallas import tpu as pltpu

F32 = jnp.float32
BF16 = jnp.bfloat16

LANE = 128
HEAD_DIM = 128
N_MIXERS = 4
ROPE_THETA = 10000.0
EPS = 1e-6
NEG_INF = -1e30
LOG2_E = 1.4426950408889634
GRID_W = 64
A_PATTERNS = ((128, 1), (512, 4), (2048, 16))
C_WIN_R = 8
C_WIN_C = 16
C_QROWS = 8
C_KROWS = 16
ATTN_ROWS = 128
NBR_D_LO = (C_WIN_R - 1) - (C_KROWS - C_QROWS) - (C_QROWS - 1)
NBR_D_HI = (C_WIN_R - 1) + (C_KROWS - 2)
D_KV_RANK = 512
D_NOPE = 128
D_ROPE = 64
D_QK = D_NOPE + D_ROPE
TOP_K = 2
CAST_ROWS = 512
VMEM_LIMIT_BYTES = 56 * 1024 * 1024


def _blk(dim, target, mult=LANE):
    best = None
    d = mult
    while d <= min(dim, target):
        if dim % d == 0:
            best = d
        d += mult
    return best if best is not None else dim


def _params(n_grid):
    return pltpu.CompilerParams(dimension_semantics=("arbitrary",) * n_grid,
                                vmem_limit_bytes=VMEM_LIMIT_BYTES)


def _stage_and_dot(a, w_ref, scr, w_rows_are_cols=False):
    rows = scr.shape[0]
    step = min(rows, CAST_ROWS)
    parts = []
    acc = None
    for r0 in range(0, rows, step):
        slab = w_ref[0, r0:r0 + step, :].astype(BF16)
        scr[r0:r0 + step, :] = slab
        if w_rows_are_cols:
            parts.append(lax.dot_general(a, slab, (((1,), (1,)), ((), ())), preferred_element_type=F32))
        else:
            part = jnp.dot(a[:, r0:r0 + step], slab, preferred_element_type=F32)
            acc = part if acc is None else acc + part
    if w_rows_are_cols:
        return parts[0] if len(parts) == 1 else jnp.concatenate(parts, axis=1)
    return acc


def _dot_staged(a, scr, w_rows_are_cols=False):
    if w_rows_are_cols:
        return lax.dot_general(a, scr[...], (((1,), (1,)), ((), ())), preferred_element_type=F32)
    return jnp.dot(a, scr[...], preferred_element_type=F32)


def _rmsnorm_kernel(*refs, group_sizes, split_rows):
    n_in = sum(group_sizes)
    x_refs = refs[:n_in]
    g_ref = refs[n_in]
    o_ref = refs[n_in + 1]
    off = 0
    k = 0
    for size in group_sizes:
        pieces = [x_refs[k + p][...] for p in range(size)]
        k += size
        width = sum(p.shape[-1] for p in pieces)
        ssq = None
        for p in pieces:
            t = jnp.sum(p * p, axis=-1, keepdims=True)
            ssq = t if ssq is None else ssq + t
        r = lax.rsqrt(ssq / width + EPS)
        for p in pieces:
            w = p.shape[-1]
            y = (p * r) * g_ref[:, off:off + w]
            if split_rows:
                o_ref[...] = y.reshape(o_ref.shape)
            else:
                o_ref[:, off:off + w] = y.astype(o_ref.dtype)
            off += w


def _pieces(arr, col_off, width):
    pw = math.gcd(col_off, width) if col_off else width
    return [(arr, col_off // pw + p, pw) for p in range(width // pw)]


def _rmsnorm(groups, gain, bm=256, out_dtype=BF16, split_rows=False):
    flat = [p for g in groups for p in g]
    n = flat[0][0].shape[0]
    bm = _blk(n, bm, 8)
    total = sum(w for _, _, w in flat)
    in_specs = [pl.BlockSpec((bm, w), functools.partial(lambda i, c: (i, c), c=c)) for _, c, w in flat]
    in_specs.append(pl.BlockSpec((1, total), lambda i: (0, 0)))
    if split_rows:
        assert len(flat) == 1
        out_spec = pl.BlockSpec((bm, total // LANE, LANE), lambda i: (i, 0, 0))
        out_shape = jax.ShapeDtypeStruct((n, total // LANE, LANE), F32)
    else:
        out_spec = pl.BlockSpec((bm, total), lambda i: (i, 0))
        out_shape = jax.ShapeDtypeStruct((n, total), out_dtype)
    return pl.pallas_call(
        functools.partial(_rmsnorm_kernel, group_sizes=tuple(len(g) for g in groups), split_rows=split_rows),
        grid=(n // bm,),
        in_specs=in_specs,
        out_specs=out_spec,
        out_shape=out_shape,
        compiler_params=_params(1),
        name="rmsnorm",
    )(*[a for a, _, _ in flat], gain.reshape(1, total))


def _mm_kernel(a_ref, w_ref, *rest, nk, has_res):
    if has_res:
        r_ref, o_ref = rest
    else:
        (o_ref,) = rest
    part = jnp.dot(a_ref[...], w_ref[...], preferred_element_type=F32)
    if nk == 1:
        o_ref[...] = (r_ref[...] + part) if has_res else part
        return
    k = pl.program_id(2)

    @pl.when(k == 0)
    def _():
        o_ref[...] = (r_ref[...] + part) if has_res else part

    @pl.when(k > 0)
    def _():
        o_ref[...] += part


def _matmul(a, w, res=None, bm=1024, bn=512, bk=None, name="matmul"):
    m, kdim = a.shape
    n = w.shape[1]
    bm = _blk(m, bm, 8)
    bn = _blk(n, bn)
    bk = kdim if bk is None else _blk(kdim, bk)
    nk = kdim // bk
    in_specs = [pl.BlockSpec((bm, bk), lambda i, j, k: (i, k)),
                pl.BlockSpec((bk, bn), lambda i, j, k: (k, j))]
    args = [a, w]
    if res is not None:
        in_specs.append(pl.BlockSpec((bm, bn), lambda i, j, k: (i, j)))
        args.append(res)
    return pl.pallas_call(
        functools.partial(_mm_kernel, nk=nk, has_res=res is not None),
        grid=(m // bm, n // bn, nk),
        in_specs=in_specs,
        out_specs=pl.BlockSpec((bm, bn), lambda i, j, k: (i, j)),
        out_shape=jax.ShapeDtypeStruct((m, n), F32),
        compiler_params=_params(3),
        name=name,
    )(*args)


def _norm_mm_kernel(*refs, n_pieces):
    x_refs = refs[:n_pieces]
    g_ref, w_ref, o_ref = refs[n_pieces:]
    pieces = [r[...] for r in x_refs]
    width = sum(p.shape[-1] for p in pieces)
    ssq = None
    for p in pieces:
        t = jnp.sum(p * p, axis=-1, keepdims=True)
        ssq = t if ssq is None else ssq + t
    r = lax.rsqrt(ssq / width + EPS)
    acc = None
    off = 0
    for p in pieces:
        w = p.shape[-1]
        y = ((p * r) * g_ref[:, off:off + w]).astype(BF16)
        part = jnp.dot(y, w_ref[off:off + w, :], preferred_element_type=F32)
        acc = part if acc is None else acc + part
        off += w
    o_ref[...] = acc


def _norm_matmul(pieces, gain, w, bm=512, name="norm_matmul"):
    m = pieces[0][0].shape[0]
    kdim, n = w.shape
    bm = _blk(m, bm, 8)
    in_specs = [pl.BlockSpec((bm, pw), functools.partial(lambda i, c: (i, c), c=c)) for _, c, pw in pieces]
    in_specs += [pl.BlockSpec((1, kdim), lambda i: (0, 0)), pl.BlockSpec((kdim, n), lambda i: (0, 0))]
    return pl.pallas_call(
        functools.partial(_norm_mm_kernel, n_pieces=len(pieces)),
        grid=(m // bm,),
        in_specs=in_specs,
        out_specs=pl.BlockSpec((bm, n), lambda i: (i, 0)),
        out_shape=jax.ShapeDtypeStruct((m, n), F32),
        compiler_params=_params(1),
        name=name,
    )(*[a for a, _, _ in pieces], gain.reshape(1, kdim), w)


def _mm_ws_kernel(a_ref, w_ref, *rest, has_res, valid_cols, w_rows_are_cols):
    if has_res:
        r_ref, o_ref, w_scr = rest
    else:
        o_ref, w_scr = rest

    def finish(part):
        if valid_cols < part.shape[1]:
            lane = lax.broadcasted_iota(jnp.int32, part.shape, 1)
            part = jnp.where(lane < valid_cols, part, 0.0)
        o_ref[...] = (r_ref[...] + part) if has_res else part

    first = pl.program_id(1) == 0

    @pl.when(first)
    def _():
        finish(_stage_and_dot(a_ref[...], w_ref, w_scr, w_rows_are_cols))

    @pl.when(jnp.logical_not(first))
    def _():
        finish(_dot_staged(a_ref[...], w_scr, w_rows_are_cols))


def _matmul_ws(a, w3, layer, n_cols, res=None, bm=1024, bn=512, col0=0, w_rows_are_cols=False, name="matmul_ws"):
    m, kdim = a.shape
    bm = _blk(m, bm, 8)
    bn = _blk(n_cols, bn)
    blk0 = col0 // bn
    assert blk0 * bn == col0
    valid_cols = min(n_cols, w3.shape[1 if w_rows_are_cols else 2] - col0)
    assert valid_cols == n_cols or n_cols == bn
    if w_rows_are_cols:
        w_spec = pl.BlockSpec((1, bn, kdim), lambda j, i: (layer, blk0 + j, 0))
        w_scr = pltpu.VMEM((bn, kdim), BF16)
    else:
        w_spec = pl.BlockSpec((1, kdim, bn), lambda j, i: (layer, 0, blk0 + j))
        w_scr = pltpu.VMEM((kdim, bn), BF16)
    in_specs = [pl.BlockSpec((bm, kdim), lambda j, i: (i, 0)), w_spec]
    args = [a, w3]
    if res is not None:
        in_specs.append(pl.BlockSpec((bm, bn), lambda j, i: (i, j)))
        args.append(res)
    return pl.pallas_call(
        functools.partial(_mm_ws_kernel, has_res=res is not None, valid_cols=valid_cols,
                          w_rows_are_cols=w_rows_are_cols),
        grid=(n_cols // bn, m // bm),
        in_specs=in_specs,
        out_specs=pl.BlockSpec((bm, bn), lambda j, i: (i, j)),
        out_shape=jax.ShapeDtypeStruct((m, n_cols), F32),
        scratch_shapes=[w_scr],
        compiler_params=_params(2),
        name=name,
    )(*args)


def _swiglu(g, u):
    return (g / (1.0 + jnp.exp(-g))) * u


def _gateup_kernel(h_ref, wg_ref, wu_ref, o_ref, wg_scr, wu_scr):
    first = pl.program_id(1) == 0

    @pl.when(first)
    def _():
        h = h_ref[...]
        g = _stage_and_dot(h, wg_ref, wg_scr)
        u = _stage_and_dot(h, wu_ref, wu_scr)
        o_ref[...] = _swiglu(g, u).astype(o_ref.dtype)

    @pl.when(jnp.logical_not(first))
    def _():
        h = h_ref[...]
        o_ref[...] = _swiglu(_dot_staged(h, wg_scr), _dot_staged(h, wu_scr)).astype(o_ref.dtype)


def _gateup_dense(h, wg3, wu3, layer, bm=1024, bn=256):
    m, d = h.shape
    f = wg3.shape[2]
    bm = _blk(m, bm, 8)
    bn = _blk(f, bn)
    w_spec = pl.BlockSpec((1, d, bn), lambda j, i: (layer, 0, j))
    return pl.pallas_call(
        _gateup_kernel,
        grid=(f // bn, m // bm),
        in_specs=[pl.BlockSpec((bm, d), lambda j, i: (i, 0)), w_spec, w_spec],
        out_specs=pl.BlockSpec((bm, bn), lambda j, i: (i, j)),
        out_shape=jax.ShapeDtypeStruct((m, f), BF16),
        scratch_shapes=[pltpu.VMEM((d, bn), BF16), pltpu.VMEM((d, bn), BF16)],
        compiler_params=_params(2),
        name="ffn_gateup",
    )(h, wg3, wu3)


ROUTE_IDX1, ROUTE_IDX2, ROUTE_G1, ROUTE_G2, ROUTE_RANK1, ROUTE_RANK2 = range(6)


def _router_kernel(h_ref, w_ref, route_ref, cnt_ref, carry, *, n_exp):
    @pl.when(pl.program_id(0) == 0)
    def _():
        carry[...] = jnp.zeros_like(carry)

    bm = h_ref.shape[0]
    h = h_ref[...].reshape(bm, h_ref.shape[1] * LANE).astype(BF16)
    logits = jnp.dot(h, w_ref[...], preferred_element_type=F32)
    lane = lax.broadcasted_iota(jnp.int32, logits.shape, 1).astype(F32)
    logits = jnp.where(lane < n_exp, logits, -jnp.inf)
    v1 = jnp.max(logits, axis=-1, keepdims=True)
    i1 = jnp.min(jnp.where(logits == v1, lane, float(LANE)), axis=-1, keepdims=True)
    rest = jnp.where(lane == i1, -jnp.inf, logits)
    v2 = jnp.max(rest, axis=-1, keepdims=True)
    i2 = jnp.min(jnp.where(rest == v2, lane, float(LANE)), axis=-1, keepdims=True)
    e2 = jnp.exp(v2 - v1)
    den = 1.0 + e2
    g1 = 1.0 / den
    g2 = e2 / den
    chosen = jnp.where((lane == i1) | (lane == i2), 1.0, 0.0)
    row = lax.broadcasted_iota(jnp.int32, (bm, bm), 0)
    col = lax.broadcasted_iota(jnp.int32, (bm, bm), 1)
    earlier = jnp.where(col < row, 1.0, 0.0).astype(BF16)
    before = jnp.dot(earlier, chosen.astype(BF16), preferred_element_type=F32) + carry[...]
    rank1 = jnp.sum(jnp.where(lane == i1, before, 0.0), axis=-1, keepdims=True)
    rank2 = jnp.sum(jnp.where(lane == i2, before, 0.0), axis=-1, keepdims=True)
    carry[...] += jnp.sum(chosen, axis=0, keepdims=True)
    route = jnp.zeros_like(logits)
    for pos, val in ((ROUTE_IDX1, i1), (ROUTE_IDX2, i2), (ROUTE_G1, g1), (ROUTE_G2, g2),
                     (ROUTE_RANK1, rank1), (ROUTE_RANK2, rank2)):
        route = jnp.where(lane == pos, val, route)
    route_ref[...] = route
    cnt_ref[...] = jnp.broadcast_to(carry[...], cnt_ref.shape)


def _router(h3, w_router_padded, n_exp, bm=512):
    m, dl, _ = h3.shape
    bm = _blk(m, bm, 8)
    return pl.pallas_call(
        functools.partial(_router_kernel, n_exp=n_exp),
        grid=(m // bm,),
        in_specs=[pl.BlockSpec((bm, dl, LANE), lambda i: (i, 0, 0)),
                  pl.BlockSpec((dl * LANE, LANE), lambda i: (0, 0))],
        out_specs=[pl.BlockSpec((bm, LANE), lambda i: (i, 0)),
                   pl.BlockSpec((8, LANE), lambda i: (0, 0))],
        out_shape=[jax.ShapeDtypeStruct((m, LANE), F32), jax.ShapeDtypeStruct((8, LANE), F32)],
        scratch_shapes=[pltpu.VMEM((1, LANE), F32)],
        compiler_params=_params(1),
        name="router",
    )(h3, w_router_padded)


def _gather_kernel(s1_ref, s2_ref, te_ref, offs_ref, cnt_ref, nu_ref, src_ref, o_ref, tok_scr, buf, sem, *,
                   tb, bm, n_tiles, n_tok):
    i = pl.program_id(0)

    @pl.when(i == 0)
    def _():
        def place(t, c):
            tok_scr[s1_ref[t]] = t
            tok_scr[s2_ref[t]] = t
            return c

        lax.fori_loop(0, n_tok, place, 0, unroll=8)

    def used(tile):
        return tile * tb < nu_ref[0] * bm

    def row_copy(r, slot, t):
        return pltpu.make_async_copy(src_ref.at[t], buf.at[slot, r], sem.at[slot])

    def issue(tile, slot):
        e = te_ref[(tile * tb) // bm]
        first = offs_ref[e]
        last = first + cnt_ref[e] - 1

        def body(r, c):
            s = tile * tb + r
            s = jnp.where(s <= last, s, first)
            row_copy(r, slot, tok_scr[s]).start()
            return c

        lax.fori_loop(0, tb, body, 0, unroll=8)

    @pl.when((i == 0) & used(0))
    def _():
        issue(0, 0)

    @pl.when((i + 1 < n_tiles) & used(i + 1))
    def _():
        issue(i + 1, (i + 1) % 2)

    slot = i % 2

    @pl.when(used(i))
    def _():
        def wait(r, c):
            row_copy(r, slot, 0).wait()
            return c

        lax.fori_loop(0, tb, wait, 0, unroll=8)
        o_ref[...] = buf[slot].reshape(o_ref.shape).astype(o_ref.dtype)

    @pl.when(jnp.logical_not(used(i)))
    def _():
        o_ref[...] = jnp.zeros_like(o_ref)


def _gather_rows(slot1, slot2, tile_expert, offs, counts, n_used, src3, n_slot, bm, tb=256):
    n_tok, dl, _ = src3.shape
    tb = min(tb, bm)
    n_tiles = n_slot // tb
    return pl.pallas_call(
        functools.partial(_gather_kernel, tb=tb, bm=bm, n_tiles=n_tiles, n_tok=n_tok),
        grid_spec=pltpu.PrefetchScalarGridSpec(
            num_scalar_prefetch=6,
            grid=(n_tiles,),
            in_specs=[pl.BlockSpec(memory_space=pl.ANY)],
            out_specs=pl.BlockSpec((tb, dl * LANE), lambda i, *_: (i, 0)),
            scratch_shapes=[pltpu.SMEM((n_slot,), jnp.int32), pltpu.VMEM((2, tb, dl, LANE), F32),
                            pltpu.SemaphoreType.DMA((2,))]),
        out_shape=jax.ShapeDtypeStruct((n_slot, dl * LANE), BF16),
        compiler_params=_params(1),
        name="gather_rows",
    )(slot1, slot2, tile_expert, offs, counts, n_used, src3)


def _new_expert(te_ref, i):
    return (i == 0) | (te_ref[i] != te_ref[jnp.maximum(i - 1, 0)])


def _expert_gateup_kernel(te_ref, nu_ref, x_ref, wg_ref, wu_ref, o_ref, wg_scr, wu_scr):
    i = pl.program_id(1)
    used = i < nu_ref[0]
    fresh = _new_expert(te_ref, i)

    @pl.when(used & fresh)
    def _():
        x = x_ref[...]
        g = _stage_and_dot(x, wg_ref, wg_scr)
        u = _stage_and_dot(x, wu_ref, wu_scr)
        o_ref[...] = _swiglu(g, u).astype(o_ref.dtype)

    @pl.when(used & jnp.logical_not(fresh))
    def _():
        x = x_ref[...]
        o_ref[...] = _swiglu(_dot_staged(x, wg_scr), _dot_staged(x, wu_scr)).astype(o_ref.dtype)

    @pl.when(jnp.logical_not(used))
    def _():
        o_ref[...] = jnp.zeros_like(o_ref)


def _expert_gateup(tile_expert, n_used, xs, wg4, wu4, layer, bm, bn=512):
    p, d = xs.shape
    ef = wg4.shape[3]
    bn = _blk(ef, bn)
    w_spec = pl.BlockSpec((None, 1, d, bn), lambda j, i, te, nu: (layer, te[i], 0, j))
    return pl.pallas_call(
        _expert_gateup_kernel,
        grid_spec=pltpu.PrefetchScalarGridSpec(
            num_scalar_prefetch=2,
            grid=(ef // bn, p // bm),
            in_specs=[pl.BlockSpec((bm, d), lambda j, i, te, nu: (i, 0)), w_spec, w_spec],
            out_specs=pl.BlockSpec((bm, bn), lambda j, i, te, nu: (i, j)),
            scratch_shapes=[pltpu.VMEM((d, bn), BF16), pltpu.VMEM((d, bn), BF16)]),
        out_shape=jax.ShapeDtypeStruct((p, ef), BF16),
        compiler_params=_params(2),
        name="expert_gateup",
    )(tile_expert, n_used, xs, wg4, wu4)


def _expert_down_kernel(te_ref, nu_ref, a_ref, w_ref, o_ref, w_scr):
    i = pl.program_id(1)
    used = i < nu_ref[0]
    fresh = _new_expert(te_ref, i)

    @pl.when(used & fresh)
    def _():
        o_ref[...] = _stage_and_dot(a_ref[...], w_ref, w_scr).reshape(o_ref.shape)

    @pl.when(used & jnp.logical_not(fresh))
    def _():
        o_ref[...] = _dot_staged(a_ref[...], w_scr).reshape(o_ref.shape)

    @pl.when(jnp.logical_not(used))
    def _():
        o_ref[...] = jnp.zeros_like(o_ref)


def _expert_down(tile_expert, n_used, a, wd4, layer, bm, bn=1024):
    p, ef = a.shape
    d = wd4.shape[3]
    bn = _blk(d, bn, 8 * LANE)
    return pl.pallas_call(
        _expert_down_kernel,
        grid_spec=pltpu.PrefetchScalarGridSpec(
            num_scalar_prefetch=2,
            grid=(d // bn, p // bm),
            in_specs=[pl.BlockSpec((bm, ef), lambda j, i, te, nu: (i, 0)),
                      pl.BlockSpec((None, 1, ef, bn), lambda j, i, te, nu: (layer, te[i], 0, j))],
            out_specs=pl.BlockSpec((bm, bn // LANE, LANE), lambda j, i, te, nu: (i, j, 0)),
            scratch_shapes=[pltpu.VMEM((ef, bn), BF16)]),
        out_shape=jax.ShapeDtypeStruct((p, d // LANE, LANE), F32),
        compiler_params=_params(2),
        name="expert_down",
    )(tile_expert, n_used, a, wd4)


def _combine_kernel(s1_ref, s2_ref, y_ref, x_ref, route_ref, o_ref, buf, sem, *, tb, n_tiles):
    i = pl.program_id(0)

    def row_copy(k, r, slot, s):
        return pltpu.make_async_copy(y_ref.at[s], buf.at[slot, k, r], sem.at[slot])

    def issue(tile, slot):
        def body(r, c):
            t = tile * tb + r
            row_copy(0, r, slot, s1_ref[t]).start()
            row_copy(1, r, slot, s2_ref[t]).start()
            return c

        lax.fori_loop(0, tb, body, 0, unroll=4)

    @pl.when(i == 0)
    def _():
        issue(0, 0)

    @pl.when(i + 1 < n_tiles)
    def _():
        issue(i + 1, (i + 1) % 2)

    slot = i % 2

    def wait(r, c):
        row_copy(0, r, slot, 0).wait()
        row_copy(1, r, slot, 0).wait()
        return c

    lax.fori_loop(0, tb, wait, 0, unroll=4)
    route = route_ref[...]
    g1 = route[:, ROUTE_G1:ROUTE_G1 + 1]
    g2 = route[:, ROUTE_G2:ROUTE_G2 + 1]
    y1 = buf[slot, 0].reshape(o_ref.shape)
    y2 = buf[slot, 1].reshape(o_ref.shape)
    o_ref[...] = x_ref[...] + (g1 * y1 + g2 * y2)


def _combine(slot1, slot2, y3, x2, route, tb=128):
    n, d = x2.shape
    dl = d // LANE
    tb = _blk(n, tb, 8)
    n_tiles = n // tb
    return pl.pallas_call(
        functools.partial(_combine_kernel, tb=tb, n_tiles=n_tiles),
        grid_spec=pltpu.PrefetchScalarGridSpec(
            num_scalar_prefetch=2,
            grid=(n_tiles,),
            in_specs=[pl.BlockSpec(memory_space=pl.ANY),
                      pl.BlockSpec((tb, d), lambda i, s1, s2: (i, 0)),
                      pl.BlockSpec((tb, LANE), lambda i, s1, s2: (i, 0))],
            out_specs=pl.BlockSpec((tb, d), lambda i, s1, s2: (i, 0)),
            scratch_shapes=[pltpu.VMEM((2, 2, tb, dl, LANE), F32), pltpu.SemaphoreType.DMA((2,))]),
        out_shape=jax.ShapeDtypeStruct((n, d), F32),
        compiler_params=_params(1),
        name="expert_combine",
    )(slot1, slot2, y3, x2, route)


def _moe(x2, h3, w_router_padded, wg4, wu4, wd4, layer, bm=512):
    n = x2.shape[0]
    n_exp = wg4.shape[1]
    bm = _blk(n, bm, 8)
    route, cnt = _router(h3, w_router_padded, n_exp)
    counts = cnt[0, :n_exp].astype(jnp.int32)
    padded = ((counts + bm - 1) // bm) * bm
    ends = jnp.cumsum(padded)
    offs = ends - padded
    idx1 = route[:, ROUTE_IDX1].astype(jnp.int32)
    idx2 = route[:, ROUTE_IDX2].astype(jnp.int32)
    slot1 = offs[idx1] + route[:, ROUTE_RANK1].astype(jnp.int32)
    slot2 = offs[idx2] + route[:, ROUTE_RANK2].astype(jnp.int32)
    n_slot = n * TOP_K + n_exp * bm
    n_tiles = n_slot // bm
    tile_start = jnp.arange(n_tiles, dtype=jnp.int32) * bm
    tile_expert = jnp.minimum(jnp.sum((ends[None, :] <= tile_start[:, None]).astype(jnp.int32), axis=1), n_exp - 1)
    n_used = (ends[-1] // bm).reshape(1)
    xs = _gather_rows(slot1, slot2, tile_expert, offs, counts, n_used, h3, n_slot, bm)
    a = _expert_gateup(tile_expert, n_used, xs, wg4, wu4, layer, bm)
    y3 = _expert_down(tile_expert, n_used, a, wd4, layer, bm)
    return _combine(slot1, slot2, y3, x2, route)


def _swap(y, kind):
    if kind == "half64":
        return pltpu.roll(y, 64, 1)
    lane = lax.broadcasted_iota(jnp.int32, y.shape, 1)
    lo = pltpu.roll(y, 96, 1)
    hi = pltpu.roll(y, 32, 1)
    return jnp.where((lane & 63) < 32, lo, hi)


def _norm_rope(chunks, gains, cos, sin, rope_flags, swap_kind, dim, transpose=False):
    ssq = None
    for c in chunks:
        t = jnp.sum(c * c, axis=-1, keepdims=True)
        ssq = t if ssq is None else ssq + t
    r = lax.rsqrt(ssq / dim + EPS)
    outs = []
    for c, g, flag in zip(chunks, gains, rope_flags):
        y = (c * r) * g
        if flag:
            y = y * cos + _swap(y, swap_kind) * sin
        outs.append((y.T if transpose else y).astype(BF16))
    return outs


def _nbr_bias(t_ref, rm_ref, win_minus_q, r0, rc, kw):
    rows = []
    for jj in range(rc // GRID_W):
        j = r0 // GRID_W + jj
        slabs = []
        for c in range(kw // LANE):
            d = win_minus_q + (C_WIN_R - 1) + 2 * c - j
            slabs.append(t_ref[0, d - NBR_D_LO])
        rows.append(jnp.concatenate(slabs, axis=1) + rm_ref[0, j:j + 1, :])
    return rows[0] if len(rows) == 1 else jnp.concatenate(rows, axis=0)


def _attn_kernel(*refs, n_grp, n_chunk, rope_flags, swap_kind, dim, scale, bq, rc, seq, kw, window, tmode,
                 k_major):
    has_rope = any(rope_flags)
    it = iter(refs)
    q_ref = next(it)
    k_refs = [next(it) for _ in range(n_chunk)]
    v_ref = next(it)
    gq_ref = next(it)
    gk_ref = next(it)
    cos_ref = next(it) if has_rope else None
    sin_ref = next(it) if has_rope else None
    t_ref = next(it) if tmode else None
    rm_ref = next(it) if tmode == "nbr" else None
    o_ref = next(it)
    k_scr = next(it)
    v_scr = next(it)
    qb = pl.program_id(2)

    @pl.when(qb == 0)
    def _():
        step = min(seq, 512)
        gains = [gk_ref[:, c * LANE:(c + 1) * LANE] for c in range(n_chunk)]
        for r0 in range(0, seq, step):
            chunks = [kr[r0:r0 + step, :] for kr in k_refs]
            cos = cos_ref[r0:r0 + step, :] if has_rope else None
            sin = sin_ref[r0:r0 + step, :] if has_rope else None
            outs = _norm_rope(chunks, gains, cos, sin, rope_flags, swap_kind, dim, transpose=k_major)
            for c, o in enumerate(outs):
                if k_major:
                    k_scr[c * LANE:(c + 1) * LANE, r0:r0 + step] = o
                else:
                    k_scr[r0:r0 + step, c * LANE:(c + 1) * LANE] = o
            v_scr[r0:r0 + step, :LANE] = v_ref[r0:r0 + step, :].astype(BF16)
            v_scr[r0:r0 + step, LANE:] = jnp.ones((step, LANE), BF16)

    if window == "full":
        kwin = k_scr[...]
        vwin = v_scr[...]
    else:
        rows = seq // GRID_W
        start_row = jnp.clip(qb * C_QROWS - C_WIN_R // 2, 0, rows - kw // GRID_W)
        start = pl.multiple_of(start_row * GRID_W, GRID_W)
        kwin = k_scr[pl.ds(start, kw), :]
        vwin = v_scr[pl.ds(start, kw), :]

    row0 = pl.multiple_of(qb * bq, bq)
    gains = [gq_ref[:, c * LANE:(c + 1) * LANE] for c in range(n_chunk)]
    for r0 in range(0, bq, rc):
        cos = cos_ref[pl.ds(row0 + r0, rc), :] if has_rope else None
        sin = sin_ref[pl.ds(row0 + r0, rc), :] if has_rope else None
        for g in range(n_grp):
            base = g * n_chunk
            chunks = [q_ref[r0:r0 + rc, (base + c) * LANE:(base + c + 1) * LANE] for c in range(n_chunk)]
            outs = _norm_rope(chunks, gains, cos, sin, rope_flags, swap_kind, dim)
            q = outs[0] if n_chunk == 1 else jnp.concatenate(outs, axis=1)
            if k_major:
                s = jnp.dot(q, kwin, preferred_element_type=F32)
            else:
                s = lax.dot_general(q, kwin, (((1,), (1,)), ((), ())), preferred_element_type=F32)
            if tmode == "nbr":
                t = _nbr_bias(t_ref, rm_ref, start_row - qb * C_QROWS, r0, rc, kw)
                s = jnp.where(t > -1e29, s * scale + t, NEG_INF)
                p = jnp.exp(s - jnp.max(s, axis=-1, keepdims=True))
            else:
                if tmode == "mul":
                    t = t_ref[r0:r0 + rc, :]
                    s = jnp.where(t > 0.0, s, NEG_INF)
                p = jnp.exp2((s - jnp.max(s, axis=-1, keepdims=True)) * (scale * LOG2_E))
                if tmode == "mul":
                    p = p * t
            o = jnp.dot(p.astype(BF16), vwin, preferred_element_type=F32)
            o_ref[r0:r0 + rc, g * LANE:(g + 1) * LANE] = o[:, :LANE] / o[:, LANE:]


def _attention(q_arr, q_col0, k_arrs, v_arr, v_col0, gq, gk, cos, sin, table, *, batch, seq, n_kv, n_grp,
               rope_flags, swap_kind, dim, bq, rc, window, tmode, name):
    n_chunk = len(k_arrs)
    k_major = n_chunk > 1 and window == "full"
    nqb = seq // bq
    kw = seq if window == "full" else min(C_KROWS, seq // GRID_W) * GRID_W
    qw = n_grp * n_chunk * LANE
    q_blk0 = q_col0 * LANE // qw
    assert q_blk0 * qw == q_col0 * LANE
    in_specs = [pl.BlockSpec((bq, qw), lambda b, g, i: (b * nqb + i, q_blk0 + g))]
    args = [q_arr]
    for arr, col0, per_head in k_arrs:
        in_specs.append(pl.BlockSpec((seq, LANE), functools.partial(
            lambda b, g, i, col0, per_head: (b, col0 + g * per_head), col0=col0, per_head=per_head)))
        args.append(arr)
    in_specs.append(pl.BlockSpec((seq, LANE), lambda b, g, i: (b, v_col0 + g)))
    args.append(v_arr)
    in_specs.append(pl.BlockSpec((1, n_chunk * LANE), lambda b, g, i: (0, 0)))
    args.append(gq.reshape(1, n_chunk * LANE))
    in_specs.append(pl.BlockSpec((1, n_chunk * LANE), lambda b, g, i: (0, 0)))
    args.append(gk.reshape(1, n_chunk * LANE))
    if any(rope_flags):
        in_specs += [pl.BlockSpec((seq, LANE), lambda b, g, i: (0, 0))] * 2
        args += [cos, sin]
    if tmode == "nbr":
        slabs, layer, row_mask = table
        assert seq // GRID_W >= C_KROWS and 2 * GRID_W == LANE
        in_specs.append(pl.BlockSpec((None, 1) + slabs.shape[2:], lambda b, g, i: (layer, g, 0, 0, 0)))
        in_specs.append(pl.BlockSpec((1,) + row_mask.shape[1:], lambda b, g, i: (i, 0, 0)))
        args += [slabs, row_mask]
    elif tmode == "mul":
        in_specs.append(pl.BlockSpec((bq, kw), lambda b, g, i: (i, 0)))
        args.append(table)
    kern = functools.partial(
        _attn_kernel, n_grp=n_grp, n_chunk=n_chunk, rope_flags=rope_flags, swap_kind=swap_kind, dim=dim,
        scale=dim ** -0.5, bq=bq, rc=min(rc, bq), seq=seq, kw=kw, window=window, tmode=tmode,
        k_major=k_major)
    return pl.pallas_call(
        kern,
        name=name,
        grid=(batch, n_kv, nqb),
        in_specs=in_specs,
        out_specs=pl.BlockSpec((bq, n_grp * LANE), lambda b, g, i: (b * nqb + i, g)),
        out_shape=jax.ShapeDtypeStruct((batch * seq, n_kv * n_grp * LANE), F32),
        scratch_shapes=[pltpu.VMEM((n_chunk * LANE, seq) if k_major else (seq, n_chunk * LANE), BF16),
                        pltpu.VMEM((seq, 2 * LANE), BF16)],
        compiler_params=_params(3),
    )(*args)


def _rope_cs(pos, half):
    inv_freq = ROPE_THETA ** (-jnp.arange(half, dtype=F32) / half)
    ang = pos.astype(F32)[:, None] * inv_freq[None, :]
    return jnp.cos(ang), jnp.sin(ang)


def _rope_tables(seq):
    t = jnp.arange(seq)
    c, s = _rope_cs(t, HEAD_DIM // 2)
    a_cos, a_sin = jnp.concatenate([c, c], -1), jnp.concatenate([-s, s], -1)
    cr, sr = _rope_cs(t // GRID_W, HEAD_DIM // 4)
    cc, sc = _rope_cs(t % GRID_W, HEAD_DIM // 4)
    b_cos = jnp.concatenate([cr, cr, cc, cc], -1)
    b_sin = jnp.concatenate([-sr, sr, -sc, sc], -1)
    cd, sd = _rope_cs(t, D_ROPE // 2)
    z = jnp.zeros((seq, LANE - D_ROPE), F32)
    d_cos = jnp.concatenate([cd, cd, z], -1)
    d_sin = jnp.concatenate([-sd, sd, z], -1)
    return (a_cos, a_sin), (b_cos, b_sin), (d_cos, d_sin)


def _dilation_multiplicity(seq):
    t = jnp.arange(seq)
    delta = t[None, :] - t[:, None]
    mult = jnp.zeros((seq, seq), jnp.int32)
    for window, d in A_PATTERNS:
        half = window // (2 * d)
        mult = mult + ((delta % d == 0) & (jnp.abs(delta) <= half * d)).astype(jnp.int32)
    return mult.astype(F32)


def _neighbourhood_tables(rpb, seq):
    depth, heads = rpb.shape[:2]
    rows = seq // GRID_W
    wr = min(C_WIN_R, rows)
    kr = min(C_KROWS, rows)
    nqb = rows // C_QROWS
    c = jnp.arange(GRID_W)
    col_start = jnp.clip(c - C_WIN_C // 2, 0, GRID_W - C_WIN_C)
    col_ok = (c[None, :] >= col_start[:, None]) & (c[None, :] < col_start[:, None] + C_WIN_C)
    dcol = jnp.clip(c[None, :] - c[:, None] + (C_WIN_C - 1), 0, 2 * C_WIN_C - 2)
    by_col = jnp.take(rpb, dcol.reshape(-1), axis=3, mode="clip")
    by_col = by_col.reshape(depth, heads, 2 * C_WIN_R - 1, GRID_W, GRID_W)
    by_col = jnp.where(col_ok, by_col, NEG_INF)
    lo_pad = -NBR_D_LO
    hi_pad = NBR_D_HI + 1 - (2 * C_WIN_R - 2)
    by_col = jnp.pad(by_col, ((0, 0), (0, 0), (lo_pad, hi_pad), (0, 0), (0, 0)), constant_values=NEG_INF)
    slabs = jnp.concatenate([by_col[:, :, :-1], by_col[:, :, 1:]], axis=-1)

    qb = jnp.arange(nqb)
    r = qb[:, None] * C_QROWS + jnp.arange(C_QROWS)[None, :]
    win0 = jnp.clip(qb * C_QROWS - C_WIN_R // 2, 0, rows - kr)
    krow = win0[:, None] + jnp.arange(kr)[None, :]
    row_start = jnp.clip(r - wr // 2, 0, rows - wr)
    row_ok = (krow[:, None, :] >= row_start[:, :, None]) & (krow[:, None, :] < row_start[:, :, None] + wr)
    row_mask = jnp.where(row_ok, 0.0, NEG_INF).astype(F32)
    row_mask = jnp.repeat(row_mask, GRID_W, axis=2)
    row_mask = jnp.pad(row_mask, ((0, 0), (0, 8 - C_QROWS), (0, 0)))
    return slabs, row_mask


def _mixer(x2, h, l, w_in_t, w_out, lw, tabs, batch, seq):
    d_model = x2.shape[1]
    gw = d_model // N_MIXERS
    heads = gw // HEAD_DIM
    kv_heads = heads // 4
    nb = gw // LANE
    q_rank = lw["g_cq"].shape[0]
    (a_cos, a_sin), (b_cos, b_sin), (d_cos, d_sin) = tabs["rope"]

    col = {"qa": 0, "ka": nb, "va": 2 * nb, "qb": 3 * nb, "kb": 4 * nb, "vb": 4 * nb + kv_heads}
    col["qc"] = 4 * nb + 2 * kv_heads
    col["kc"] = col["qc"] + nb
    col["vc"] = col["kc"] + nb
    col["cq"] = col["vc"] + nb
    col["ckv"] = col["cq"] + q_rank // LANE
    n_main = (col["ckv"] + D_KV_RANK // LANE) * LANE
    proj = _matmul_ws(h, w_in_t, l, n_main, w_rows_are_cols=True, name="in_proj")
    kpe = _matmul_ws(h, w_in_t, l, LANE, bn=LANE, col0=n_main, w_rows_are_cols=True, name="kpe_proj")

    g_abc = lw["g_qk_abc"]
    common = dict(batch=batch, seq=seq)

    oa = _attention(proj, col["qa"], [(proj, col["ka"], 1)], proj, col["va"], g_abc[0, 0], g_abc[0, 1],
                    a_cos, a_sin, tabs["mult"], n_kv=heads, n_grp=1, rope_flags=(True,), swap_kind="half64",
                    dim=HEAD_DIM, bq=min(2048, seq), rc=ATTN_ROWS, window="full", tmode="mul", name="attn_a",
                    **common)
    ob = _attention(proj, col["qb"], [(proj, col["kb"], 1)], proj, col["vb"], g_abc[1, 0], g_abc[1, 1],
                    b_cos, b_sin, None, n_kv=kv_heads, n_grp=4, rope_flags=(True,), swap_kind="half32",
                    dim=HEAD_DIM, bq=min(1024, seq), rc=ATTN_ROWS, window="full", tmode=None, name="attn_b",
                    **common)
    oc = _attention(proj, col["qc"], [(proj, col["kc"], 1)], proj, col["vc"], g_abc[2, 0], g_abc[2, 1],
                    None, None, (tabs["nbr"][0], l, tabs["nbr"][1]), n_kv=heads, n_grp=1, rope_flags=(False,),
                    swap_kind=None, dim=HEAD_DIM, bq=C_QROWS * GRID_W, rc=ATTN_ROWS, window="rows", tmode="nbr",
                    name="attn_c", **common)

    qd = _norm_matmul(_pieces(proj, col["cq"] * LANE, q_rank), lw["g_cq"], lw["w_uq"], name="uq_proj")
    kvd = _norm_matmul(_pieces(proj, col["ckv"] * LANE, D_KV_RANK), lw["g_ckv"], lw["w_ukv"], name="ukv_proj")
    od = _attention(qd, 0, [(kvd, 0, 1), (kpe, 0, 0)], kvd, heads, lw["gq_mla"], lw["gk_mla"],
                    d_cos, d_sin, None, n_kv=heads, n_grp=1, rope_flags=(False, True), swap_kind="half32",
                    dim=D_QK, bq=min(2048, seq), rc=ATTN_ROWS, window="full", tmode=None, name="attn_d",
                    **common)

    normed = _rmsnorm([_pieces(o, 0, gw) for o in (oa, ob, oc, od)], lw["g_grp"])
    return _matmul_ws(normed, w_out, l, d_model, res=x2, name="out_proj")


def _layer_weights(l, d_model, g_qk_abc, w_uq, w_ukv, g_cq, g_ckv, g_qk_mla, g_grp):
    heads = d_model // N_MIXERS // HEAD_DIM
    q_rank = w_uq.shape[1]
    uq = w_uq[l].reshape(q_rank, heads, D_QK)
    uq = jnp.pad(uq, ((0, 0), (0, 0), (0, 2 * LANE - D_QK))).reshape(q_rank, heads * 2 * LANE).astype(BF16)
    ukv = w_ukv[l].reshape(D_KV_RANK, heads, D_NOPE + HEAD_DIM)
    ukv = jnp.concatenate([ukv[:, :, :D_NOPE].reshape(D_KV_RANK, heads * D_NOPE),
                           ukv[:, :, D_NOPE:].reshape(D_KV_RANK, heads * HEAD_DIM)], axis=1).astype(BF16)
    pad_g = lambda g: jnp.pad(g, (0, 2 * LANE - D_QK))
    return dict(w_uq=uq, w_ukv=ukv, g_qk_abc=g_qk_abc[l], g_cq=g_cq[l], g_ckv=g_ckv[l],
                gq_mla=pad_g(g_qk_mla[l, 0]), gk_mla=pad_g(g_qk_mla[l, 1]), g_grp=g_grp[l])


def kernel(x, g_mix, w_in, g_qk_abc, w_uq, w_ukv, g_cq, g_ckv, g_qk_mla, rpb, g_grp, w_out, g_ffn, w_gate,
           w_up, w_down, w_router, we_gate, we_up, we_down):
    batch, seq, d_model = x.shape
    depth = g_mix.shape[0]
    n_exp = we_gate.shape[1]
    tabs = dict(rope=_rope_tables(seq), mult=_dilation_multiplicity(seq), nbr=_neighbourhood_tables(rpb, seq))
    x2 = x.reshape(batch * seq, d_model)
    w_in_t = jnp.swapaxes(w_in, 1, 2)
    for l in range(depth):
        lw = _layer_weights(l, d_model, g_qk_abc, w_uq, w_ukv, g_cq, g_ckv, g_qk_mla, g_grp)
        h = _rmsnorm([_pieces(x2, 0, d_model)], g_mix[l])
        x2 = _mixer(x2, h, l, w_in_t, w_out, lw, tabs, batch, seq)
        i = l // 2
        if l % 2 == 0:
            h = _rmsnorm([_pieces(x2, 0, d_model)], g_ffn[l])
            a = _gateup_dense(h, w_gate, w_up, i)
            x2 = _matmul(a, w_down[i].astype(BF16), res=x2, bk=w_down.shape[1] // 2, name="ffn_down")
        else:
            h3 = _rmsnorm([_pieces(x2, 0, d_model)], g_ffn[l], split_rows=True)
            wr = jnp.pad(w_router[i], ((0, 0), (0, LANE - n_exp))).astype(BF16)
            x2 = _moe(x2, h3, wr, we_gate, we_up, we_down, i)
    return x2.reshape(batch, seq, d_model)
```

```python
import functools
import math

import jax
import jax.numpy as jnp
from jax import lax
from jax.experimental import pallas as pl
from jax.experimental.pallas import tpu as pltpu

F32 = jnp.float32
BF16 = jnp.bfloat16

LANE = 128
HEAD_DIM = 128
N_MIXERS = 4
ROPE_THETA = 10000.0
EPS = 1e-6
NEG_INF = -1e30
LOG2_E = 1.4426950408889634
GRID_W = 64
A_PATTERNS = ((128, 1), (512, 4), (2048, 16))
C_WIN_R = 8
C_WIN_C = 16
C_QROWS = 2
C_KROWS = 10
ATTN_ROWS = 128
NBR_D_LO = (C_WIN_R - 1) - (C_KROWS - C_QROWS) - (C_QROWS - 1)
NBR_D_HI = (C_WIN_R - 1) + (C_KROWS - 2)
D_KV_RANK = 512
D_NOPE = 128
D_ROPE = 64
D_QK = D_NOPE + D_ROPE
TOP_K = 2
CAST_ROWS = 512
VMEM_LIMIT_BYTES = 56 * 1024 * 1024


def _blk(dim, target, mult=LANE):
    best = None
    d = mult
    while d <= min(dim, target):
        if dim % d == 0:
            best = d
        d += mult
    return best if best is not None else dim


def _params(n_grid):
    return pltpu.CompilerParams(dimension_semantics=("arbitrary",) * n_grid,
                                vmem_limit_bytes=VMEM_LIMIT_BYTES)


def _stage_and_dot(a, w_ref, scr, w_rows_are_cols=False):
    rows = scr.shape[0]
    step = min(rows, CAST_ROWS)
    parts = []
    acc = None
    for r0 in range(0, rows, step):
        slab = w_ref[0, r0:r0 + step, :].astype(BF16)
        scr[r0:r0 + step, :] = slab
        if w_rows_are_cols:
            parts.append(lax.dot_general(a, slab, (((1,), (1,)), ((), ())), preferred_element_type=F32))
        else:
            part = jnp.dot(a[:, r0:r0 + step], slab, preferred_element_type=F32)
            acc = part if acc is None else acc + part
    if w_rows_are_cols:
        return parts[0] if len(parts) == 1 else jnp.concatenate(parts, axis=1)
    return acc


def _dot_staged(a, scr, w_rows_are_cols=False):
    if w_rows_are_cols:
        return lax.dot_general(a, scr[...], (((1,), (1,)), ((), ())), preferred_element_type=F32)
    return jnp.dot(a, scr[...], preferred_element_type=F32)


def _rmsnorm_kernel(*refs, group_sizes, split_rows):
    n_in = sum(group_sizes)
    x_refs = refs[:n_in]
    g_ref = refs[n_in]
    o_ref = refs[n_in + 1]
    off = 0
    k = 0
    for size in group_sizes:
        pieces = [x_refs[k + p][...] for p in range(size)]
        k += size
        width = sum(p.shape[-1] for p in pieces)
        ssq = None
        for p in pieces:
            t = jnp.sum(p * p, axis=-1, keepdims=True)
            ssq = t if ssq is None else ssq + t
        r = lax.rsqrt(ssq / width + EPS)
        for p in pieces:
            w = p.shape[-1]
            y = (p * r) * g_ref[:, off:off + w]
            if split_rows:
                o_ref[...] = y.reshape(o_ref.shape)
            else:
                o_ref[:, off:off + w] = y.astype(o_ref.dtype)
            off += w


def _pieces(arr, col_off, width):
    pw = math.gcd(col_off, width) if col_off else width
    return [(arr, col_off // pw + p, pw) for p in range(width // pw)]


def _rmsnorm(groups, gain, bm=256, out_dtype=BF16, split_rows=False):
    flat = [p for g in groups for p in g]
    n = flat[0][0].shape[0]
    bm = _blk(n, bm, 8)
    total = sum(w for _, _, w in flat)
    in_specs = [pl.BlockSpec((bm, w), functools.partial(lambda i, c: (i, c), c=c)) for _, c, w in flat]
    in_specs.append(pl.BlockSpec((1, total), lambda i: (0, 0)))
    if split_rows:
        assert len(flat) == 1
        out_spec = pl.BlockSpec((bm, total // LANE, LANE), lambda i: (i, 0, 0))
        out_shape = jax.ShapeDtypeStruct((n, total // LANE, LANE), F32)
    else:
        out_spec = pl.BlockSpec((bm, total), lambda i: (i, 0))
        out_shape = jax.ShapeDtypeStruct((n, total), out_dtype)
    return pl.pallas_call(
        functools.partial(_rmsnorm_kernel, group_sizes=tuple(len(g) for g in groups), split_rows=split_rows),
        grid=(n // bm,),
        in_specs=in_specs,
        out_specs=out_spec,
        out_shape=out_shape,
        compiler_params=_params(1),
        name="rmsnorm",
    )(*[a for a, _, _ in flat], gain.reshape(1, total))


def _mm_kernel(a_ref, w_ref, *rest, nk, has_res):
    if has_res:
        r_ref, o_ref = rest
    else:
        (o_ref,) = rest
    part = jnp.dot(a_ref[...], w_ref[...], preferred_element_type=F32)
    if nk == 1:
        o_ref[...] = (r_ref[...] + part) if has_res else part
        return
    k = pl.program_id(2)

    @pl.when(k == 0)
    def _():
        o_ref[...] = (r_ref[...] + part) if has_res else part

    @pl.when(k > 0)
    def _():
        o_ref[...] += part


def _matmul(a, w, res=None, bm=1024, bn=512, bk=None, name="matmul"):
    m, kdim = a.shape
    n = w.shape[1]
    bm = _blk(m, bm, 8)
    bn = _blk(n, bn)
    bk = kdim if bk is None else _blk(kdim, bk)
    nk = kdim // bk
    in_specs = [pl.BlockSpec((bm, bk), lambda i, j, k: (i, k)),
                pl.BlockSpec((bk, bn), lambda i, j, k: (k, j))]
    args = [a, w]
    if res is not None:
        in_specs.append(pl.BlockSpec((bm, bn), lambda i, j, k: (i, j)))
        args.append(res)
    return pl.pallas_call(
        functools.partial(_mm_kernel, nk=nk, has_res=res is not None),
        grid=(m // bm, n // bn, nk),
        in_specs=in_specs,
        out_specs=pl.BlockSpec((bm, bn), lambda i, j, k: (i, j)),
        out_shape=jax.ShapeDtypeStruct((m, n), F32),
        compiler_params=_params(3),
        name=name,
    )(*args)


def _norm_mm_kernel(*refs, n_pieces):
    x_refs = refs[:n_pieces]
    g_ref, w_ref, o_ref = refs[n_pieces:]
    pieces = [r[...] for r in x_refs]
    width = sum(p.shape[-1] for p in pieces)
    ssq = None
    for p in pieces:
        t = jnp.sum(p * p, axis=-1, keepdims=True)
        ssq = t if ssq is None else ssq + t
    r = lax.rsqrt(ssq / width + EPS)
    acc = None
    off = 0
    for p in pieces:
        w = p.shape[-1]
        y = ((p * r) * g_ref[:, off:off + w]).astype(BF16)
        part = jnp.dot(y, w_ref[off:off + w, :], preferred_element_type=F32)
        acc = part if acc is None else acc + part
        off += w
    o_ref[...] = acc


def _norm_matmul(pieces, gain, w, bm=512, name="norm_matmul"):
    m = pieces[0][0].shape[0]
    kdim, n = w.shape
    bm = _blk(m, bm, 8)
    in_specs = [pl.BlockSpec((bm, pw), functools.partial(lambda i, c: (i, c), c=c)) for _, c, pw in pieces]
    in_specs += [pl.BlockSpec((1, kdim), lambda i: (0, 0)), pl.BlockSpec((kdim, n), lambda i: (0, 0))]
    return pl.pallas_call(
        functools.partial(_norm_mm_kernel, n_pieces=len(pieces)),
        grid=(m // bm,),
        in_specs=in_specs,
        out_specs=pl.BlockSpec((bm, n), lambda i: (i, 0)),
        out_shape=jax.ShapeDtypeStruct((m, n), F32),
        compiler_params=_params(1),
        name=name,
    )(*[a for a, _, _ in pieces], gain.reshape(1, kdim), w)


def _mm_ws_kernel(a_ref, w_ref, *rest, has_res, valid_cols, w_rows_are_cols):
    if has_res:
        r_ref, o_ref, w_scr = rest
    else:
        o_ref, w_scr = rest

    def finish(part):
        if valid_cols < part.shape[1]:
            lane = lax.broadcasted_iota(jnp.int32, part.shape, 1)
            part = jnp.where(lane < valid_cols, part, 0.0)
        o_ref[...] = (r_ref[...] + part) if has_res else part

    first = pl.program_id(1) == 0

    @pl.when(first)
    def _():
        finish(_stage_and_dot(a_ref[...], w_ref, w_scr, w_rows_are_cols))

    @pl.when(jnp.logical_not(first))
    def _():
        finish(_dot_staged(a_ref[...], w_scr, w_rows_are_cols))


def _matmul_ws(a, w3, layer, n_cols, res=None, bm=1024, bn=512, col0=0, w_rows_are_cols=False, name="matmul_ws"):
    m, kdim = a.shape
    bm = _blk(m, bm, 8)
    bn = _blk(n_cols, bn)
    blk0 = col0 // bn
    assert blk0 * bn == col0
    valid_cols = min(n_cols, w3.shape[1 if w_rows_are_cols else 2] - col0)
    assert valid_cols == n_cols or n_cols == bn
    if w_rows_are_cols:
        w_spec = pl.BlockSpec((1, bn, kdim), lambda j, i: (layer, blk0 + j, 0))
        w_scr = pltpu.VMEM((bn, kdim), BF16)
    else:
        w_spec = pl.BlockSpec((1, kdim, bn), lambda j, i: (layer, 0, blk0 + j))
        w_scr = pltpu.VMEM((kdim, bn), BF16)
    in_specs = [pl.BlockSpec((bm, kdim), lambda j, i: (i, 0)), w_spec]
    args = [a, w3]
    if res is not None:
        in_specs.append(pl.BlockSpec((bm, bn), lambda j, i: (i, j)))
        args.append(res)
    return pl.pallas_call(
        functools.partial(_mm_ws_kernel, has_res=res is not None, valid_cols=valid_cols,
                          w_rows_are_cols=w_rows_are_cols),
        grid=(n_cols // bn, m // bm),
        in_specs=in_specs,
        out_specs=pl.BlockSpec((bm, bn), lambda j, i: (i, j)),
        out_shape=jax.ShapeDtypeStruct((m, n_cols), F32),
        scratch_shapes=[w_scr],
        compiler_params=_params(2),
        name=name,
    )(*args)


def _swiglu(g, u):
    return (g / (1.0 + jnp.exp(-g))) * u


def _gateup_kernel(h_ref, wg_ref, wu_ref, o_ref, wg_scr, wu_scr):
    first = pl.program_id(1) == 0

    @pl.when(first)
    def _():
        h = h_ref[...]
        g = _stage_and_dot(h, wg_ref, wg_scr)
        u = _stage_and_dot(h, wu_ref, wu_scr)
        o_ref[...] = _swiglu(g, u).astype(o_ref.dtype)

    @pl.when(jnp.logical_not(first))
    def _():
        h = h_ref[...]
        o_ref[...] = _swiglu(_dot_staged(h, wg_scr), _dot_staged(h, wu_scr)).astype(o_ref.dtype)


def _gateup_dense(h, wg3, wu3, layer, bm=1024, bn=256):
    m, d = h.shape
    f = wg3.shape[2]
    bm = _blk(m, bm, 8)
    bn = _blk(f, bn)
    w_spec = pl.BlockSpec((1, d, bn), lambda j, i: (layer, 0, j))
    return pl.pallas_call(
        _gateup_kernel,
        grid=(f // bn, m // bm),
        in_specs=[pl.BlockSpec((bm, d), lambda j, i: (i, 0)), w_spec, w_spec],
        out_specs=pl.BlockSpec((bm, bn), lambda j, i: (i, j)),
        out_shape=jax.ShapeDtypeStruct((m, f), BF16),
        scratch_shapes=[pltpu.VMEM((d, bn), BF16), pltpu.VMEM((d, bn), BF16)],
        compiler_params=_params(2),
        name="ffn_gateup",
    )(h, wg3, wu3)


ROUTE_IDX1, ROUTE_IDX2, ROUTE_G1, ROUTE_G2, ROUTE_RANK1, ROUTE_RANK2 = range(6)


def _router_kernel(h_ref, w_ref, route_ref, cnt_ref, carry, *, n_exp):
    @pl.when(pl.program_id(0) == 0)
    def _():
        carry[...] = jnp.zeros_like(carry)

    bm = h_ref.shape[0]
    h = h_ref[...].reshape(bm, h_ref.shape[1] * LANE).astype(BF16)
    logits = jnp.dot(h, w_ref[...], preferred_element_type=F32)
    lane = lax.broadcasted_iota(jnp.int32, logits.shape, 1).astype(F32)
    logits = jnp.where(lane < n_exp, logits, -jnp.inf)
    v1 = jnp.max(logits, axis=-1, keepdims=True)
    i1 = jnp.min(jnp.where(logits == v1, lane, float(LANE)), axis=-1, keepdims=True)
    rest = jnp.where(lane == i1, -jnp.inf, logits)
    v2 = jnp.max(rest, axis=-1, keepdims=True)
    i2 = jnp.min(jnp.where(rest == v2, lane, float(LANE)), axis=-1, keepdims=True)
    e2 = jnp.exp(v2 - v1)
    den = 1.0 + e2
    g1 = 1.0 / den
    g2 = e2 / den
    chosen = jnp.where((lane == i1) | (lane == i2), 1.0, 0.0)
    row = lax.broadcasted_iota(jnp.int32, (bm, bm), 0)
    col = lax.broadcasted_iota(jnp.int32, (bm, bm), 1)
    earlier = jnp.where(col < row, 1.0, 0.0).astype(BF16)
    before = jnp.dot(earlier, chosen.astype(BF16), preferred_element_type=F32) + carry[...]
    rank1 = jnp.sum(jnp.where(lane == i1, before, 0.0), axis=-1, keepdims=True)
    rank2 = jnp.sum(jnp.where(lane == i2, before, 0.0), axis=-1, keepdims=True)
    carry[...] += jnp.sum(chosen, axis=0, keepdims=True)
    route = jnp.zeros_like(logits)
    for pos, val in ((ROUTE_IDX1, i1), (ROUTE_IDX2, i2), (ROUTE_G1, g1), (ROUTE_G2, g2),
                     (ROUTE_RANK1, rank1), (ROUTE_RANK2, rank2)):
        route = jnp.where(lane == pos, val, route)
    route_ref[...] = route
    cnt_ref[...] = jnp.broadcast_to(carry[...], cnt_ref.shape)


def _router(h3, w_router_padded, n_exp, bm=512):
    m, dl, _ = h3.shape
    bm = _blk(m, bm, 8)
    return pl.pallas_call(
        functools.partial(_router_kernel, n_exp=n_exp),
        grid=(m // bm,),
        in_specs=[pl.BlockSpec((bm, dl, LANE), lambda i: (i, 0, 0)),
                  pl.BlockSpec((dl * LANE, LANE), lambda i: (0, 0))],
        out_specs=[pl.BlockSpec((bm, LANE), lambda i: (i, 0)),
                   pl.BlockSpec((8, LANE), lambda i: (0, 0))],
        out_shape=[jax.ShapeDtypeStruct((m, LANE), F32), jax.ShapeDtypeStruct((8, LANE), F32)],
        scratch_shapes=[pltpu.VMEM((1, LANE), F32)],
        compiler_params=_params(1),
        name="router",
    )(h3, w_router_padded)


def _gather_kernel(s1_ref, s2_ref, te_ref, offs_ref, cnt_ref, nu_ref, src_ref, o_ref, tok_scr, buf, sem, *,
                   tb, bm, n_tiles, n_tok):
    i = pl.program_id(0)

    @pl.when(i == 0)
    def _():
        def place(t, c):
            tok_scr[s1_ref[t]] = t
            tok_scr[s2_ref[t]] = t
            return c

        lax.fori_loop(0, n_tok, place, 0, unroll=8)

    def used(tile):
        return tile * tb < nu_ref[0] * bm

    def row_copy(r, slot, t):
        return pltpu.make_async_copy(src_ref.at[t], buf.at[slot, r], sem.at[slot])

    def issue(tile, slot):
        e = te_ref[(tile * tb) // bm]
        first = offs_ref[e]
        last = first + cnt_ref[e] - 1

        def body(r, c):
            s = tile * tb + r
            s = jnp.where(s <= last, s, first)
            row_copy(r, slot, tok_scr[s]).start()
            return c

        lax.fori_loop(0, tb, body, 0, unroll=8)

    @pl.when((i == 0) & used(0))
    def _():
        issue(0, 0)

    @pl.when((i + 1 < n_tiles) & used(i + 1))
    def _():
        issue(i + 1, (i + 1) % 2)

    slot = i % 2

    @pl.when(used(i))
    def _():
        def wait(r, c):
            row_copy(r, slot, 0).wait()
            return c

        lax.fori_loop(0, tb, wait, 0, unroll=8)
        o_ref[...] = buf[slot].reshape(o_ref.shape).astype(o_ref.dtype)

    @pl.when(jnp.logical_not(used(i)))
    def _():
        o_ref[...] = jnp.zeros_like(o_ref)


def _gather_rows(slot1, slot2, tile_expert, offs, counts, n_used, src3, n_slot, bm, tb=256):
    n_tok, dl, _ = src3.shape
    tb = min(tb, bm)
    n_tiles = n_slot // tb
    return pl.pallas_call(
        functools.partial(_gather_kernel, tb=tb, bm=bm, n_tiles=n_tiles, n_tok=n_tok),
        grid_spec=pltpu.PrefetchScalarGridSpec(
            num_scalar_prefetch=6,
            grid=(n_tiles,),
            in_specs=[pl.BlockSpec(memory_space=pl.ANY)],
            out_specs=pl.BlockSpec((tb, dl * LANE), lambda i, *_: (i, 0)),
            scratch_shapes=[pltpu.SMEM((n_slot,), jnp.int32), pltpu.VMEM((2, tb, dl, LANE), F32),
                            pltpu.SemaphoreType.DMA((2,))]),
        out_shape=jax.ShapeDtypeStruct((n_slot, dl * LANE), BF16),
        compiler_params=_params(1),
        name="gather_rows",
    )(slot1, slot2, tile_expert, offs, counts, n_used, src3)


def _new_expert(te_ref, i):
    return (i == 0) | (te_ref[i] != te_ref[jnp.maximum(i - 1, 0)])


def _expert_gateup_kernel(te_ref, nu_ref, x_ref, wg_ref, wu_ref, o_ref, wg_scr, wu_scr):
    i = pl.program_id(1)
    used = i < nu_ref[0]
    fresh = _new_expert(te_ref, i)

    @pl.when(used & fresh)
    def _():
        x = x_ref[...]
        g = _stage_and_dot(x, wg_ref, wg_scr)
        u = _stage_and_dot(x, wu_ref, wu_scr)
        o_ref[...] = _swiglu(g, u).astype(o_ref.dtype)

    @pl.when(used & jnp.logical_not(fresh))
    def _():
        x = x_ref[...]
        o_ref[...] = _swiglu(_dot_staged(x, wg_scr), _dot_staged(x, wu_scr)).astype(o_ref.dtype)

    @pl.when(jnp.logical_not(used))
    def _():
        o_ref[...] = jnp.zeros_like(o_ref)


def _expert_gateup(tile_expert, n_used, xs, wg4, wu4, layer, bm, bn=512):
    p, d = xs.shape
    ef = wg4.shape[3]
    bn = _blk(ef, bn)
    w_spec = pl.BlockSpec((None, 1, d, bn), lambda j, i, te, nu: (layer, te[i], 0, j))
    return pl.pallas_call(
        _expert_gateup_kernel,
        grid_spec=pltpu.PrefetchScalarGridSpec(
            num_scalar_prefetch=2,
            grid=(ef // bn, p // bm),
            in_specs=[pl.BlockSpec((bm, d), lambda j, i, te, nu: (i, 0)), w_spec, w_spec],
            out_specs=pl.BlockSpec((bm, bn), lambda j, i, te, nu: (i, j)),
            scratch_shapes=[pltpu.VMEM((d, bn), BF16), pltpu.VMEM((d, bn), BF16)]),
        out_shape=jax.ShapeDtypeStruct((p, ef), BF16),
        compiler_params=_params(2),
        name="expert_gateup",
    )(tile_expert, n_used, xs, wg4, wu4)


def _expert_down_kernel(te_ref, nu_ref, a_ref, w_ref, o_ref, w_scr):
    i = pl.program_id(1)
    used = i < nu_ref[0]
    fresh = _new_expert(te_ref, i)

    @pl.when(used & fresh)
    def _():
        o_ref[...] = _stage_and_dot(a_ref[...], w_ref, w_scr).reshape(o_ref.shape)

    @pl.when(used & jnp.logical_not(fresh))
    def _():
        o_ref[...] = _dot_staged(a_ref[...], w_scr).reshape(o_ref.shape)

    @pl.when(jnp.logical_not(used))
    def _():
        o_ref[...] = jnp.zeros_like(o_ref)


def _expert_down(tile_expert, n_used, a, wd4, layer, bm, bn=1024):
    p, ef = a.shape
    d = wd4.shape[3]
    bn = _blk(d, bn, 8 * LANE)
    return pl.pallas_call(
        _expert_down_kernel,
        grid_spec=pltpu.PrefetchScalarGridSpec(
            num_scalar_prefetch=2,
            grid=(d // bn, p // bm),
            in_specs=[pl.BlockSpec((bm, ef), lambda j, i, te, nu: (i, 0)),
                      pl.BlockSpec((None, 1, ef, bn), lambda j, i, te, nu: (layer, te[i], 0, j))],
            out_specs=pl.BlockSpec((bm, bn // LANE, LANE), lambda j, i, te, nu: (i, j, 0)),
            scratch_shapes=[pltpu.VMEM((ef, bn), BF16)]),
        out_shape=jax.ShapeDtypeStruct((p, d // LANE, LANE), F32),
        compiler_params=_params(2),
        name="expert_down",
    )(tile_expert, n_used, a, wd4)


def _combine_kernel(s1_ref, s2_ref, y_ref, x_ref, route_ref, o_ref, buf, sem, *, tb, n_tiles):
    i = pl.program_id(0)

    def row_copy(k, r, slot, s):
        return pltpu.make_async_copy(y_ref.at[s], buf.at[slot, k, r], sem.at[slot])

    def issue(tile, slot):
        def body(r, c):
            t = tile * tb + r
            row_copy(0, r, slot, s1_ref[t]).start()
            row_copy(1, r, slot, s2_ref[t]).start()
            return c

        lax.fori_loop(0, tb, body, 0, unroll=4)

    @pl.when(i == 0)
    def _():
        issue(0, 0)

    @pl.when(i + 1 < n_tiles)
    def _():
        issue(i + 1, (i + 1) % 2)

    slot = i % 2

    def wait(r, c):
        row_copy(0, r, slot, 0).wait()
        row_copy(1, r, slot, 0).wait()
        return c

    lax.fori_loop(0, tb, wait, 0, unroll=4)
    route = route_ref[...]
    g1 = route[:, ROUTE_G1:ROUTE_G1 + 1]
    g2 = route[:, ROUTE_G2:ROUTE_G2 + 1]
    y1 = buf[slot, 0].reshape(o_ref.shape)
    y2 = buf[slot, 1].reshape(o_ref.shape)
    o_ref[...] = x_ref[...] + (g1 * y1 + g2 * y2)


def _combine(slot1, slot2, y3, x2, route, tb=128):
    n, d = x2.shape
    dl = d // LANE
    tb = _blk(n, tb, 8)
    n_tiles = n // tb
    return pl.pallas_call(
        functools.partial(_combine_kernel, tb=tb, n_tiles=n_tiles),
        grid_spec=pltpu.PrefetchScalarGridSpec(
            num_scalar_prefetch=2,
            grid=(n_tiles,),
            in_specs=[pl.BlockSpec(memory_space=pl.ANY),
                      pl.BlockSpec((tb, d), lambda i, s1, s2: (i, 0)),
                      pl.BlockSpec((tb, LANE), lambda i, s1, s2: (i, 0))],
            out_specs=pl.BlockSpec((tb, d), lambda i, s1, s2: (i, 0)),
            scratch_shapes=[pltpu.VMEM((2, 2, tb, dl, LANE), F32), pltpu.SemaphoreType.DMA((2,))]),
        out_shape=jax.ShapeDtypeStruct((n, d), F32),
        compiler_params=_params(1),
        name="expert_combine",
    )(slot1, slot2, y3, x2, route)


def _moe(x2, h3, w_router_padded, wg4, wu4, wd4, layer, bm=512):
    n = x2.shape[0]
    n_exp = wg4.shape[1]
    bm = _blk(n, bm, 8)
    route, cnt = _router(h3, w_router_padded, n_exp)
    counts = cnt[0, :n_exp].astype(jnp.int32)
    padded = ((counts + bm - 1) // bm) * bm
    ends = jnp.cumsum(padded)
    offs = ends - padded
    idx1 = route[:, ROUTE_IDX1].astype(jnp.int32)
    idx2 = route[:, ROUTE_IDX2].astype(jnp.int32)
    slot1 = offs[idx1] + route[:, ROUTE_RANK1].astype(jnp.int32)
    slot2 = offs[idx2] + route[:, ROUTE_RANK2].astype(jnp.int32)
    n_slot = n * TOP_K + n_exp * bm
    n_tiles = n_slot // bm
    tile_start = jnp.arange(n_tiles, dtype=jnp.int32) * bm
    tile_expert = jnp.minimum(jnp.sum((ends[None, :] <= tile_start[:, None]).astype(jnp.int32), axis=1), n_exp - 1)
    n_used = (ends[-1] // bm).reshape(1)
    xs = _gather_rows(slot1, slot2, tile_expert, offs, counts, n_used, h3, n_slot, bm)
    a = _expert_gateup(tile_expert, n_used, xs, wg4, wu4, layer, bm)
    y3 = _expert_down(tile_expert, n_used, a, wd4, layer, bm)
    return _combine(slot1, slot2, y3, x2, route)


def _swap(y, kind):
    if kind == "half64":
        return pltpu.roll(y, 64, 1)
    lane = lax.broadcasted_iota(jnp.int32, y.shape, 1)
    lo = pltpu.roll(y, 96, 1)
    hi = pltpu.roll(y, 32, 1)
    return jnp.where((lane & 63) < 32, lo, hi)


def _norm_rope(chunks, gains, cos, sin, rope_flags, swap_kind, dim, transpose=False):
    ssq = None
    for c in chunks:
        t = jnp.sum(c * c, axis=-1, keepdims=True)
        ssq = t if ssq is None else ssq + t
    r = lax.rsqrt(ssq / dim + EPS)
    outs = []
    for c, g, flag in zip(chunks, gains, rope_flags):
        y = (c * r) * g
        if flag:
            y = y * cos + _swap(y, swap_kind) * sin
        outs.append((y.T if transpose else y).astype(BF16))
    return outs


def _nbr_bias(t_ref, rm_ref, group, win_minus_q, kw):
    rows = []
    for j in range(C_QROWS):
        slabs = []
        for c in range(kw // LANE):
            d = win_minus_q + (C_WIN_R - 1) + 2 * c - j
            slabs.append(t_ref[0, d - NBR_D_LO])
        rows.append(jnp.concatenate(slabs, axis=1) + rm_ref[group, j:j + 1, :])
    return rows[0] if len(rows) == 1 else jnp.concatenate(rows, axis=0)


def _attn_kernel(*refs, n_grp, n_chunk, rope_flags, swap_kind, dim, scale, bq, rc, seq, kw, window, tmode,
                 k_major):
    has_rope = any(rope_flags)
    it = iter(refs)
    q_ref = next(it)
    k_refs = [next(it) for _ in range(n_chunk)]
    v_ref = next(it)
    gq_ref = next(it)
    gk_ref = next(it)
    cos_ref = next(it) if has_rope else None
    sin_ref = next(it) if has_rope else None
    t_ref = next(it) if tmode else None
    rm_ref = next(it) if tmode == "nbr" else None
    o_ref = next(it)
    k_scr = next(it)
    v_scr = next(it)
    qb = pl.program_id(2)

    @pl.when(qb == 0)
    def _():
        step = min(seq, 512)
        gains = [gk_ref[:, c * LANE:(c + 1) * LANE] for c in range(n_chunk)]
        for r0 in range(0, seq, step):
            chunks = [kr[r0:r0 + step, :] for kr in k_refs]
            cos = cos_ref[r0:r0 + step, :] if has_rope else None
            sin = sin_ref[r0:r0 + step, :] if has_rope else None
            outs = _norm_rope(chunks, gains, cos, sin, rope_flags, swap_kind, dim, transpose=k_major)
            for c, o in enumerate(outs):
                if k_major:
                    k_scr[c * LANE:(c + 1) * LANE, r0:r0 + step] = o
                else:
                    k_scr[r0:r0 + step, c * LANE:(c + 1) * LANE] = o
            v_scr[r0:r0 + step, :LANE] = v_ref[r0:r0 + step, :].astype(BF16)
            v_scr[r0:r0 + step, LANE:] = jnp.ones((step, LANE), BF16)

    if window == "full":
        kwin = k_scr[...]
        vwin = v_scr[...]

    row0 = pl.multiple_of(qb * bq, bq)
    gains = [gq_ref[:, c * LANE:(c + 1) * LANE] for c in range(n_chunk)]
    for r0 in range(0, bq, rc):
        cos = cos_ref[pl.ds(row0 + r0, rc), :] if has_rope else None
        sin = sin_ref[pl.ds(row0 + r0, rc), :] if has_rope else None
        if window == "rows":
            q_row = r0 // GRID_W
            start_row = min(max(q_row - C_WIN_R // 2, 0), seq // GRID_W - kw // GRID_W)
            kwin = k_scr[start_row * GRID_W:start_row * GRID_W + kw, :]
            vwin = v_scr[start_row * GRID_W:start_row * GRID_W + kw, :]
        for g in range(n_grp):
            base = g * n_chunk
            chunks = [q_ref[r0:r0 + rc, (base + c) * LANE:(base + c + 1) * LANE] for c in range(n_chunk)]
            outs = _norm_rope(chunks, gains, cos, sin, rope_flags, swap_kind, dim)
            q = outs[0] if n_chunk == 1 else jnp.concatenate(outs, axis=1)
            if k_major:
                s = jnp.dot(q, kwin, preferred_element_type=F32)
            else:
                s = lax.dot_general(q, kwin, (((1,), (1,)), ((), ())), preferred_element_type=F32)
            if tmode == "nbr":
                t = _nbr_bias(t_ref, rm_ref, q_row // C_QROWS, start_row - q_row, kw)
                s = jnp.where(t > -1e29, s * scale + t, NEG_INF)
                p = jnp.exp(s - jnp.max(s, axis=-1, keepdims=True))
            else:
                if tmode == "mul":
                    t = t_ref[r0:r0 + rc, :]
                    s = jnp.where(t > 0.0, s, NEG_INF)
                p = jnp.exp2((s - jnp.max(s, axis=-1, keepdims=True)) * (scale * LOG2_E))
                if tmode == "mul":
                    p = p * t
            o = jnp.dot(p.astype(BF16), vwin, preferred_element_type=F32)
            o_ref[r0:r0 + rc, g * LANE:(g + 1) * LANE] = o[:, :LANE] / o[:, LANE:]


def _attention(q_arr, q_col0, k_arrs, v_arr, v_col0, gq, gk, cos, sin, table, *, batch, seq, n_kv, n_grp,
               rope_flags, swap_kind, dim, bq, rc, window, tmode, name):
    n_chunk = len(k_arrs)
    k_major = n_chunk > 1 and window == "full"
    nqb = seq // bq
    kw = seq if window == "full" else min(C_KROWS, seq // GRID_W) * GRID_W
    qw = n_grp * n_chunk * LANE
    q_blk0 = q_col0 * LANE // qw
    assert q_blk0 * qw == q_col0 * LANE
    in_specs = [pl.BlockSpec((bq, qw), lambda b, g, i: (b * nqb + i, q_blk0 + g))]
    args = [q_arr]
    for arr, col0, per_head in k_arrs:
        in_specs.append(pl.BlockSpec((seq, LANE), functools.partial(
            lambda b, g, i, col0, per_head: (b, col0 + g * per_head), col0=col0, per_head=per_head)))
        args.append(arr)
    in_specs.append(pl.BlockSpec((seq, LANE), lambda b, g, i: (b, v_col0 + g)))
    args.append(v_arr)
    in_specs.append(pl.BlockSpec((1, n_chunk * LANE), lambda b, g, i: (0, 0)))
    args.append(gq.reshape(1, n_chunk * LANE))
    in_specs.append(pl.BlockSpec((1, n_chunk * LANE), lambda b, g, i: (0, 0)))
    args.append(gk.reshape(1, n_chunk * LANE))
    if any(rope_flags):
        in_specs += [pl.BlockSpec((seq, LANE), lambda b, g, i: (0, 0))] * 2
        args += [cos, sin]
    if tmode == "nbr":
        slabs, layer, row_mask = table
        assert seq // GRID_W >= C_KROWS and 2 * GRID_W == LANE and bq == seq and rc == C_QROWS * GRID_W
        in_specs.append(pl.BlockSpec((None, 1) + slabs.shape[2:], lambda b, g, i: (layer, g, 0, 0, 0)))
        in_specs.append(pl.BlockSpec(row_mask.shape, lambda b, g, i: (0, 0, 0)))
        args += [slabs, row_mask]
    elif tmode == "mul":
        in_specs.append(pl.BlockSpec((bq, kw), lambda b, g, i: (i, 0)))
        args.append(table)
    kern = functools.partial(
        _attn_kernel, n_grp=n_grp, n_chunk=n_chunk, rope_flags=rope_flags, swap_kind=swap_kind, dim=dim,
        scale=dim ** -0.5, bq=bq, rc=min(rc, bq), seq=seq, kw=kw, window=window, tmode=tmode,
        k_major=k_major)
    return pl.pallas_call(
        kern,
        name=name,
        grid=(batch, n_kv, nqb),
        in_specs=in_specs,
        out_specs=pl.BlockSpec((bq, n_grp * LANE), lambda b, g, i: (b * nqb + i, g)),
        out_shape=jax.ShapeDtypeStruct((batch * seq, n_kv * n_grp * LANE), F32),
        scratch_shapes=[pltpu.VMEM((n_chunk * LANE, seq) if k_major else (seq, n_chunk * LANE), BF16),
                        pltpu.VMEM((seq, 2 * LANE), BF16)],
        compiler_params=_params(3),
    )(*args)


def _rope_cs(pos, half):
    inv_freq = ROPE_THETA ** (-jnp.arange(half, dtype=F32) / half)
    ang = pos.astype(F32)[:, None] * inv_freq[None, :]
    return jnp.cos(ang), jnp.sin(ang)


def _rope_tables(seq):
    t = jnp.arange(seq)
    c, s = _rope_cs(t, HEAD_DIM // 2)
    a_cos, a_sin = jnp.concatenate([c, c], -1), jnp.concatenate([-s, s], -1)
    cr, sr = _rope_cs(t // GRID_W, HEAD_DIM // 4)
    cc, sc = _rope_cs(t % GRID_W, HEAD_DIM // 4)
    b_cos = jnp.concatenate([cr, cr, cc, cc], -1)
    b_sin = jnp.concatenate([-sr, sr, -sc, sc], -1)
    cd, sd = _rope_cs(t, D_ROPE // 2)
    z = jnp.zeros((seq, LANE - D_ROPE), F32)
    d_cos = jnp.concatenate([cd, cd, z], -1)
    d_sin = jnp.concatenate([-sd, sd, z], -1)
    return (a_cos, a_sin), (b_cos, b_sin), (d_cos, d_sin)


def _dilation_multiplicity(seq):
    t = jnp.arange(seq)
    delta = t[None, :] - t[:, None]
    mult = jnp.zeros((seq, seq), jnp.int32)
    for window, d in A_PATTERNS:
        half = window // (2 * d)
        mult = mult + ((delta % d == 0) & (jnp.abs(delta) <= half * d)).astype(jnp.int32)
    return mult.astype(F32)


def _neighbourhood_tables(rpb, seq):
    depth, heads = rpb.shape[:2]
    rows = seq // GRID_W
    wr = min(C_WIN_R, rows)
    kr = min(C_KROWS, rows)
    nqb = rows // C_QROWS
    c = jnp.arange(GRID_W)
    col_start = jnp.clip(c - C_WIN_C // 2, 0, GRID_W - C_WIN_C)
    col_ok = (c[None, :] >= col_start[:, None]) & (c[None, :] < col_start[:, None] + C_WIN_C)
    dcol = jnp.clip(c[None, :] - c[:, None] + (C_WIN_C - 1), 0, 2 * C_WIN_C - 2)
    by_col = jnp.take(rpb, dcol.reshape(-1), axis=3, mode="clip")
    by_col = by_col.reshape(depth, heads, 2 * C_WIN_R - 1, GRID_W, GRID_W)
    by_col = jnp.where(col_ok, by_col, NEG_INF)
    lo_pad = -NBR_D_LO
    hi_pad = NBR_D_HI + 1 - (2 * C_WIN_R - 2)
    by_col = jnp.pad(by_col, ((0, 0), (0, 0), (lo_pad, hi_pad), (0, 0), (0, 0)), constant_values=NEG_INF)
    slabs = jnp.concatenate([by_col[:, :, :-1], by_col[:, :, 1:]], axis=-1)

    qb = jnp.arange(nqb)
    r = qb[:, None] * C_QROWS + jnp.arange(C_QROWS)[None, :]
    win0 = jnp.clip(qb * C_QROWS - C_WIN_R // 2, 0, rows - kr)
    krow = win0[:, None] + jnp.arange(kr)[None, :]
    row_start = jnp.clip(r - wr // 2, 0, rows - wr)
    row_ok = (krow[:, None, :] >= row_start[:, :, None]) & (krow[:, None, :] < row_start[:, :, None] + wr)
    row_mask = jnp.where(row_ok, 0.0, NEG_INF).astype(F32)
    row_mask = jnp.repeat(row_mask, GRID_W, axis=2)
    row_mask = jnp.pad(row_mask, ((0, 0), (0, 8 - C_QROWS), (0, 0)))
    return slabs, row_mask


def _mixer(x2, h, l, w_in_t, w_out, lw, tabs, batch, seq):
    d_model = x2.shape[1]
    gw = d_model // N_MIXERS
    heads = gw // HEAD_DIM
    kv_heads = heads // 4
    nb = gw // LANE
    q_rank = lw["g_cq"].shape[0]
    (a_cos, a_sin), (b_cos, b_sin), (d_cos, d_sin) = tabs["rope"]

    col = {"qa": 0, "ka": nb, "va": 2 * nb, "qb": 3 * nb, "kb": 4 * nb, "vb": 4 * nb + kv_heads}
    col["qc"] = 4 * nb + 2 * kv_heads
    col["kc"] = col["qc"] + nb
    col["vc"] = col["kc"] + nb
    col["cq"] = col["vc"] + nb
    col["ckv"] = col["cq"] + q_rank // LANE
    n_main = (col["ckv"] + D_KV_RANK // LANE) * LANE
    proj = _matmul_ws(h, w_in_t, l, n_main, w_rows_are_cols=True, name="in_proj")
    kpe = _matmul_ws(h, w_in_t, l, LANE, bn=LANE, col0=n_main, w_rows_are_cols=True, name="kpe_proj")

    g_abc = lw["g_qk_abc"]
    common = dict(batch=batch, seq=seq)

    oa = _attention(proj, col["qa"], [(proj, col["ka"], 1)], proj, col["va"], g_abc[0, 0], g_abc[0, 1],
                    a_cos, a_sin, tabs["mult"], n_kv=heads, n_grp=1, rope_flags=(True,), swap_kind="half64",
                    dim=HEAD_DIM, bq=min(2048, seq), rc=ATTN_ROWS, window="full", tmode="mul", name="attn_a",
                    **common)
    ob = _attention(proj, col["qb"], [(proj, col["kb"], 1)], proj, col["vb"], g_abc[1, 0], g_abc[1, 1],
                    b_cos, b_sin, None, n_kv=kv_heads, n_grp=4, rope_flags=(True,), swap_kind="half32",
                    dim=HEAD_DIM, bq=min(1024, seq), rc=ATTN_ROWS, window="full", tmode=None, name="attn_b",
                    **common)
    oc = _attention(proj, col["qc"], [(proj, col["kc"], 1)], proj, col["vc"], g_abc[2, 0], g_abc[2, 1],
                    None, None, (tabs["nbr"][0], l, tabs["nbr"][1]), n_kv=heads, n_grp=1, rope_flags=(False,),
                    swap_kind=None, dim=HEAD_DIM, bq=seq, rc=C_QROWS * GRID_W, window="rows", tmode="nbr",
                    name="attn_c", **common)

    qd = _norm_matmul(_pieces(proj, col["cq"] * LANE, q_rank), lw["g_cq"], lw["w_uq"], name="uq_proj")
    kvd = _norm_matmul(_pieces(proj, col["ckv"] * LANE, D_KV_RANK), lw["g_ckv"], lw["w_ukv"], name="ukv_proj")
    od = _attention(qd, 0, [(kvd, 0, 1), (kpe, 0, 0)], kvd, heads, lw["gq_mla"], lw["gk_mla"],
                    d_cos, d_sin, None, n_kv=heads, n_grp=1, rope_flags=(False, True), swap_kind="half32",
                    dim=D_QK, bq=min(2048, seq), rc=ATTN_ROWS, window="full", tmode=None, name="attn_d",
                    **common)

    normed = _rmsnorm([_pieces(o, 0, gw) for o in (oa, ob, oc, od)], lw["g_grp"])
    return _matmul_ws(normed, w_out, l, d_model, res=x2, name="out_proj")


def _layer_weights(l, d_model, g_qk_abc, w_uq, w_ukv, g_cq, g_ckv, g_qk_mla, g_grp):
    heads = d_model // N_MIXERS // HEAD_DIM
    q_rank = w_uq.shape[1]
    uq = w_uq[l].reshape(q_rank, heads, D_QK)
    uq = jnp.pad(uq, ((0, 0), (0, 0), (0, 2 * LANE - D_QK))).reshape(q_rank, heads * 2 * LANE).astype(BF16)
    ukv = w_ukv[l].reshape(D_KV_RANK, heads, D_NOPE + HEAD_DIM)
    ukv = jnp.concatenate([ukv[:, :, :D_NOPE].reshape(D_KV_RANK, heads * D_NOPE),
                           ukv[:, :, D_NOPE:].reshape(D_KV_RANK, heads * HEAD_DIM)], axis=1).astype(BF16)
    pad_g = lambda g: jnp.pad(g, (0, 2 * LANE - D_QK))
    return dict(w_uq=uq, w_ukv=ukv, g_qk_abc=g_qk_abc[l], g_cq=g_cq[l], g_ckv=g_ckv[l],
                gq_mla=pad_g(g_qk_mla[l, 0]), gk_mla=pad_g(g_qk_mla[l, 1]), g_grp=g_grp[l])


def kernel(x, g_mix, w_in, g_qk_abc, w_uq, w_ukv, g_cq, g_ckv, g_qk_mla, rpb, g_grp, w_out, g_ffn, w_gate,
           w_up, w_down, w_router, we_gate, we_up, we_down):
    batch, seq, d_model = x.shape
    depth = g_mix.shape[0]
    n_exp = we_gate.shape[1]
    tabs = dict(rope=_rope_tables(seq), mult=_dilation_multiplicity(seq), nbr=_neighbourhood_tables(rpb, seq))
    x2 = x.reshape(batch * seq, d_model)
    w_in_t = jnp.swapaxes(w_in, 1, 2)
    for l in range(depth):
        lw = _layer_weights(l, d_model, g_qk_abc, w_uq, w_ukv, g_cq, g_ckv, g_qk_mla, g_grp)
        h = _rmsnorm([_pieces(x2, 0, d_model)], g_mix[l])
        x2 = _mixer(x2, h, l, w_in_t, w_out, lw, tabs, batch, seq)
        i = l // 2
        if l % 2 == 0:
            h = _rmsnorm([_pieces(x2, 0, d_model)], g_ffn[l])
            a = _gateup_dense(h, w_gate, w_up, i)
            x2 = _matmul(a, w_down[i].astype(BF16), res=x2, bk=w_down.shape[1] // 2, name="ffn_down")
        else:
            h3 = _rmsnorm([_pieces(x2, 0, d_model)], g_ffn[l], split_rows=True)
            wr = jnp.pad(w_router[i], ((0, 0), (0, LANE - n_exp))).astype(BF16)
            x2 = _moe(x2, h3, wr, we_gate, we_up, we_down, i)
    return x2.reshape(batch, seq, d_model)
```

```python
import functools
import math

import jax
import jax.numpy as jnp
from jax import lax
from jax.experimental import pallas as pl
from jax.experimental.pallas import tpu as pltpu

F32 = jnp.float32
BF16 = jnp.bfloat16

LANE = 128
HEAD_DIM = 128
N_MIXERS = 4
ROPE_THETA = 10000.0
EPS = 1e-6
NEG_INF = -1e30
LOG2_E = 1.4426950408889634
GRID_W = 64
A_PATTERNS = ((128, 1), (512, 4), (2048, 16))
C_WIN_R = 8
C_WIN_C = 16
C_QROWS = 2
C_KROWS = 10
ATTN_ROWS = 128
NBR_D_LO = (C_WIN_R - 1) - (C_KROWS - C_QROWS) - (C_QROWS - 1)
NBR_D_HI = (C_WIN_R - 1) + (C_KROWS - 2)
D_KV_RANK = 512
D_NOPE = 128
D_ROPE = 64
D_QK = D_NOPE + D_ROPE
TOP_K = 2
CAST_ROWS = 512
VMEM_LIMIT_BYTES = 61 * 1024 * 1024


def _blk(dim, target, mult=LANE):
    best = None
    d = mult
    while d <= min(dim, target):
        if dim % d == 0:
            best = d
        d += mult
    return best if best is not None else dim


def _params(n_grid):
    return pltpu.CompilerParams(dimension_semantics=("arbitrary",) * n_grid,
                                vmem_limit_bytes=VMEM_LIMIT_BYTES)


def _stage_and_dot(a, w_ref, scr, w_rows_are_cols=False):
    rows = scr.shape[0]
    step = min(rows, CAST_ROWS)
    parts = []
    acc = None
    for r0 in range(0, rows, step):
        slab = w_ref[0, r0:r0 + step, :].astype(BF16)
        scr[r0:r0 + step, :] = slab
        if w_rows_are_cols:
            parts.append(lax.dot_general(a, slab, (((1,), (1,)), ((), ())), preferred_element_type=F32))
        else:
            part = jnp.dot(a[:, r0:r0 + step], slab, preferred_element_type=F32)
            acc = part if acc is None else acc + part
    if w_rows_are_cols:
        return parts[0] if len(parts) == 1 else jnp.concatenate(parts, axis=1)
    return acc


def _dot_staged(a, scr, w_rows_are_cols=False):
    if w_rows_are_cols:
        return lax.dot_general(a, scr[...], (((1,), (1,)), ((), ())), preferred_element_type=F32)
    return jnp.dot(a, scr[...], preferred_element_type=F32)


def _rmsnorm_kernel(*refs, group_sizes, split_rows):
    n_in = sum(group_sizes)
    x_refs = refs[:n_in]
    g_ref = refs[n_in]
    o_ref = refs[n_in + 1]
    off = 0
    k = 0
    for size in group_sizes:
        pieces = [x_refs[k + p][...] for p in range(size)]
        k += size
        width = sum(p.shape[-1] for p in pieces)
        ssq = None
        for p in pieces:
            t = jnp.sum(p * p, axis=-1, keepdims=True)
            ssq = t if ssq is None else ssq + t
        r = lax.rsqrt(ssq / width + EPS)
        for p in pieces:
            w = p.shape[-1]
            y = (p * r) * g_ref[:, off:off + w]
            if split_rows:
                o_ref[...] = y.reshape(o_ref.shape)
            else:
                o_ref[:, off:off + w] = y.astype(o_ref.dtype)
            off += w


def _pieces(arr, col_off, width):
    pw = math.gcd(col_off, width) if col_off else width
    return [(arr, col_off // pw + p, pw) for p in range(width // pw)]


def _rmsnorm(groups, gain, bm=256, out_dtype=BF16, split_rows=False):
    flat = [p for g in groups for p in g]
    n = flat[0][0].shape[0]
    bm = _blk(n, bm, 8)
    total = sum(w for _, _, w in flat)
    in_specs = [pl.BlockSpec((bm, w), functools.partial(lambda i, c: (i, c), c=c)) for _, c, w in flat]
    in_specs.append(pl.BlockSpec((1, total), lambda i: (0, 0)))
    if split_rows:
        assert len(flat) == 1
        out_spec = pl.BlockSpec((bm, total // LANE, LANE), lambda i: (i, 0, 0))
        out_shape = jax.ShapeDtypeStruct((n, total // LANE, LANE), F32)
    else:
        out_spec = pl.BlockSpec((bm, total), lambda i: (i, 0))
        out_shape = jax.ShapeDtypeStruct((n, total), out_dtype)
    return pl.pallas_call(
        functools.partial(_rmsnorm_kernel, group_sizes=tuple(len(g) for g in groups), split_rows=split_rows),
        grid=(n // bm,),
        in_specs=in_specs,
        out_specs=out_spec,
        out_shape=out_shape,
        compiler_params=_params(1),
        name="rmsnorm",
    )(*[a for a, _, _ in flat], gain.reshape(1, total))


def _mm_kernel(a_ref, w_ref, *rest, nk, has_res):
    if has_res:
        r_ref, o_ref = rest
    else:
        (o_ref,) = rest
    part = jnp.dot(a_ref[...], w_ref[...], preferred_element_type=F32)
    if nk == 1:
        o_ref[...] = (r_ref[...] + part) if has_res else part
        return
    k = pl.program_id(2)

    @pl.when(k == 0)
    def _():
        o_ref[...] = (r_ref[...] + part) if has_res else part

    @pl.when(k > 0)
    def _():
        o_ref[...] += part


def _matmul(a, w, res=None, bm=1024, bn=512, bk=None, name="matmul"):
    m, kdim = a.shape
    n = w.shape[1]
    bm = _blk(m, bm, 8)
    bn = _blk(n, bn)
    bk = kdim if bk is None else _blk(kdim, bk)
    nk = kdim // bk
    in_specs = [pl.BlockSpec((bm, bk), lambda i, j, k: (i, k)),
                pl.BlockSpec((bk, bn), lambda i, j, k: (k, j))]
    args = [a, w]
    if res is not None:
        in_specs.append(pl.BlockSpec((bm, bn), lambda i, j, k: (i, j)))
        args.append(res)
    return pl.pallas_call(
        functools.partial(_mm_kernel, nk=nk, has_res=res is not None),
        grid=(m // bm, n // bn, nk),
        in_specs=in_specs,
        out_specs=pl.BlockSpec((bm, bn), lambda i, j, k: (i, j)),
        out_shape=jax.ShapeDtypeStruct((m, n), F32),
        compiler_params=_params(3),
        name=name,
    )(*args)


def _norm_mm_kernel(*refs, n_pieces):
    x_refs = refs[:n_pieces]
    g_ref, w_ref, o_ref = refs[n_pieces:]
    pieces = [r[...] for r in x_refs]
    width = sum(p.shape[-1] for p in pieces)
    ssq = None
    for p in pieces:
        t = jnp.sum(p * p, axis=-1, keepdims=True)
        ssq = t if ssq is None else ssq + t
    r = lax.rsqrt(ssq / width + EPS)
    acc = None
    off = 0
    for p in pieces:
        w = p.shape[-1]
        y = ((p * r) * g_ref[:, off:off + w]).astype(BF16)
        part = jnp.dot(y, w_ref[off:off + w, :], preferred_element_type=F32)
        acc = part if acc is None else acc + part
        off += w
    o_ref[...] = acc


def _norm_matmul(pieces, gain, w, bm=512, name="norm_matmul"):
    m = pieces[0][0].shape[0]
    kdim, n = w.shape
    bm = _blk(m, bm, 8)
    in_specs = [pl.BlockSpec((bm, pw), functools.partial(lambda i, c: (i, c), c=c)) for _, c, pw in pieces]
    in_specs += [pl.BlockSpec((1, kdim), lambda i: (0, 0)), pl.BlockSpec((kdim, n), lambda i: (0, 0))]
    return pl.pallas_call(
        functools.partial(_norm_mm_kernel, n_pieces=len(pieces)),
        grid=(m // bm,),
        in_specs=in_specs,
        out_specs=pl.BlockSpec((bm, n), lambda i: (i, 0)),
        out_shape=jax.ShapeDtypeStruct((m, n), F32),
        compiler_params=_params(1),
        name=name,
    )(*[a for a, _, _ in pieces], gain.reshape(1, kdim), w)


def _mm_ws_kernel(a_ref, w_ref, *rest, has_res, valid_cols, w_rows_are_cols):
    if has_res:
        r_ref, o_ref, w_scr = rest
    else:
        o_ref, w_scr = rest

    def finish(part):
        if valid_cols < part.shape[1]:
            lane = lax.broadcasted_iota(jnp.int32, part.shape, 1)
            part = jnp.where(lane < valid_cols, part, 0.0)
        o_ref[...] = (r_ref[...] + part) if has_res else part

    first = pl.program_id(1) == 0

    @pl.when(first)
    def _():
        finish(_stage_and_dot(a_ref[...], w_ref, w_scr, w_rows_are_cols))

    @pl.when(jnp.logical_not(first))
    def _():
        finish(_dot_staged(a_ref[...], w_scr, w_rows_are_cols))


def _matmul_ws(a, w3, layer, n_cols, res=None, bm=512, bn=1024, col0=0, w_rows_are_cols=False, name="matmul_ws"):
    m, kdim = a.shape
    bm = _blk(m, bm, 8)
    bn = _blk(n_cols, bn)
    blk0 = col0 // bn
    assert blk0 * bn == col0
    valid_cols = min(n_cols, w3.shape[1 if w_rows_are_cols else 2] - col0)
    assert valid_cols == n_cols or n_cols == bn
    if w_rows_are_cols:
        w_spec = pl.BlockSpec((1, bn, kdim), lambda j, i: (layer, blk0 + j, 0))
        w_scr = pltpu.VMEM((bn, kdim), BF16)
    else:
        w_spec = pl.BlockSpec((1, kdim, bn), lambda j, i: (layer, 0, blk0 + j))
        w_scr = pltpu.VMEM((kdim, bn), BF16)
    in_specs = [pl.BlockSpec((bm, kdim), lambda j, i: (i, 0)), w_spec]
    args = [a, w3]
    if res is not None:
        in_specs.append(pl.BlockSpec((bm, bn), lambda j, i: (i, j)))
        args.append(res)
    return pl.pallas_call(
        functools.partial(_mm_ws_kernel, has_res=res is not None, valid_cols=valid_cols,
                          w_rows_are_cols=w_rows_are_cols),
        grid=(n_cols // bn, m // bm),
        in_specs=in_specs,
        out_specs=pl.BlockSpec((bm, bn), lambda j, i: (i, j)),
        out_shape=jax.ShapeDtypeStruct((m, n_cols), F32),
        scratch_shapes=[w_scr],
        compiler_params=_params(2),
        name=name,
    )(*args)


def _swiglu(g, u):
    return (g / (1.0 + jnp.exp(-g))) * u


def _gateup_kernel(h_ref, wg_ref, wu_ref, o_ref, wg_scr, wu_scr):
    first = pl.program_id(1) == 0

    @pl.when(first)
    def _():
        h = h_ref[...]
        g = _stage_and_dot(h, wg_ref, wg_scr)
        u = _stage_and_dot(h, wu_ref, wu_scr)
        o_ref[...] = _swiglu(g, u).astype(o_ref.dtype)

    @pl.when(jnp.logical_not(first))
    def _():
        h = h_ref[...]
        o_ref[...] = _swiglu(_dot_staged(h, wg_scr), _dot_staged(h, wu_scr)).astype(o_ref.dtype)


def _gateup_dense(h, wg3, wu3, layer, bm=512, bn=512):
    m, d = h.shape
    f = wg3.shape[2]
    bm = _blk(m, bm, 8)
    bn = min(bn, f)
    w_spec = pl.BlockSpec((1, d, bn), lambda j, i: (layer, 0, j))
    return pl.pallas_call(
        _gateup_kernel,
        grid=(pl.cdiv(f, bn), m // bm),
        in_specs=[pl.BlockSpec((bm, d), lambda j, i: (i, 0)), w_spec, w_spec],
        out_specs=pl.BlockSpec((bm, bn), lambda j, i: (i, j)),
        out_shape=jax.ShapeDtypeStruct((m, f), BF16),
        scratch_shapes=[pltpu.VMEM((d, bn), BF16), pltpu.VMEM((d, bn), BF16)],
        compiler_params=_params(2),
        name="ffn_gateup",
    )(h, wg3, wu3)


ROUTE_IDX1, ROUTE_IDX2, ROUTE_G1, ROUTE_G2, ROUTE_RANK1, ROUTE_RANK2 = range(6)


def _router_kernel(h_ref, w_ref, route_ref, cnt_ref, carry, *, n_exp):
    @pl.when(pl.program_id(0) == 0)
    def _():
        carry[...] = jnp.zeros_like(carry)

    bm = h_ref.shape[0]
    h = h_ref[...].reshape(bm, h_ref.shape[1] * LANE).astype(BF16)
    logits = jnp.dot(h, w_ref[...], preferred_element_type=F32)
    lane = lax.broadcasted_iota(jnp.int32, logits.shape, 1).astype(F32)
    logits = jnp.where(lane < n_exp, logits, -jnp.inf)
    v1 = jnp.max(logits, axis=-1, keepdims=True)
    i1 = jnp.min(jnp.where(logits == v1, lane, float(LANE)), axis=-1, keepdims=True)
    rest = jnp.where(lane == i1, -jnp.inf, logits)
    v2 = jnp.max(rest, axis=-1, keepdims=True)
    i2 = jnp.min(jnp.where(rest == v2, lane, float(LANE)), axis=-1, keepdims=True)
    e2 = jnp.exp(v2 - v1)
    den = 1.0 + e2
    g1 = 1.0 / den
    g2 = e2 / den
    chosen = jnp.where((lane == i1) | (lane == i2), 1.0, 0.0)
    row = lax.broadcasted_iota(jnp.int32, (bm, bm), 0)
    col = lax.broadcasted_iota(jnp.int32, (bm, bm), 1)
    earlier = jnp.where(col < row, 1.0, 0.0).astype(BF16)
    before = jnp.dot(earlier, chosen.astype(BF16), preferred_element_type=F32) + carry[...]
    rank1 = jnp.sum(jnp.where(lane == i1, before, 0.0), axis=-1, keepdims=True)
    rank2 = jnp.sum(jnp.where(lane == i2, before, 0.0), axis=-1, keepdims=True)
    carry[...] += jnp.sum(chosen, axis=0, keepdims=True)
    route = jnp.zeros_like(logits)
    for pos, val in ((ROUTE_IDX1, i1), (ROUTE_IDX2, i2), (ROUTE_G1, g1), (ROUTE_G2, g2),
                     (ROUTE_RANK1, rank1), (ROUTE_RANK2, rank2)):
        route = jnp.where(lane == pos, val, route)
    route_ref[...] = route
    cnt_ref[...] = jnp.broadcast_to(carry[...], cnt_ref.shape)


def _router(h3, w_router_padded, n_exp, bm=512):
    m, dl, _ = h3.shape
    bm = _blk(m, bm, 8)
    return pl.pallas_call(
        functools.partial(_router_kernel, n_exp=n_exp),
        grid=(m // bm,),
        in_specs=[pl.BlockSpec((bm, dl, LANE), lambda i: (i, 0, 0)),
                  pl.BlockSpec((dl * LANE, LANE), lambda i: (0, 0))],
        out_specs=[pl.BlockSpec((bm, LANE), lambda i: (i, 0)),
                   pl.BlockSpec((8, LANE), lambda i: (0, 0))],
        out_shape=[jax.ShapeDtypeStruct((m, LANE), F32), jax.ShapeDtypeStruct((8, LANE), F32)],
        scratch_shapes=[pltpu.VMEM((1, LANE), F32)],
        compiler_params=_params(1),
        name="router",
    )(h3, w_router_padded)


def _gather_kernel(s1_ref, s2_ref, te_ref, offs_ref, cnt_ref, nu_ref, src_ref, o_ref, tok_scr, buf, sem, *,
                   tb, bm, n_tiles, n_tok):
    i = pl.program_id(0)

    @pl.when(i == 0)
    def _():
        def place(t, c):
            tok_scr[s1_ref[t]] = t
            tok_scr[s2_ref[t]] = t
            return c

        lax.fori_loop(0, n_tok, place, 0, unroll=8)

    def used(tile):
        return tile * tb < nu_ref[0] * bm

    def row_copy(r, slot, t):
        return pltpu.make_async_copy(src_ref.at[t], buf.at[slot, r], sem.at[slot])

    def issue(tile, slot):
        e = te_ref[(tile * tb) // bm]
        first = offs_ref[e]
        last = first + cnt_ref[e] - 1

        def body(r, c):
            s = tile * tb + r
            s = jnp.where(s <= last, s, first)
            row_copy(r, slot, tok_scr[s]).start()
            return c

        lax.fori_loop(0, tb, body, 0, unroll=8)

    @pl.when((i == 0) & used(0))
    def _():
        issue(0, 0)

    @pl.when((i + 1 < n_tiles) & used(i + 1))
    def _():
        issue(i + 1, (i + 1) % 2)

    slot = i % 2

    @pl.when(used(i))
    def _():
        def wait(r, c):
            row_copy(r, slot, 0).wait()
            return c

        lax.fori_loop(0, tb, wait, 0, unroll=8)
        o_ref[...] = buf[slot].reshape(o_ref.shape).astype(o_ref.dtype)

    @pl.when(jnp.logical_not(used(i)))
    def _():
        o_ref[...] = jnp.zeros_like(o_ref)


def _gather_rows(slot1, slot2, tile_expert, offs, counts, n_used, src3, n_slot, bm, tb=256):
    n_tok, dl, _ = src3.shape
    tb = min(tb, bm)
    n_tiles = n_slot // tb
    return pl.pallas_call(
        functools.partial(_gather_kernel, tb=tb, bm=bm, n_tiles=n_tiles, n_tok=n_tok),
        grid_spec=pltpu.PrefetchScalarGridSpec(
            num_scalar_prefetch=6,
            grid=(n_tiles,),
            in_specs=[pl.BlockSpec(memory_space=pl.ANY)],
            out_specs=pl.BlockSpec((tb, dl * LANE), lambda i, *_: (i, 0)),
            scratch_shapes=[pltpu.SMEM((n_slot,), jnp.int32), pltpu.VMEM((2, tb, dl, LANE), F32),
                            pltpu.SemaphoreType.DMA((2,))]),
        out_shape=jax.ShapeDtypeStruct((n_slot, dl * LANE), BF16),
        compiler_params=_params(1),
        name="gather_rows",
    )(slot1, slot2, tile_expert, offs, counts, n_used, src3)


def _new_expert(te_ref, i):
    return (i == 0) | (te_ref[i] != te_ref[jnp.maximum(i - 1, 0)])


def _expert_gateup_kernel(te_ref, nu_ref, x_ref, wg_ref, wu_ref, o_ref, wg_scr, wu_scr):
    i = pl.program_id(1)
    used = i < nu_ref[0]
    fresh = _new_expert(te_ref, i)

    @pl.when(used & fresh)
    def _():
        x = x_ref[...]
        g = _stage_and_dot(x, wg_ref, wg_scr)
        u = _stage_and_dot(x, wu_ref, wu_scr)
        o_ref[...] = _swiglu(g, u).astype(o_ref.dtype)

    @pl.when(used & jnp.logical_not(fresh))
    def _():
        x = x_ref[...]
        o_ref[...] = _swiglu(_dot_staged(x, wg_scr), _dot_staged(x, wu_scr)).astype(o_ref.dtype)

    @pl.when(jnp.logical_not(used))
    def _():
        o_ref[...] = jnp.zeros_like(o_ref)


def _expert_gateup(tile_expert, n_used, xs, wg4, wu4, layer, bm, bn=512):
    p, d = xs.shape
    ef = wg4.shape[3]
    bn = _blk(ef, bn)
    w_spec = pl.BlockSpec((None, 1, d, bn), lambda j, i, te, nu: (layer, te[i], 0, j))
    return pl.pallas_call(
        _expert_gateup_kernel,
        grid_spec=pltpu.PrefetchScalarGridSpec(
            num_scalar_prefetch=2,
            grid=(ef // bn, p // bm),
            in_specs=[pl.BlockSpec((bm, d), lambda j, i, te, nu: (i, 0)), w_spec, w_spec],
            out_specs=pl.BlockSpec((bm, bn), lambda j, i, te, nu: (i, j)),
            scratch_shapes=[pltpu.VMEM((d, bn), BF16), pltpu.VMEM((d, bn), BF16)]),
        out_shape=jax.ShapeDtypeStruct((p, ef), BF16),
        compiler_params=_params(2),
        name="expert_gateup",
    )(tile_expert, n_used, xs, wg4, wu4)


def _expert_down_kernel(te_ref, nu_ref, a_ref, w_ref, o_ref, w_scr):
    i = pl.program_id(1)
    used = i < nu_ref[0]
    fresh = _new_expert(te_ref, i)

    @pl.when(used & fresh)
    def _():
        o_ref[...] = _stage_and_dot(a_ref[...], w_ref, w_scr).reshape(o_ref.shape)

    @pl.when(used & jnp.logical_not(fresh))
    def _():
        o_ref[...] = _dot_staged(a_ref[...], w_scr).reshape(o_ref.shape)

    @pl.when(jnp.logical_not(used))
    def _():
        o_ref[...] = jnp.zeros_like(o_ref)


def _expert_down(tile_expert, n_used, a, wd4, layer, bm, bn=1024):
    p, ef = a.shape
    d = wd4.shape[3]
    bn = _blk(d, bn, 8 * LANE)
    return pl.pallas_call(
        _expert_down_kernel,
        grid_spec=pltpu.PrefetchScalarGridSpec(
            num_scalar_prefetch=2,
            grid=(d // bn, p // bm),
            in_specs=[pl.BlockSpec((bm, ef), lambda j, i, te, nu: (i, 0)),
                      pl.BlockSpec((None, 1, ef, bn), lambda j, i, te, nu: (layer, te[i], 0, j))],
            out_specs=pl.BlockSpec((bm, bn // LANE, LANE), lambda j, i, te, nu: (i, j, 0)),
            scratch_shapes=[pltpu.VMEM((ef, bn), BF16)]),
        out_shape=jax.ShapeDtypeStruct((p, d // LANE, LANE), F32),
        compiler_params=_params(2),
        name="expert_down",
    )(tile_expert, n_used, a, wd4)


def _combine_kernel(s1_ref, s2_ref, y_ref, x_ref, route_ref, o_ref, buf, sem, *, tb, n_tiles):
    i = pl.program_id(0)

    def row_copy(k, r, slot, s):
        return pltpu.make_async_copy(y_ref.at[s], buf.at[slot, k, r], sem.at[slot])

    def issue(tile, slot):
        def body(r, c):
            t = tile * tb + r
            row_copy(0, r, slot, s1_ref[t]).start()
            row_copy(1, r, slot, s2_ref[t]).start()
            return c

        lax.fori_loop(0, tb, body, 0, unroll=4)

    @pl.when(i == 0)
    def _():
        issue(0, 0)

    @pl.when(i + 1 < n_tiles)
    def _():
        issue(i + 1, (i + 1) % 2)

    slot = i % 2

    def wait(r, c):
        row_copy(0, r, slot, 0).wait()
        row_copy(1, r, slot, 0).wait()
        return c

    lax.fori_loop(0, tb, wait, 0, unroll=4)
    route = route_ref[...]
    g1 = route[:, ROUTE_G1:ROUTE_G1 + 1]
    g2 = route[:, ROUTE_G2:ROUTE_G2 + 1]
    y1 = buf[slot, 0].reshape(o_ref.shape)
    y2 = buf[slot, 1].reshape(o_ref.shape)
    o_ref[...] = x_ref[...] + (g1 * y1 + g2 * y2)


def _combine(slot1, slot2, y3, x2, route, tb=128):
    n, d = x2.shape
    dl = d // LANE
    tb = _blk(n, tb, 8)
    n_tiles = n // tb
    return pl.pallas_call(
        functools.partial(_combine_kernel, tb=tb, n_tiles=n_tiles),
        grid_spec=pltpu.PrefetchScalarGridSpec(
            num_scalar_prefetch=2,
            grid=(n_tiles,),
            in_specs=[pl.BlockSpec(memory_space=pl.ANY),
                      pl.BlockSpec((tb, d), lambda i, s1, s2: (i, 0)),
                      pl.BlockSpec((tb, LANE), lambda i, s1, s2: (i, 0))],
            out_specs=pl.BlockSpec((tb, d), lambda i, s1, s2: (i, 0)),
            scratch_shapes=[pltpu.VMEM((2, 2, tb, dl, LANE), F32), pltpu.SemaphoreType.DMA((2,))]),
        out_shape=jax.ShapeDtypeStruct((n, d), F32),
        compiler_params=_params(1),
        name="expert_combine",
    )(slot1, slot2, y3, x2, route)


def _moe(x2, h3, w_router_padded, wg4, wu4, wd4, layer, bm=512):
    n = x2.shape[0]
    n_exp = wg4.shape[1]
    bm = _blk(n, bm, 8)
    route, cnt = _router(h3, w_router_padded, n_exp)
    counts = cnt[0, :n_exp].astype(jnp.int32)
    padded = ((counts + bm - 1) // bm) * bm
    ends = jnp.cumsum(padded)
    offs = ends - padded
    idx1 = route[:, ROUTE_IDX1].astype(jnp.int32)
    idx2 = route[:, ROUTE_IDX2].astype(jnp.int32)
    slot1 = offs[idx1] + route[:, ROUTE_RANK1].astype(jnp.int32)
    slot2 = offs[idx2] + route[:, ROUTE_RANK2].astype(jnp.int32)
    n_slot = n * TOP_K + n_exp * bm
    n_tiles = n_slot // bm
    tile_start = jnp.arange(n_tiles, dtype=jnp.int32) * bm
    tile_expert = jnp.minimum(jnp.sum((ends[None, :] <= tile_start[:, None]).astype(jnp.int32), axis=1), n_exp - 1)
    n_used = (ends[-1] // bm).reshape(1)
    xs = _gather_rows(slot1, slot2, tile_expert, offs, counts, n_used, h3, n_slot, bm)
    a = _expert_gateup(tile_expert, n_used, xs, wg4, wu4, layer, bm)
    y3 = _expert_down(tile_expert, n_used, a, wd4, layer, bm)
    return _combine(slot1, slot2, y3, x2, route)


def _swap(y, kind):
    if kind == "half64":
        return pltpu.roll(y, 64, 1)
    lane = lax.broadcasted_iota(jnp.int32, y.shape, 1)
    lo = pltpu.roll(y, 96, 1)
    hi = pltpu.roll(y, 32, 1)
    return jnp.where((lane & 63) < 32, lo, hi)


def _norm_rope(chunks, gains, cos, sin, rope_flags, swap_kind, dim, transpose=False):
    ssq = None
    for c in chunks:
        t = jnp.sum(c * c, axis=-1, keepdims=True)
        ssq = t if ssq is None else ssq + t
    r = lax.rsqrt(ssq / dim + EPS)
    outs = []
    for c, g, flag in zip(chunks, gains, rope_flags):
        y = (c * r) * g
        if flag:
            y = y * cos + _swap(y, swap_kind) * sin
        outs.append((y.T if transpose else y).astype(BF16))
    return outs


def _nbr_bias(t_ref, rm_ref, group, win_minus_q, kw):
    rows = []
    for j in range(C_QROWS):
        slabs = []
        for c in range(kw // LANE):
            d = win_minus_q + (C_WIN_R - 1) + 2 * c - j
            slabs.append(t_ref[0, d - NBR_D_LO])
        rows.append(jnp.concatenate(slabs, axis=1) + rm_ref[group, j:j + 1, :])
    return rows[0] if len(rows) == 1 else jnp.concatenate(rows, axis=0)


def _attn_kernel(*refs, n_grp, n_chunk, rope_flags, swap_kind, dim, scale, bq, rc, seq, kw, window, tmode,
                 k_major):
    has_rope = any(rope_flags)
    it = iter(refs)
    q_ref = next(it)
    k_refs = [next(it) for _ in range(n_chunk)]
    v_ref = next(it)
    gq_ref = next(it)
    gk_ref = next(it)
    cos_ref = next(it) if has_rope else None
    sin_ref = next(it) if has_rope else None
    t_ref = next(it) if tmode else None
    rm_ref = next(it) if tmode == "nbr" else None
    o_ref = next(it)
    k_scr = next(it)
    v_scr = next(it)
    qb = pl.program_id(2)

    @pl.when(qb == 0)
    def _():
        step = min(seq, 512)
        gains = [gk_ref[:, c * LANE:(c + 1) * LANE] for c in range(n_chunk)]
        for r0 in range(0, seq, step):
            chunks = [kr[r0:r0 + step, :] for kr in k_refs]
            cos = cos_ref[r0:r0 + step, :] if has_rope else None
            sin = sin_ref[r0:r0 + step, :] if has_rope else None
            outs = _norm_rope(chunks, gains, cos, sin, rope_flags, swap_kind, dim, transpose=k_major)
            for c, o in enumerate(outs):
                if k_major:
                    k_scr[c * LANE:(c + 1) * LANE, r0:r0 + step] = o
                else:
                    k_scr[r0:r0 + step, c * LANE:(c + 1) * LANE] = o
            v_scr[r0:r0 + step, :LANE] = v_ref[r0:r0 + step, :].astype(BF16)
            v_scr[r0:r0 + step, LANE:] = jnp.ones((step, LANE), BF16)

    if window == "full":
        kwin = k_scr[...]
        vwin = v_scr[...]

    row0 = pl.multiple_of(qb * bq, bq)
    gains = [gq_ref[:, c * LANE:(c + 1) * LANE] for c in range(n_chunk)]
    for r0 in range(0, bq, rc):
        cos = cos_ref[pl.ds(row0 + r0, rc), :] if has_rope else None
        sin = sin_ref[pl.ds(row0 + r0, rc), :] if has_rope else None
        if window == "rows":
            q_row = r0 // GRID_W
            start_row = min(max(q_row - C_WIN_R // 2, 0), seq // GRID_W - kw // GRID_W)
            kwin = k_scr[start_row * GRID_W:start_row * GRID_W + kw, :]
            vwin = v_scr[start_row * GRID_W:start_row * GRID_W + kw, :]
        for g in range(n_grp):
            base = g * n_chunk
            chunks = [q_ref[r0:r0 + rc, (base + c) * LANE:(base + c + 1) * LANE] for c in range(n_chunk)]
            outs = _norm_rope(chunks, gains, cos, sin, rope_flags, swap_kind, dim)
            q = outs[0] if n_chunk == 1 else jnp.concatenate(outs, axis=1)
            if k_major:
                s = jnp.dot(q, kwin, preferred_element_type=F32)
            else:
                s = lax.dot_general(q, kwin, (((1,), (1,)), ((), ())), preferred_element_type=F32)
            if tmode == "nbr":
                t = _nbr_bias(t_ref, rm_ref, q_row // C_QROWS, start_row - q_row, kw)
                s = jnp.where(t > -1e29, s * scale + t, NEG_INF)
                p = jnp.exp(s - jnp.max(s, axis=-1, keepdims=True))
            else:
                if tmode == "mul":
                    t = t_ref[r0:r0 + rc, :]
                    s = jnp.where(t > 0.0, s, NEG_INF)
                p = jnp.exp2((s - jnp.max(s, axis=-1, keepdims=True)) * (scale * LOG2_E))
                if tmode == "mul":
                    p = p * t
            o = jnp.dot(p.astype(BF16), vwin, preferred_element_type=F32)
            o_ref[r0:r0 + rc, g * LANE:(g + 1) * LANE] = o[:, :LANE] / o[:, LANE:]


def _attention(q_arr, q_col0, k_arrs, v_arr, v_col0, gq, gk, cos, sin, table, *, batch, seq, n_kv, n_grp,
               rope_flags, swap_kind, dim, bq, rc, window, tmode, name):
    n_chunk = len(k_arrs)
    k_major = n_chunk > 1 and window == "full"
    nqb = seq // bq
    kw = seq if window == "full" else min(C_KROWS, seq // GRID_W) * GRID_W
    qw = n_grp * n_chunk * LANE
    q_blk0 = q_col0 * LANE // qw
    assert q_blk0 * qw == q_col0 * LANE
    in_specs = [pl.BlockSpec((bq, qw), lambda b, g, i: (b * nqb + i, q_blk0 + g))]
    args = [q_arr]
    for arr, col0, per_head in k_arrs:
        in_specs.append(pl.BlockSpec((seq, LANE), functools.partial(
            lambda b, g, i, col0, per_head: (b, col0 + g * per_head), col0=col0, per_head=per_head)))
        args.append(arr)
    in_specs.append(pl.BlockSpec((seq, LANE), lambda b, g, i: (b, v_col0 + g)))
    args.append(v_arr)
    in_specs.append(pl.BlockSpec((1, n_chunk * LANE), lambda b, g, i: (0, 0)))
    args.append(gq.reshape(1, n_chunk * LANE))
    in_specs.append(pl.BlockSpec((1, n_chunk * LANE), lambda b, g, i: (0, 0)))
    args.append(gk.reshape(1, n_chunk * LANE))
    if any(rope_flags):
        in_specs += [pl.BlockSpec((seq, LANE), lambda b, g, i: (0, 0))] * 2
        args += [cos, sin]
    if tmode == "nbr":
        slabs, layer, row_mask = table
        assert seq // GRID_W >= C_KROWS and 2 * GRID_W == LANE and bq == seq and rc == C_QROWS * GRID_W
        in_specs.append(pl.BlockSpec((None, 1) + slabs.shape[2:], lambda b, g, i: (layer, g, 0, 0, 0)))
        in_specs.append(pl.BlockSpec(row_mask.shape, lambda b, g, i: (0, 0, 0)))
        args += [slabs, row_mask]
    elif tmode == "mul":
        in_specs.append(pl.BlockSpec((bq, kw), lambda b, g, i: (i, 0)))
        args.append(table)
    kern = functools.partial(
        _attn_kernel, n_grp=n_grp, n_chunk=n_chunk, rope_flags=rope_flags, swap_kind=swap_kind, dim=dim,
        scale=dim ** -0.5, bq=bq, rc=min(rc, bq), seq=seq, kw=kw, window=window, tmode=tmode,
        k_major=k_major)
    return pl.pallas_call(
        kern,
        name=name,
        grid=(batch, n_kv, nqb),
        in_specs=in_specs,
        out_specs=pl.BlockSpec((bq, n_grp * LANE), lambda b, g, i: (b * nqb + i, g)),
        out_shape=jax.ShapeDtypeStruct((batch * seq, n_kv * n_grp * LANE), F32),
        scratch_shapes=[pltpu.VMEM((n_chunk * LANE, seq) if k_major else (seq, n_chunk * LANE), BF16),
                        pltpu.VMEM((seq, 2 * LANE), BF16)],
        compiler_params=_params(3),
    )(*args)


def _rope_cs(pos, half):
    inv_freq = ROPE_THETA ** (-jnp.arange(half, dtype=F32) / half)
    ang = pos.astype(F32)[:, None] * inv_freq[None, :]
    return jnp.cos(ang), jnp.sin(ang)


def _rope_tables(seq):
    t = jnp.arange(seq)
    c, s = _rope_cs(t, HEAD_DIM // 2)
    a_cos, a_sin = jnp.concatenate([c, c], -1), jnp.concatenate([-s, s], -1)
    cr, sr = _rope_cs(t // GRID_W, HEAD_DIM // 4)
    cc, sc = _rope_cs(t % GRID_W, HEAD_DIM // 4)
    b_cos = jnp.concatenate([cr, cr, cc, cc], -1)
    b_sin = jnp.concatenate([-sr, sr, -sc, sc], -1)
    cd, sd = _rope_cs(t, D_ROPE // 2)
    z = jnp.zeros((seq, LANE - D_ROPE), F32)
    d_cos = jnp.concatenate([cd, cd, z], -1)
    d_sin = jnp.concatenate([-sd, sd, z], -1)
    return (a_cos, a_sin), (b_cos, b_sin), (d_cos, d_sin)


def _dilation_multiplicity(seq):
    t = jnp.arange(seq)
    delta = t[None, :] - t[:, None]
    mult = jnp.zeros((seq, seq), jnp.int32)
    for window, d in A_PATTERNS:
        half = window // (2 * d)
        mult = mult + ((delta % d == 0) & (jnp.abs(delta) <= half * d)).astype(jnp.int32)
    return mult.astype(F32)


def _neighbourhood_tables(rpb, seq):
    depth, heads = rpb.shape[:2]
    rows = seq // GRID_W
    wr = min(C_WIN_R, rows)
    kr = min(C_KROWS, rows)
    nqb = rows // C_QROWS
    c = jnp.arange(GRID_W)
    col_start = jnp.clip(c - C_WIN_C // 2, 0, GRID_W - C_WIN_C)
    col_ok = (c[None, :] >= col_start[:, None]) & (c[None, :] < col_start[:, None] + C_WIN_C)
    dcol = jnp.clip(c[None, :] - c[:, None] + (C_WIN_C - 1), 0, 2 * C_WIN_C - 2)
    by_col = jnp.take(rpb, dcol.reshape(-1), axis=3, mode="clip")
    by_col = by_col.reshape(depth, heads, 2 * C_WIN_R - 1, GRID_W, GRID_W)
    by_col = jnp.where(col_ok, by_col, NEG_INF)
    lo_pad = -NBR_D_LO
    hi_pad = NBR_D_HI + 1 - (2 * C_WIN_R - 2)
    by_col = jnp.pad(by_col, ((0, 0), (0, 0), (lo_pad, hi_pad), (0, 0), (0, 0)), constant_values=NEG_INF)
    slabs = jnp.concatenate([by_col[:, :, :-1], by_col[:, :, 1:]], axis=-1)

    qb = jnp.arange(nqb)
    r = qb[:, None] * C_QROWS + jnp.arange(C_QROWS)[None, :]
    win0 = jnp.clip(qb * C_QROWS - C_WIN_R // 2, 0, rows - kr)
    krow = win0[:, None] + jnp.arange(kr)[None, :]
    row_start = jnp.clip(r - wr // 2, 0, rows - wr)
    row_ok = (krow[:, None, :] >= row_start[:, :, None]) & (krow[:, None, :] < row_start[:, :, None] + wr)
    row_mask = jnp.where(row_ok, 0.0, NEG_INF).astype(F32)
    row_mask = jnp.repeat(row_mask, GRID_W, axis=2)
    row_mask = jnp.pad(row_mask, ((0, 0), (0, 8 - C_QROWS), (0, 0)))
    return slabs, row_mask


def _mixer(x2, h, l, w_in_t, w_out, lw, tabs, batch, seq):
    d_model = x2.shape[1]
    gw = d_model // N_MIXERS
    heads = gw // HEAD_DIM
    kv_heads = heads // 4
    nb = gw // LANE
    q_rank = lw["g_cq"].shape[0]
    (a_cos, a_sin), (b_cos, b_sin), (d_cos, d_sin) = tabs["rope"]

    col = {"qa": 0, "ka": nb, "va": 2 * nb, "qb": 3 * nb, "kb": 4 * nb, "vb": 4 * nb + kv_heads}
    col["qc"] = 4 * nb + 2 * kv_heads
    col["kc"] = col["qc"] + nb
    col["vc"] = col["kc"] + nb
    col["cq"] = col["vc"] + nb
    col["ckv"] = col["cq"] + q_rank // LANE
    n_main = (col["ckv"] + D_KV_RANK // LANE) * LANE
    proj = _matmul_ws(h, w_in_t, l, n_main, w_rows_are_cols=True, name="in_proj")
    kpe = _matmul_ws(h, w_in_t, l, LANE, bn=LANE, col0=n_main, w_rows_are_cols=True, name="kpe_proj")

    g_abc = lw["g_qk_abc"]
    common = dict(batch=batch, seq=seq)

    oa = _attention(proj, col["qa"], [(proj, col["ka"], 1)], proj, col["va"], g_abc[0, 0], g_abc[0, 1],
                    a_cos, a_sin, tabs["mult"], n_kv=heads, n_grp=1, rope_flags=(True,), swap_kind="half64",
                    dim=HEAD_DIM, bq=min(2048, seq), rc=ATTN_ROWS, window="full", tmode="mul", name="attn_a",
                    **common)
    ob = _attention(proj, col["qb"], [(proj, col["kb"], 1)], proj, col["vb"], g_abc[1, 0], g_abc[1, 1],
                    b_cos, b_sin, None, n_kv=kv_heads, n_grp=4, rope_flags=(True,), swap_kind="half32",
                    dim=HEAD_DIM, bq=min(1024, seq), rc=ATTN_ROWS, window="full", tmode=None, name="attn_b",
                    **common)
    oc = _attention(proj, col["qc"], [(proj, col["kc"], 1)], proj, col["vc"], g_abc[2, 0], g_abc[2, 1],
                    None, None, (tabs["nbr"][0], l, tabs["nbr"][1]), n_kv=heads, n_grp=1, rope_flags=(False,),
                    swap_kind=None, dim=HEAD_DIM, bq=seq, rc=C_QROWS * GRID_W, window="rows", tmode="nbr",
                    name="attn_c", **common)

    qd = _norm_matmul(_pieces(proj, col["cq"] * LANE, q_rank), lw["g_cq"], lw["w_uq"], name="uq_proj")
    kvd = _norm_matmul(_pieces(proj, col["ckv"] * LANE, D_KV_RANK), lw["g_ckv"], lw["w_ukv"], name="ukv_proj")
    od = _attention(qd, 0, [(kvd, 0, 1), (kpe, 0, 0)], kvd, heads, lw["gq_mla"], lw["gk_mla"],
                    d_cos, d_sin, None, n_kv=heads, n_grp=1, rope_flags=(False, True), swap_kind="half32",
                    dim=D_QK, bq=min(2048, seq), rc=ATTN_ROWS, window="full", tmode=None, name="attn_d",
                    **common)

    normed = _rmsnorm([_pieces(o, 0, gw) for o in (oa, ob, oc, od)], lw["g_grp"])
    return _matmul_ws(normed, w_out, l, d_model, res=x2, bm=1024, bn=512, name="out_proj")


def _layer_weights(l, d_model, g_qk_abc, w_uq, w_ukv, g_cq, g_ckv, g_qk_mla, g_grp):
    heads = d_model // N_MIXERS // HEAD_DIM
    q_rank = w_uq.shape[1]
    uq = w_uq[l].reshape(q_rank, heads, D_QK)
    uq = jnp.pad(uq, ((0, 0), (0, 0), (0, 2 * LANE - D_QK))).reshape(q_rank, heads * 2 * LANE).astype(BF16)
    ukv = w_ukv[l].reshape(D_KV_RANK, heads, D_NOPE + HEAD_DIM)
    ukv = jnp.concatenate([ukv[:, :, :D_NOPE].reshape(D_KV_RANK, heads * D_NOPE),
                           ukv[:, :, D_NOPE:].reshape(D_KV_RANK, heads * HEAD_DIM)], axis=1).astype(BF16)
    pad_g = lambda g: jnp.pad(g, (0, 2 * LANE - D_QK))
    return dict(w_uq=uq, w_ukv=ukv, g_qk_abc=g_qk_abc[l], g_cq=g_cq[l], g_ckv=g_ckv[l],
                gq_mla=pad_g(g_qk_mla[l, 0]), gk_mla=pad_g(g_qk_mla[l, 1]), g_grp=g_grp[l])


def kernel(x, g_mix, w_in, g_qk_abc, w_uq, w_ukv, g_cq, g_ckv, g_qk_mla, rpb, g_grp, w_out, g_ffn, w_gate,
           w_up, w_down, w_router, we_gate, we_up, we_down):
    batch, seq, d_model = x.shape
    depth = g_mix.shape[0]
    n_exp = we_gate.shape[1]
    tabs = dict(rope=_rope_tables(seq), mult=_dilation_multiplicity(seq), nbr=_neighbourhood_tables(rpb, seq))
    x2 = x.reshape(batch * seq, d_model)
    w_in_t = jnp.swapaxes(w_in, 1, 2)
    for l in range(depth):
        lw = _layer_weights(l, d_model, g_qk_abc, w_uq, w_ukv, g_cq, g_ckv, g_qk_mla, g_grp)
        h = _rmsnorm([_pieces(x2, 0, d_model)], g_mix[l])
        x2 = _mixer(x2, h, l, w_in_t, w_out, lw, tabs, batch, seq)
        i = l // 2
        if l % 2 == 0:
            h = _rmsnorm([_pieces(x2, 0, d_model)], g_ffn[l])
            a = _gateup_dense(h, w_gate, w_up, i)
            x2 = _matmul(a, w_down[i].astype(BF16), res=x2, bk=w_down.shape[1] // 2, name="ffn_down")
        else:
            h3 = _rmsnorm([_pieces(x2, 0, d_model)], g_ffn[l], split_rows=True)
            wr = jnp.pad(w_router[i], ((0, 0), (0, LANE - n_exp))).astype(BF16)
            x2 = _moe(x2, h3, wr, we_gate, we_up, we_down, i)
    return x2.reshape(batch, seq, d_model)
```

```python
import functools
import math

import jax
import jax.numpy as jnp
from jax import lax
from jax.experimental import pallas as pl
from jax.experimental.pallas import tpu as pltpu

F32 = jnp.float32
BF16 = jnp.bfloat16

LANE = 128
HEAD_DIM = 128
N_MIXERS = 4
ROPE_THETA = 10000.0
EPS = 1e-6
NEG_INF = -1e30
LOG2_E = 1.4426950408889634
GRID_W = 64
A_PATTERNS = ((128, 1), (512, 4), (2048, 16))
C_WIN_R = 8
C_WIN_C = 16
C_QROWS = 2
C_KROWS = 10
ATTN_ROWS = 128
NBR_D_LO = (C_WIN_R - 1) - (C_KROWS - C_QROWS) - (C_QROWS - 1)
NBR_D_HI = (C_WIN_R - 1) + (C_KROWS - 2)
D_KV_RANK = 512
D_NOPE = 128
D_ROPE = 64
D_QK = D_NOPE + D_ROPE
TOP_K = 2
CAST_ROWS = 512
VMEM_LIMIT_BYTES = 56 * 1024 * 1024


def _blk(dim, target, mult=LANE):
    best = None
    d = mult
    while d <= min(dim, target):
        if dim % d == 0:
            best = d
        d += mult
    return best if best is not None else dim


def _params(n_grid):
    return pltpu.CompilerParams(dimension_semantics=("arbitrary",) * n_grid,
                                vmem_limit_bytes=VMEM_LIMIT_BYTES)


def _stage_and_dot(a, w_ref, scr, w_rows_are_cols=False):
    rows = scr.shape[0]
    step = min(rows, CAST_ROWS)
    parts = []
    acc = None
    for r0 in range(0, rows, step):
        slab = w_ref[0, r0:r0 + step, :].astype(BF16)
        scr[r0:r0 + step, :] = slab
        if w_rows_are_cols:
            parts.append(lax.dot_general(a, slab, (((1,), (1,)), ((), ())), preferred_element_type=F32))
        else:
            part = jnp.dot(a[:, r0:r0 + step], slab, preferred_element_type=F32)
            acc = part if acc is None else acc + part
    if w_rows_are_cols:
        return parts[0] if len(parts) == 1 else jnp.concatenate(parts, axis=1)
    return acc


def _dot_staged(a, scr, w_rows_are_cols=False):
    if w_rows_are_cols:
        return lax.dot_general(a, scr[...], (((1,), (1,)), ((), ())), preferred_element_type=F32)
    return jnp.dot(a, scr[...], preferred_element_type=F32)


def _rmsnorm_kernel(*refs, group_sizes, split_rows):
    n_in = sum(group_sizes)
    x_refs = refs[:n_in]
    g_ref = refs[n_in]
    o_ref = refs[n_in + 1]
    off = 0
    k = 0
    for size in group_sizes:
        pieces = [x_refs[k + p][...] for p in range(size)]
        k += size
        width = sum(p.shape[-1] for p in pieces)
        ssq = None
        for p in pieces:
            t = jnp.sum(p * p, axis=-1, keepdims=True)
            ssq = t if ssq is None else ssq + t
        r = lax.rsqrt(ssq / width + EPS)
        for p in pieces:
            w = p.shape[-1]
            y = (p * r) * g_ref[:, off:off + w]
            if split_rows:
                o_ref[...] = y.reshape(o_ref.shape)
            else:
                o_ref[:, off:off + w] = y.astype(o_ref.dtype)
            off += w


def _pieces(arr, col_off, width):
    pw = math.gcd(col_off, width) if col_off else width
    return [(arr, col_off // pw + p, pw) for p in range(width // pw)]


def _rmsnorm(groups, gain, bm=256, out_dtype=BF16, split_rows=False):
    flat = [p for g in groups for p in g]
    n = flat[0][0].shape[0]
    bm = _blk(n, bm, 8)
    total = sum(w for _, _, w in flat)
    in_specs = [pl.BlockSpec((bm, w), functools.partial(lambda i, c: (i, c), c=c)) for _, c, w in flat]
    in_specs.append(pl.BlockSpec((1, total), lambda i: (0, 0)))
    if split_rows:
        assert len(flat) == 1
        out_spec = pl.BlockSpec((bm, total // LANE, LANE), lambda i: (i, 0, 0))
        out_shape = jax.ShapeDtypeStruct((n, total // LANE, LANE), F32)
    else:
        out_spec = pl.BlockSpec((bm, total), lambda i: (i, 0))
        out_shape = jax.ShapeDtypeStruct((n, total), out_dtype)
    return pl.pallas_call(
        functools.partial(_rmsnorm_kernel, group_sizes=tuple(len(g) for g in groups), split_rows=split_rows),
        grid=(n // bm,),
        in_specs=in_specs,
        out_specs=out_spec,
        out_shape=out_shape,
        compiler_params=_params(1),
        name="rmsnorm",
    )(*[a for a, _, _ in flat], gain.reshape(1, total))


def _mm_kernel(a_ref, w_ref, *rest, nk, has_res):
    if has_res:
        r_ref, o_ref = rest
    else:
        (o_ref,) = rest
    w = w_ref[0].astype(BF16) if len(w_ref.shape) == 3 else w_ref[...]
    part = jnp.dot(a_ref[...], w, preferred_element_type=F32)
    if nk == 1:
        o_ref[...] = (r_ref[...] + part) if has_res else part
        return
    k = pl.program_id(2)

    @pl.when(k == 0)
    def _():
        o_ref[...] = (r_ref[...] + part) if has_res else part

    @pl.when(k > 0)
    def _():
        o_ref[...] += part


def _matmul(a, w, res=None, bm=1024, bn=512, bk=None, layer=None, name="matmul"):
    m, kdim = a.shape
    n = w.shape[-1]
    bm = _blk(m, bm, 8)
    bn = _blk(n, bn)
    bk = kdim if bk is None else _blk(kdim, bk)
    nk = kdim // bk
    if layer is None:
        w_spec = pl.BlockSpec((bk, bn), lambda i, j, k: (k, j))
    else:
        w_spec = pl.BlockSpec((1, bk, bn), lambda i, j, k: (layer, k, j))
    in_specs = [pl.BlockSpec((bm, bk), lambda i, j, k: (i, k)), w_spec]
    args = [a, w]
    if res is not None:
        in_specs.append(pl.BlockSpec((bm, bn), lambda i, j, k: (i, j)))
        args.append(res)
    return pl.pallas_call(
        functools.partial(_mm_kernel, nk=nk, has_res=res is not None),
        grid=(m // bm, n // bn, nk),
        in_specs=in_specs,
        out_specs=pl.BlockSpec((bm, bn), lambda i, j, k: (i, j)),
        out_shape=jax.ShapeDtypeStruct((m, n), F32),
        compiler_params=_params(3),
        name=name,
    )(*args)


def _norm_mm_kernel(*refs, n_pieces):
    x_refs = refs[:n_pieces]
    g_ref, w_ref, o_ref = refs[n_pieces:]
    pieces = [r[...] for r in x_refs]
    width = sum(p.shape[-1] for p in pieces)
    ssq = None
    for p in pieces:
        t = jnp.sum(p * p, axis=-1, keepdims=True)
        ssq = t if ssq is None else ssq + t
    r = lax.rsqrt(ssq / width + EPS)
    acc = None
    off = 0
    for p in pieces:
        w = p.shape[-1]
        y = ((p * r) * g_ref[:, off:off + w]).astype(BF16)
        part = jnp.dot(y, w_ref[off:off + w, :], preferred_element_type=F32)
        acc = part if acc is None else acc + part
        off += w
    o_ref[...] = acc


def _norm_matmul(pieces, gain, w, bm=512, name="norm_matmul"):
    m = pieces[0][0].shape[0]
    kdim, n = w.shape
    bm = _blk(m, bm, 8)
    in_specs = [pl.BlockSpec((bm, pw), functools.partial(lambda i, c: (i, c), c=c)) for _, c, pw in pieces]
    in_specs += [pl.BlockSpec((1, kdim), lambda i: (0, 0)), pl.BlockSpec((kdim, n), lambda i: (0, 0))]
    return pl.pallas_call(
        functools.partial(_norm_mm_kernel, n_pieces=len(pieces)),
        grid=(m // bm,),
        in_specs=in_specs,
        out_specs=pl.BlockSpec((bm, n), lambda i: (i, 0)),
        out_shape=jax.ShapeDtypeStruct((m, n), F32),
        compiler_params=_params(1),
        name=name,
    )(*[a for a, _, _ in pieces], gain.reshape(1, kdim), w)


def _mm_ws_kernel(a_ref, w_ref, *rest, has_res, valid_cols, w_rows_are_cols):
    if has_res:
        r_ref, o_ref, w_scr = rest
    else:
        o_ref, w_scr = rest

    def finish(part):
        if valid_cols < part.shape[1]:
            lane = lax.broadcasted_iota(jnp.int32, part.shape, 1)
            part = jnp.where(lane < valid_cols, part, 0.0)
        o_ref[...] = (r_ref[...] + part) if has_res else part

    first = pl.program_id(1) == 0

    @pl.when(first)
    def _():
        finish(_stage_and_dot(a_ref[...], w_ref, w_scr, w_rows_are_cols))

    @pl.when(jnp.logical_not(first))
    def _():
        finish(_dot_staged(a_ref[...], w_scr, w_rows_are_cols))


def _matmul_ws(a, w3, layer, n_cols, res=None, bm=1024, bn=512, col0=0, w_rows_are_cols=False, name="matmul_ws"):
    m, kdim = a.shape
    bm = _blk(m, bm, 8)
    bn = _blk(n_cols, bn)
    blk0 = col0 // bn
    assert blk0 * bn == col0
    valid_cols = min(n_cols, w3.shape[1 if w_rows_are_cols else 2] - col0)
    assert valid_cols == n_cols or n_cols == bn
    if w_rows_are_cols:
        w_spec = pl.BlockSpec((1, bn, kdim), lambda j, i: (layer, blk0 + j, 0))
        w_scr = pltpu.VMEM((bn, kdim), BF16)
    else:
        w_spec = pl.BlockSpec((1, kdim, bn), lambda j, i: (layer, 0, blk0 + j))
        w_scr = pltpu.VMEM((kdim, bn), BF16)
    in_specs = [pl.BlockSpec((bm, kdim), lambda j, i: (i, 0)), w_spec]
    args = [a, w3]
    if res is not None:
        in_specs.append(pl.BlockSpec((bm, bn), lambda j, i: (i, j)))
        args.append(res)
    return pl.pallas_call(
        functools.partial(_mm_ws_kernel, has_res=res is not None, valid_cols=valid_cols,
                          w_rows_are_cols=w_rows_are_cols),
        grid=(n_cols // bn, m // bm),
        in_specs=in_specs,
        out_specs=pl.BlockSpec((bm, bn), lambda j, i: (i, j)),
        out_shape=jax.ShapeDtypeStruct((m, n_cols), F32),
        scratch_shapes=[w_scr],
        compiler_params=_params(2),
        name=name,
    )(*args)


def _swiglu(g, u):
    return (g / (1.0 + jnp.exp(-g))) * u


def _gateup_kernel(h_ref, wg_ref, wu_ref, o_ref, wg_scr, wu_scr):
    first = pl.program_id(1) == 0

    @pl.when(first)
    def _():
        h = h_ref[...]
        g = _stage_and_dot(h, wg_ref, wg_scr)
        u = _stage_and_dot(h, wu_ref, wu_scr)
        o_ref[...] = _swiglu(g, u).astype(o_ref.dtype)

    @pl.when(jnp.logical_not(first))
    def _():
        h = h_ref[...]
        o_ref[...] = _swiglu(_dot_staged(h, wg_scr), _dot_staged(h, wu_scr)).astype(o_ref.dtype)


def _gateup_dense(h, wg3, wu3, layer, bm=1024, bn=256):
    m, d = h.shape
    f = wg3.shape[2]
    bm = _blk(m, bm, 8)
    bn = _blk(f, bn)
    w_spec = pl.BlockSpec((1, d, bn), lambda j, i: (layer, 0, j))
    return pl.pallas_call(
        _gateup_kernel,
        grid=(f // bn, m // bm),
        in_specs=[pl.BlockSpec((bm, d), lambda j, i: (i, 0)), w_spec, w_spec],
        out_specs=pl.BlockSpec((bm, bn), lambda j, i: (i, j)),
        out_shape=jax.ShapeDtypeStruct((m, f), BF16),
        scratch_shapes=[pltpu.VMEM((d, bn), BF16), pltpu.VMEM((d, bn), BF16)],
        compiler_params=_params(2),
        name="ffn_gateup",
    )(h, wg3, wu3)


ROUTE_IDX1, ROUTE_IDX2, ROUTE_G1, ROUTE_G2, ROUTE_RANK1, ROUTE_RANK2 = range(6)


def _router_kernel(h_ref, w_ref, route_ref, cnt_ref, carry, *, n_exp):
    @pl.when(pl.program_id(0) == 0)
    def _():
        carry[...] = jnp.zeros_like(carry)

    bm = h_ref.shape[0]
    h = h_ref[...].reshape(bm, h_ref.shape[1] * LANE).astype(BF16)
    logits = jnp.dot(h, w_ref[...], preferred_element_type=F32)
    lane = lax.broadcasted_iota(jnp.int32, logits.shape, 1).astype(F32)
    logits = jnp.where(lane < n_exp, logits, -jnp.inf)
    v1 = jnp.max(logits, axis=-1, keepdims=True)
    i1 = jnp.min(jnp.where(logits == v1, lane, float(LANE)), axis=-1, keepdims=True)
    rest = jnp.where(lane == i1, -jnp.inf, logits)
    v2 = jnp.max(rest, axis=-1, keepdims=True)
    i2 = jnp.min(jnp.where(rest == v2, lane, float(LANE)), axis=-1, keepdims=True)
    e2 = jnp.exp(v2 - v1)
    den = 1.0 + e2
    g1 = 1.0 / den
    g2 = e2 / den
    chosen = jnp.where((lane == i1) | (lane == i2), 1.0, 0.0)
    row = lax.broadcasted_iota(jnp.int32, (bm, bm), 0)
    col = lax.broadcasted_iota(jnp.int32, (bm, bm), 1)
    earlier = jnp.where(col < row, 1.0, 0.0).astype(BF16)
    before = jnp.dot(earlier, chosen.astype(BF16), preferred_element_type=F32) + carry[...]
    rank1 = jnp.sum(jnp.where(lane == i1, before, 0.0), axis=-1, keepdims=True)
    rank2 = jnp.sum(jnp.where(lane == i2, before, 0.0), axis=-1, keepdims=True)
    carry[...] += jnp.sum(chosen, axis=0, keepdims=True)
    route = jnp.zeros_like(logits)
    for pos, val in ((ROUTE_IDX1, i1), (ROUTE_IDX2, i2), (ROUTE_G1, g1), (ROUTE_G2, g2),
                     (ROUTE_RANK1, rank1), (ROUTE_RANK2, rank2)):
        route = jnp.where(lane == pos, val, route)
    route_ref[...] = route
    cnt_ref[...] = jnp.broadcast_to(carry[...], cnt_ref.shape)


def _router(h3, w_router_padded, n_exp, bm=512):
    m, dl, _ = h3.shape
    bm = _blk(m, bm, 8)
    return pl.pallas_call(
        functools.partial(_router_kernel, n_exp=n_exp),
        grid=(m // bm,),
        in_specs=[pl.BlockSpec((bm, dl, LANE), lambda i: (i, 0, 0)),
                  pl.BlockSpec((dl * LANE, LANE), lambda i: (0, 0))],
        out_specs=[pl.BlockSpec((bm, LANE), lambda i: (i, 0)),
                   pl.BlockSpec((8, LANE), lambda i: (0, 0))],
        out_shape=[jax.ShapeDtypeStruct((m, LANE), F32), jax.ShapeDtypeStruct((8, LANE), F32)],
        scratch_shapes=[pltpu.VMEM((1, LANE), F32)],
        compiler_params=_params(1),
        name="router",
    )(h3, w_router_padded)


def _gather_kernel(s1_ref, s2_ref, te_ref, offs_ref, cnt_ref, nu_ref, src_ref, o_ref, tok_scr, buf, sem, *,
                   tb, bm, n_tiles, n_tok):
    i = pl.program_id(0)

    @pl.when(i == 0)
    def _():
        def place(t, c):
            tok_scr[s1_ref[t]] = t
            tok_scr[s2_ref[t]] = t
            return c

        lax.fori_loop(0, n_tok, place, 0, unroll=8)

    def used(tile):
        return tile * tb < nu_ref[0] * bm

    def row_copy(r, slot, t):
        return pltpu.make_async_copy(src_ref.at[t], buf.at[slot, r], sem.at[slot])

    def issue(tile, slot):
        e = te_ref[(tile * tb) // bm]
        first = offs_ref[e]
        last = first + cnt_ref[e] - 1

        def body(r, c):
            s = tile * tb + r
            s = jnp.where(s <= last, s, first)
            row_copy(r, slot, tok_scr[s]).start()
            return c

        lax.fori_loop(0, tb, body, 0, unroll=8)

    @pl.when((i == 0) & used(0))
    def _():
        issue(0, 0)

    @pl.when((i + 1 < n_tiles) & used(i + 1))
    def _():
        issue(i + 1, (i + 1) % 2)

    slot = i % 2

    @pl.when(used(i))
    def _():
        def wait(r, c):
            row_copy(r, slot, 0).wait()
            return c

        lax.fori_loop(0, tb, wait, 0, unroll=8)
        o_ref[...] = buf[slot].reshape(o_ref.shape).astype(o_ref.dtype)

    @pl.when(jnp.logical_not(used(i)))
    def _():
        o_ref[...] = jnp.zeros_like(o_ref)


def _gather_rows(slot1, slot2, tile_expert, offs, counts, n_used, src3, n_slot, bm, tb=256):
    n_tok, dl, _ = src3.shape
    tb = min(tb, bm)
    n_tiles = n_slot // tb
    return pl.pallas_call(
        functools.partial(_gather_kernel, tb=tb, bm=bm, n_tiles=n_tiles, n_tok=n_tok),
        grid_spec=pltpu.PrefetchScalarGridSpec(
            num_scalar_prefetch=6,
            grid=(n_tiles,),
            in_specs=[pl.BlockSpec(memory_space=pl.ANY)],
            out_specs=pl.BlockSpec((tb, dl * LANE), lambda i, *_: (i, 0)),
            scratch_shapes=[pltpu.SMEM((n_slot,), jnp.int32), pltpu.VMEM((2, tb, dl, LANE), F32),
                            pltpu.SemaphoreType.DMA((2,))]),
        out_shape=jax.ShapeDtypeStruct((n_slot, dl * LANE), BF16),
        compiler_params=_params(1),
        name="gather_rows",
    )(slot1, slot2, tile_expert, offs, counts, n_used, src3)


def _new_expert(te_ref, i):
    return (i == 0) | (te_ref[i] != te_ref[jnp.maximum(i - 1, 0)])


def _expert_gateup_kernel(te_ref, nu_ref, x_ref, wg_ref, wu_ref, o_ref, wg_scr, wu_scr):
    i = pl.program_id(1)
    used = i < nu_ref[0]
    fresh = _new_expert(te_ref, i)

    @pl.when(used & fresh)
    def _():
        x = x_ref[...]
        g = _stage_and_dot(x, wg_ref, wg_scr)
        u = _stage_and_dot(x, wu_ref, wu_scr)
        o_ref[...] = _swiglu(g, u).astype(o_ref.dtype)

    @pl.when(used & jnp.logical_not(fresh))
    def _():
        x = x_ref[...]
        o_ref[...] = _swiglu(_dot_staged(x, wg_scr), _dot_staged(x, wu_scr)).astype(o_ref.dtype)

    @pl.when(jnp.logical_not(used))
    def _():
        o_ref[...] = jnp.zeros_like(o_ref)


def _expert_gateup(tile_expert, n_used, xs, wg4, wu4, layer, bm, bn=512):
    p, d = xs.shape
    ef = wg4.shape[3]
    bn = _blk(ef, bn)
    w_spec = pl.BlockSpec((None, 1, d, bn), lambda j, i, te, nu: (layer, te[i], 0, j))
    return pl.pallas_call(
        _expert_gateup_kernel,
        grid_spec=pltpu.PrefetchScalarGridSpec(
            num_scalar_prefetch=2,
            grid=(ef // bn, p // bm),
            in_specs=[pl.BlockSpec((bm, d), lambda j, i, te, nu: (i, 0)), w_spec, w_spec],
            out_specs=pl.BlockSpec((bm, bn), lambda j, i, te, nu: (i, j)),
            scratch_shapes=[pltpu.VMEM((d, bn), BF16), pltpu.VMEM((d, bn), BF16)]),
        out_shape=jax.ShapeDtypeStruct((p, ef), BF16),
        compiler_params=_params(2),
        name="expert_gateup",
    )(tile_expert, n_used, xs, wg4, wu4)


def _expert_down_kernel(te_ref, nu_ref, a_ref, w_ref, o_ref, w_scr):
    i = pl.program_id(1)
    used = i < nu_ref[0]
    fresh = _new_expert(te_ref, i)

    @pl.when(used & fresh)
    def _():
        o_ref[...] = _stage_and_dot(a_ref[...], w_ref, w_scr).reshape(o_ref.shape)

    @pl.when(used & jnp.logical_not(fresh))
    def _():
        o_ref[...] = _dot_staged(a_ref[...], w_scr).reshape(o_ref.shape)

    @pl.when(jnp.logical_not(used))
    def _():
        o_ref[...] = jnp.zeros_like(o_ref)


def _expert_down(tile_expert, n_used, a, wd4, layer, bm, bn=1024):
    p, ef = a.shape
    d = wd4.shape[3]
    bn = _blk(d, bn, 8 * LANE)
    return pl.pallas_call(
        _expert_down_kernel,
        grid_spec=pltpu.PrefetchScalarGridSpec(
            num_scalar_prefetch=2,
            grid=(d // bn, p // bm),
            in_specs=[pl.BlockSpec((bm, ef), lambda j, i, te, nu: (i, 0)),
                      pl.BlockSpec((None, 1, ef, bn), lambda j, i, te, nu: (layer, te[i], 0, j))],
            out_specs=pl.BlockSpec((bm, bn // LANE, LANE), lambda j, i, te, nu: (i, j, 0)),
            scratch_shapes=[pltpu.VMEM((ef, bn), BF16)]),
        out_shape=jax.ShapeDtypeStruct((p, d // LANE, LANE), F32),
        compiler_params=_params(2),
        name="expert_down",
    )(tile_expert, n_used, a, wd4)


def _combine_kernel(s1_ref, s2_ref, y_ref, x_ref, route_ref, o_ref, buf, sem, *, tb, n_tiles):
    i = pl.program_id(0)

    def row_copy(k, r, slot, s):
        return pltpu.make_async_copy(y_ref.at[s], buf.at[slot, k, r], sem.at[slot])

    def issue(tile, slot):
        def body(r, c):
            t = tile * tb + r
            row_copy(0, r, slot, s1_ref[t]).start()
            row_copy(1, r, slot, s2_ref[t]).start()
            return c

        lax.fori_loop(0, tb, body, 0, unroll=4)

    @pl.when(i == 0)
    def _():
        issue(0, 0)

    @pl.when(i + 1 < n_tiles)
    def _():
        issue(i + 1, (i + 1) % 2)

    slot = i % 2

    def wait(r, c):
        row_copy(0, r, slot, 0).wait()
        row_copy(1, r, slot, 0).wait()
        return c

    lax.fori_loop(0, tb, wait, 0, unroll=4)
    route = route_ref[...]
    g1 = route[:, ROUTE_G1:ROUTE_G1 + 1]
    g2 = route[:, ROUTE_G2:ROUTE_G2 + 1]
    y1 = buf[slot, 0].reshape(o_ref.shape)
    y2 = buf[slot, 1].reshape(o_ref.shape)
    o_ref[...] = x_ref[...] + (g1 * y1 + g2 * y2)


def _combine(slot1, slot2, y3, x2, route, tb=128):
    n, d = x2.shape
    dl = d // LANE
    tb = _blk(n, tb, 8)
    n_tiles = n // tb
    return pl.pallas_call(
        functools.partial(_combine_kernel, tb=tb, n_tiles=n_tiles),
        grid_spec=pltpu.PrefetchScalarGridSpec(
            num_scalar_prefetch=2,
            grid=(n_tiles,),
            in_specs=[pl.BlockSpec(memory_space=pl.ANY),
                      pl.BlockSpec((tb, d), lambda i, s1, s2: (i, 0)),
                      pl.BlockSpec((tb, LANE), lambda i, s1, s2: (i, 0))],
            out_specs=pl.BlockSpec((tb, d), lambda i, s1, s2: (i, 0)),
            scratch_shapes=[pltpu.VMEM((2, 2, tb, dl, LANE), F32), pltpu.SemaphoreType.DMA((2,))]),
        out_shape=jax.ShapeDtypeStruct((n, d), F32),
        compiler_params=_params(1),
        name="expert_combine",
    )(slot1, slot2, y3, x2, route)


def _moe(x2, h3, w_router_padded, wg4, wu4, wd4, layer, bm=512):
    n = x2.shape[0]
    n_exp = wg4.shape[1]
    bm = _blk(n, bm, 8)
    route, cnt = _router(h3, w_router_padded, n_exp)
    counts = cnt[0, :n_exp].astype(jnp.int32)
    padded = ((counts + bm - 1) // bm) * bm
    ends = jnp.cumsum(padded)
    offs = ends - padded
    idx1 = route[:, ROUTE_IDX1].astype(jnp.int32)
    idx2 = route[:, ROUTE_IDX2].astype(jnp.int32)
    slot1 = offs[idx1] + route[:, ROUTE_RANK1].astype(jnp.int32)
    slot2 = offs[idx2] + route[:, ROUTE_RANK2].astype(jnp.int32)
    n_slot = n * TOP_K + n_exp * bm
    n_tiles = n_slot // bm
    tile_start = jnp.arange(n_tiles, dtype=jnp.int32) * bm
    tile_expert = jnp.minimum(jnp.sum((ends[None, :] <= tile_start[:, None]).astype(jnp.int32), axis=1), n_exp - 1)
    n_used = (ends[-1] // bm).reshape(1)
    xs = _gather_rows(slot1, slot2, tile_expert, offs, counts, n_used, h3, n_slot, bm)
    a = _expert_gateup(tile_expert, n_used, xs, wg4, wu4, layer, bm)
    y3 = _expert_down(tile_expert, n_used, a, wd4, layer, bm)
    return _combine(slot1, slot2, y3, x2, route)


def _swap(y, kind):
    if kind == "half64":
        return pltpu.roll(y, 64, 1)
    lane = lax.broadcasted_iota(jnp.int32, y.shape, 1)
    lo = pltpu.roll(y, 96, 1)
    hi = pltpu.roll(y, 32, 1)
    return jnp.where((lane & 63) < 32, lo, hi)


def _norm_rope(chunks, gains, cos, sin, rope_flags, swap_kind, dim, transpose=False):
    ssq = None
    for c in chunks:
        t = jnp.sum(c * c, axis=-1, keepdims=True)
        ssq = t if ssq is None else ssq + t
    r = lax.rsqrt(ssq / dim + EPS)
    outs = []
    for c, g, flag in zip(chunks, gains, rope_flags):
        y = (c * r) * g
        if flag:
            y = y * cos + _swap(y, swap_kind) * sin
        outs.append((y.T if transpose else y).astype(BF16))
    return outs


def _nbr_bias(t_ref, rm_ref, group, win_minus_q, kw):
    rows = []
    for j in range(C_QROWS):
        slabs = []
        for c in range(kw // LANE):
            d = win_minus_q + (C_WIN_R - 1) + 2 * c - j
            slabs.append(t_ref[0, d - NBR_D_LO])
        rows.append(jnp.concatenate(slabs, axis=1) + rm_ref[group, j:j + 1, :])
    return rows[0] if len(rows) == 1 else jnp.concatenate(rows, axis=0)


def _attn_kernel(*refs, n_grp, n_chunk, rope_flags, swap_kind, dim, scale, bq, rc, seq, kw, window, tmode,
                 k_major):
    has_rope = any(rope_flags)
    it = iter(refs)
    q_ref = next(it)
    k_refs = [next(it) for _ in range(n_chunk)]
    v_ref = next(it)
    gq_ref = next(it)
    gk_ref = next(it)
    cos_ref = next(it) if has_rope else None
    sin_ref = next(it) if has_rope else None
    t_ref = next(it) if tmode else None
    rm_ref = next(it) if tmode == "nbr" else None
    o_ref = next(it)
    k_scr = next(it)
    v_scr = next(it)
    qb = pl.program_id(2)

    @pl.when(qb == 0)
    def _():
        step = min(seq, 512)
        gains = [gk_ref[:, c * LANE:(c + 1) * LANE] for c in range(n_chunk)]
        for r0 in range(0, seq, step):
            chunks = [kr[r0:r0 + step, :] for kr in k_refs]
            cos = cos_ref[r0:r0 + step, :] if has_rope else None
            sin = sin_ref[r0:r0 + step, :] if has_rope else None
            outs = _norm_rope(chunks, gains, cos, sin, rope_flags, swap_kind, dim, transpose=k_major)
            for c, o in enumerate(outs):
                if k_major:
                    k_scr[c * LANE:(c + 1) * LANE, r0:r0 + step] = o
                else:
                    k_scr[r0:r0 + step, c * LANE:(c + 1) * LANE] = o
            v_scr[r0:r0 + step, :LANE] = v_ref[r0:r0 + step, :].astype(BF16)
            v_scr[r0:r0 + step, LANE:] = jnp.ones((step, LANE), BF16)

    if window == "full":
        kwin = k_scr[...]
        vwin = v_scr[...]

    row0 = pl.multiple_of(qb * bq, bq)
    gains = [gq_ref[:, c * LANE:(c + 1) * LANE] for c in range(n_chunk)]
    for r0 in range(0, bq, rc):
        cos = cos_ref[pl.ds(row0 + r0, rc), :] if has_rope else None
        sin = sin_ref[pl.ds(row0 + r0, rc), :] if has_rope else None
        if window == "rows":
            q_row = r0 // GRID_W
            start_row = min(max(q_row - C_WIN_R // 2, 0), seq // GRID_W - kw // GRID_W)
            kwin = k_scr[start_row * GRID_W:start_row * GRID_W + kw, :]
            vwin = v_scr[start_row * GRID_W:start_row * GRID_W + kw, :]
        for g in range(n_grp):
            base = g * n_chunk
            chunks = [q_ref[r0:r0 + rc, (base + c) * LANE:(base + c + 1) * LANE] for c in range(n_chunk)]
            outs = _norm_rope(chunks, gains, cos, sin, rope_flags, swap_kind, dim)
            q = outs[0] if n_chunk == 1 else jnp.concatenate(outs, axis=1)
            if k_major:
                s = jnp.dot(q, kwin, preferred_element_type=F32)
            else:
                s = lax.dot_general(q, kwin, (((1,), (1,)), ((), ())), preferred_element_type=F32)
            if tmode == "nbr":
                t = _nbr_bias(t_ref, rm_ref, q_row // C_QROWS, start_row - q_row, kw)
                s = jnp.where(t > -1e29, s * scale + t, NEG_INF)
                p = jnp.exp(s - jnp.max(s, axis=-1, keepdims=True))
            else:
                if tmode == "mul":
                    t = t_ref[r0:r0 + rc, :]
                    s = jnp.where(t > 0.0, s, NEG_INF)
                p = jnp.exp2((s - jnp.max(s, axis=-1, keepdims=True)) * (scale * LOG2_E))
                if tmode == "mul":
                    p = p * t
            o = jnp.dot(p.astype(BF16), vwin, preferred_element_type=F32)
            o_ref[r0:r0 + rc, g * LANE:(g + 1) * LANE] = o[:, :LANE] / o[:, LANE:]


def _attention(q_arr, q_col0, k_arrs, v_arr, v_col0, gq, gk, cos, sin, table, *, batch, seq, n_kv, n_grp,
               rope_flags, swap_kind, dim, bq, rc, window, tmode, name):
    n_chunk = len(k_arrs)
    k_major = n_chunk > 1 and window == "full"
    nqb = seq // bq
    kw = seq if window == "full" else min(C_KROWS, seq // GRID_W) * GRID_W
    qw = n_grp * n_chunk * LANE
    q_blk0 = q_col0 * LANE // qw
    assert q_blk0 * qw == q_col0 * LANE
    in_specs = [pl.BlockSpec((bq, qw), lambda b, g, i: (b * nqb + i, q_blk0 + g))]
    args = [q_arr]
    for arr, col0, per_head in k_arrs:
        in_specs.append(pl.BlockSpec((seq, LANE), functools.partial(
            lambda b, g, i, col0, per_head: (b, col0 + g * per_head), col0=col0, per_head=per_head)))
        args.append(arr)
    in_specs.append(pl.BlockSpec((seq, LANE), lambda b, g, i: (b, v_col0 + g)))
    args.append(v_arr)
    in_specs.append(pl.BlockSpec((1, n_chunk * LANE), lambda b, g, i: (0, 0)))
    args.append(gq.reshape(1, n_chunk * LANE))
    in_specs.append(pl.BlockSpec((1, n_chunk * LANE), lambda b, g, i: (0, 0)))
    args.append(gk.reshape(1, n_chunk * LANE))
    if any(rope_flags):
        in_specs += [pl.BlockSpec((seq, LANE), lambda b, g, i: (0, 0))] * 2
        args += [cos, sin]
    if tmode == "nbr":
        slabs, layer, row_mask = table
        assert seq // GRID_W >= C_KROWS and 2 * GRID_W == LANE and bq == seq and rc == C_QROWS * GRID_W
        in_specs.append(pl.BlockSpec((None, 1) + slabs.shape[2:], lambda b, g, i: (layer, g, 0, 0, 0)))
        in_specs.append(pl.BlockSpec(row_mask.shape, lambda b, g, i: (0, 0, 0)))
        args += [slabs, row_mask]
    elif tmode == "mul":
        in_specs.append(pl.BlockSpec((bq, kw), lambda b, g, i: (i, 0)))
        args.append(table)
    kern = functools.partial(
        _attn_kernel, n_grp=n_grp, n_chunk=n_chunk, rope_flags=rope_flags, swap_kind=swap_kind, dim=dim,
        scale=dim ** -0.5, bq=bq, rc=min(rc, bq), seq=seq, kw=kw, window=window, tmode=tmode,
        k_major=k_major)
    return pl.pallas_call(
        kern,
        name=name,
        grid=(batch, n_kv, nqb),
        in_specs=in_specs,
        out_specs=pl.BlockSpec((bq, n_grp * LANE), lambda b, g, i: (b * nqb + i, g)),
        out_shape=jax.ShapeDtypeStruct((batch * seq, n_kv * n_grp * LANE), F32),
        scratch_shapes=[pltpu.VMEM((n_chunk * LANE, seq) if k_major else (seq, n_chunk * LANE), BF16),
                        pltpu.VMEM((seq, 2 * LANE), BF16)],
        compiler_params=_params(3),
    )(*args)


def _rope_cs(pos, half):
    inv_freq = ROPE_THETA ** (-jnp.arange(half, dtype=F32) / half)
    ang = pos.astype(F32)[:, None] * inv_freq[None, :]
    return jnp.cos(ang), jnp.sin(ang)


def _rope_tables(seq):
    t = jnp.arange(seq)
    c, s = _rope_cs(t, HEAD_DIM // 2)
    a_cos, a_sin = jnp.concatenate([c, c], -1), jnp.concatenate([-s, s], -1)
    cr, sr = _rope_cs(t // GRID_W, HEAD_DIM // 4)
    cc, sc = _rope_cs(t % GRID_W, HEAD_DIM // 4)
    b_cos = jnp.concatenate([cr, cr, cc, cc], -1)
    b_sin = jnp.concatenate([-sr, sr, -sc, sc], -1)
    cd, sd = _rope_cs(t, D_ROPE // 2)
    z = jnp.zeros((seq, LANE - D_ROPE), F32)
    d_cos = jnp.concatenate([cd, cd, z], -1)
    d_sin = jnp.concatenate([-sd, sd, z], -1)
    return (a_cos, a_sin), (b_cos, b_sin), (d_cos, d_sin)


def _dilation_multiplicity(seq):
    t = jnp.arange(seq)
    delta = t[None, :] - t[:, None]
    mult = jnp.zeros((seq, seq), jnp.int32)
    for window, d in A_PATTERNS:
        half = window // (2 * d)
        mult = mult + ((delta % d == 0) & (jnp.abs(delta) <= half * d)).astype(jnp.int32)
    return mult.astype(F32)


def _neighbourhood_tables(rpb, seq):
    depth, heads = rpb.shape[:2]
    rows = seq // GRID_W
    wr = min(C_WIN_R, rows)
    kr = min(C_KROWS, rows)
    nqb = rows // C_QROWS
    c = jnp.arange(GRID_W)
    col_start = jnp.clip(c - C_WIN_C // 2, 0, GRID_W - C_WIN_C)
    col_ok = (c[None, :] >= col_start[:, None]) & (c[None, :] < col_start[:, None] + C_WIN_C)
    dcol = jnp.clip(c[None, :] - c[:, None] + (C_WIN_C - 1), 0, 2 * C_WIN_C - 2)
    by_col = jnp.take(rpb, dcol.reshape(-1), axis=3, mode="clip")
    by_col = by_col.reshape(depth, heads, 2 * C_WIN_R - 1, GRID_W, GRID_W)
    by_col = jnp.where(col_ok, by_col, NEG_INF)
    lo_pad = -NBR_D_LO
    hi_pad = NBR_D_HI + 1 - (2 * C_WIN_R - 2)
    by_col = jnp.pad(by_col, ((0, 0), (0, 0), (lo_pad, hi_pad), (0, 0), (0, 0)), constant_values=NEG_INF)
    slabs = jnp.concatenate([by_col[:, :, :-1], by_col[:, :, 1:]], axis=-1)

    qb = jnp.arange(nqb)
    r = qb[:, None] * C_QROWS + jnp.arange(C_QROWS)[None, :]
    win0 = jnp.clip(qb * C_QROWS - C_WIN_R // 2, 0, rows - kr)
    krow = win0[:, None] + jnp.arange(kr)[None, :]
    row_start = jnp.clip(r - wr // 2, 0, rows - wr)
    row_ok = (krow[:, None, :] >= row_start[:, :, None]) & (krow[:, None, :] < row_start[:, :, None] + wr)
    row_mask = jnp.where(row_ok, 0.0, NEG_INF).astype(F32)
    row_mask = jnp.repeat(row_mask, GRID_W, axis=2)
    row_mask = jnp.pad(row_mask, ((0, 0), (0, 8 - C_QROWS), (0, 0)))
    return slabs, row_mask


def _mixer(x2, h, l, w_in_t, w_out, lw, tabs, batch, seq):
    d_model = x2.shape[1]
    gw = d_model // N_MIXERS
    heads = gw // HEAD_DIM
    kv_heads = heads // 4
    nb = gw // LANE
    q_rank = lw["g_cq"].shape[0]
    (a_cos, a_sin), (b_cos, b_sin), (d_cos, d_sin) = tabs["rope"]

    col = {"qa": 0, "ka": nb, "va": 2 * nb, "qb": 3 * nb, "kb": 4 * nb, "vb": 4 * nb + kv_heads}
    col["qc"] = 4 * nb + 2 * kv_heads
    col["kc"] = col["qc"] + nb
    col["vc"] = col["kc"] + nb
    col["cq"] = col["vc"] + nb
    col["ckv"] = col["cq"] + q_rank // LANE
    n_main = (col["ckv"] + D_KV_RANK // LANE) * LANE
    proj = _matmul_ws(h, w_in_t, l, n_main, w_rows_are_cols=True, name="in_proj")
    kpe = _matmul_ws(h, w_in_t, l, LANE, bn=LANE, col0=n_main, w_rows_are_cols=True, name="kpe_proj")

    g_abc = lw["g_qk_abc"]
    common = dict(batch=batch, seq=seq)

    oa = _attention(proj, col["qa"], [(proj, col["ka"], 1)], proj, col["va"], g_abc[0, 0], g_abc[0, 1],
                    a_cos, a_sin, tabs["mult"], n_kv=heads, n_grp=1, rope_flags=(True,), swap_kind="half64",
                    dim=HEAD_DIM, bq=min(2048, seq), rc=ATTN_ROWS, window="full", tmode="mul", name="attn_a",
                    **common)
    ob = _attention(proj, col["qb"], [(proj, col["kb"], 1)], proj, col["vb"], g_abc[1, 0], g_abc[1, 1],
                    b_cos, b_sin, None, n_kv=kv_heads, n_grp=4, rope_flags=(True,), swap_kind="half32",
                    dim=HEAD_DIM, bq=min(1024, seq), rc=ATTN_ROWS, window="full", tmode=None, name="attn_b",
                    **common)
    oc = _attention(proj, col["qc"], [(proj, col["kc"], 1)], proj, col["vc"], g_abc[2, 0], g_abc[2, 1],
                    None, None, (tabs["nbr"][0], l, tabs["nbr"][1]), n_kv=heads, n_grp=1, rope_flags=(False,),
                    swap_kind=None, dim=HEAD_DIM, bq=seq, rc=C_QROWS * GRID_W, window="rows", tmode="nbr",
                    name="attn_c", **common)

    qd = _norm_matmul(_pieces(proj, col["cq"] * LANE, q_rank), lw["g_cq"], lw["w_uq"], name="uq_proj")
    kvd = _norm_matmul(_pieces(proj, col["ckv"] * LANE, D_KV_RANK), lw["g_ckv"], lw["w_ukv"], name="ukv_proj")
    od = _attention(qd, 0, [(kvd, 0, 1), (kpe, 0, 0)], kvd, heads, lw["gq_mla"], lw["gk_mla"],
                    d_cos, d_sin, None, n_kv=heads, n_grp=1, rope_flags=(False, True), swap_kind="half32",
                    dim=D_QK, bq=min(2048, seq), rc=ATTN_ROWS, window="full", tmode=None, name="attn_d",
                    **common)

    normed = _rmsnorm([_pieces(o, 0, gw) for o in (oa, ob, oc, od)], lw["g_grp"])
    return _matmul_ws(normed, w_out, l, d_model, res=x2, name="out_proj")


def _layer_weights(l, d_model, g_qk_abc, w_uq, w_ukv, g_cq, g_ckv, g_qk_mla, g_grp):
    heads = d_model // N_MIXERS // HEAD_DIM
    q_rank = w_uq.shape[1]
    uq = w_uq[l].reshape(q_rank, heads, D_QK)
    uq = jnp.pad(uq, ((0, 0), (0, 0), (0, 2 * LANE - D_QK))).reshape(q_rank, heads * 2 * LANE).astype(BF16)
    ukv = w_ukv[l].reshape(D_KV_RANK, heads, D_NOPE + HEAD_DIM)
    ukv = jnp.concatenate([ukv[:, :, :D_NOPE].reshape(D_KV_RANK, heads * D_NOPE),
                           ukv[:, :, D_NOPE:].reshape(D_KV_RANK, heads * HEAD_DIM)], axis=1).astype(BF16)
    pad_g = lambda g: jnp.pad(g, (0, 2 * LANE - D_QK))
    return dict(w_uq=uq, w_ukv=ukv, g_qk_abc=g_qk_abc[l], g_cq=g_cq[l], g_ckv=g_ckv[l],
                gq_mla=pad_g(g_qk_mla[l, 0]), gk_mla=pad_g(g_qk_mla[l, 1]), g_grp=g_grp[l])


def kernel(x, g_mix, w_in, g_qk_abc, w_uq, w_ukv, g_cq, g_ckv, g_qk_mla, rpb, g_grp, w_out, g_ffn, w_gate,
           w_up, w_down, w_router, we_gate, we_up, we_down):
    batch, seq, d_model = x.shape
    depth = g_mix.shape[0]
    n_exp = we_gate.shape[1]
    tabs = dict(rope=_rope_tables(seq), mult=_dilation_multiplicity(seq), nbr=_neighbourhood_tables(rpb, seq))
    x2 = x.reshape(batch * seq, d_model)
    w_in_t = jnp.swapaxes(w_in, 1, 2)
    for l in range(depth):
        lw = _layer_weights(l, d_model, g_qk_abc, w_uq, w_ukv, g_cq, g_ckv, g_qk_mla, g_grp)
        h = _rmsnorm([_pieces(x2, 0, d_model)], g_mix[l])
        x2 = _mixer(x2, h, l, w_in_t, w_out, lw, tabs, batch, seq)
        i = l // 2
        if l % 2 == 0:
            h = _rmsnorm([_pieces(x2, 0, d_model)], g_ffn[l])
            a = _gateup_dense(h, w_gate, w_up, i)
            x2 = _matmul(a, w_down, res=x2, bn=256, bk=w_down.shape[1] // 2, layer=i, name="ffn_down")
        else:
            h3 = _rmsnorm([_pieces(x2, 0, d_model)], g_ffn[l], split_rows=True)
            wr = jnp.pad(w_router[i], ((0, 0), (0, LANE - n_exp))).astype(BF16)
            x2 = _moe(x2, h3, wr, we_gate, we_up, we_down, i)
    return x2.reshape(batch, seq, d_model)
```
